```python
import jax, jax.numpy as jnp
from jax import lax
import numpy as np

D_MODEL = 1024
BATCH = 2
SEQ = 8192
DEPTH = 2
DEC_BATCH = 4
DEC_SEQ = 8192
PAST_LEN = 128

N_EVEN = (DEPTH + 1) // 2
N_ODD = DEPTH // 2
D_FF = 4 * D_MODEL
RMS_EPS = 1e-6
GRID_W = 64
Q_BLOCK = 128
RWKV_DIM = D_MODEL // 2
RWKV_HEAD = 64
RWKV_HEADS = RWKV_DIM // RWKV_HEAD
DECAY_LORA = 64
ICLR_LORA = 64
GATE_LORA = 128
GN_EPS = 64e-5
HEAD_DIM = 64
N_Q_HEADS = (D_MODEL // 2) // HEAD_DIM
N_KV_HEADS = 2
GQA_GROUP = N_Q_HEADS // N_KV_HEADS
ROPE_THETA = 10000.0
AXIS_PAIRS = HEAD_DIM // 4
S5_GROUP = 16
S5_GROUPS = D_MODEL // S5_GROUP
S5_STATE = 64
RWKV_SPLITS = (RWKV_DIM, RWKV_DIM, RWKV_DIM, DECAY_LORA, DECAY_LORA, ICLR_LORA, GATE_LORA)
RWKV_IN_WIDTH = sum(RWKV_SPLITS)
ATT_SPLITS = (N_Q_HEADS * HEAD_DIM, N_KV_HEADS * HEAD_DIM, N_KV_HEADS * HEAD_DIM)
IN_WIDTH = RWKV_IN_WIDTH + sum(ATT_SPLITS)

kernel_name = 'hybrid_rwkv7_axialgqa_s5_encoder'


def _split(z, sizes):
    return jnp.split(z, [int(s) for s in np.cumsum(sizes)[:-1]], axis=-1)


def _rms_norm(x, g):
    xf = x.astype(jnp.float32)
    y = xf * lax.rsqrt(jnp.mean(jnp.square(xf), axis=-1, keepdims=True) + RMS_EPS)
    return (y * g.astype(jnp.float32)).astype(x.dtype)


def _sqrelu_mlp(h, w_up, w_down):
    return jnp.square(jax.nn.relu(h @ w_up)) @ w_down


def _centred_shift(h, mu):
    prev = jnp.pad(h[:, :-1], ((0, 0), (1, 0), (0, 0)))
    nxt = jnp.pad(h[:, 1:], ((0, 0), (0, 1), (0, 0)))
    return h + mu * (0.5 * (prev + nxt) - h)


def _wkv_scan(r, w, k, v, a, b):
    bsz, _, nh, n = r.shape
    xs = tuple(jnp.moveaxis(z.astype(jnp.float32), 1, 0) for z in (r, w, k, v, a, b))

    def step(S, inp):
        r_t, w_t, k_t, v_t, a_t, b_t = inp
        sa = jnp.einsum('bhvk,bhk->bhv', S, a_t)
        S = S * w_t[:, :, None, :] + sa[..., None] * b_t[:, :, None, :] + v_t[..., None] * k_t[:, :, None, :]
        return S, jnp.einsum('bhvk,bhk->bhv', S, r_t)

    S0 = jnp.zeros((bsz, nh, n, n), jnp.float32)
    _, y = lax.scan(step, S0, xs)
    return jnp.moveaxis(y, 0, 1)


def _rwkv7_bidir(h, p, i):
    bsz, T, _ = h.shape
    f32 = jnp.float32
    h = _centred_shift(h, p['hyb_shift_mu'][i])
    r, k, v, hw_f, hw_b, ha, hg = _split(h, RWKV_SPLITS)

    def heads(z):
        return z.astype(f32).reshape(bsz, T, RWKV_HEADS, RWKV_HEAD)

    def decay(w0, w_up, hw):
        wl = (w0 + jnp.tanh(hw) @ w_up).astype(f32)
        return heads(jnp.exp(-jnp.exp(-jax.nn.softplus(-wl) - 0.5)))

    w_f = decay(p['rwkv_w0_f'][i], p['rwkv_w_up_f'][i], hw_f)
    w_b = decay(p['rwkv_w0_b'][i], p['rwkv_w_up_b'][i], hw_b)
    a = jax.nn.sigmoid((p['rwkv_a0'][i] + ha @ p['rwkv_a_up'][i]).astype(f32))
    g = jax.nn.sigmoid(hg) @ p['rwkv_g_up'][i]
    kk = heads(k * p['rwkv_k_k'][i])
    kk = kk / jnp.maximum(jnp.sqrt(jnp.sum(jnp.square(kk), axis=-1, keepdims=True)), 1e-12)
    k = heads(k.astype(f32) * (1.0 + (a - 1.0) * p['rwkv_k_a'][i].astype(f32)))
    r = heads(r)
    v = heads(v)
    rem = -kk
    add = kk * heads(a)
    flip = lambda z: jnp.flip(z, axis=1)
    y = _wkv_scan(r, w_f, k, v, rem, add) + flip(
        _wkv_scan(flip(r), flip(w_b), flip(k), flip(v), flip(rem), flip(add)))
    mean = jnp.mean(y, axis=-1, keepdims=True)
    var = jnp.mean(jnp.square(y - mean), axis=-1, keepdims=True)
    y = ((y - mean) * lax.rsqrt(var + GN_EPS)).reshape(bsz, T, RWKV_DIM)
    y = y * p['rwkv_lnx_g'][i].astype(f32) + p['rwkv_lnx_b'][i].astype(f32)
    bonus = jnp.sum(r * k * p['rwkv_r_k'][i].astype(f32), axis=-1, keepdims=True) * v
    y = (y + bonus.reshape(bsz, T, RWKV_DIM)) * g.astype(f32)
    return y.astype(h.dtype)


def _axial_rope(T):
    rows = T // GRID_W
    row_ids = jnp.repeat(jnp.arange(rows, dtype=jnp.float32), GRID_W)
    col_ids = jnp.tile(jnp.arange(GRID_W, dtype=jnp.float32), rows)
    inv_freq = ROPE_THETA ** (-jnp.arange(AXIS_PAIRS, dtype=jnp.float32) / AXIS_PAIRS)
    ang = jnp.concatenate([row_ids[:, None] * inv_freq, col_ids[:, None] * inv_freq], axis=-1)
    return jnp.cos(ang), jnp.sin(ang)


def _apply_rope(x, cos, sin):
    xf = x.astype(jnp.float32).reshape(*x.shape[:-1], HEAD_DIM // 2, 2)
    x1, x2 = xf[..., 0], xf[..., 1]
    c = cos[None, :, None, :]
    s = sin[None, :, None, :]
    out = jnp.stack([x1 * c - x2 * s, x1 * s + x2 * c], axis=-1)
    return out.reshape(x.shape).astype(x.dtype)


def _block_attention(q, k, v):
    bsz, T = q.shape[:2]
    nb = T // Q_BLOCK
    qb = q.reshape(bsz, nb, Q_BLOCK, N_KV_HEADS, GQA_GROUP, HEAD_DIM).transpose(1, 0, 2, 3, 4, 5)
    scale = HEAD_DIM ** -0.5

    def one_block(qblk):
        s = jnp.einsum('bqhgd,bkhd->bhgqk', qblk, k).astype(jnp.float32) * scale
        pr = jax.nn.softmax(s, axis=-1).astype(v.dtype)
        return jnp.einsum('bhgqk,bkhd->bqhgd', pr, v)

    o = lax.map(one_block, qb)
    return o.transpose(1, 0, 2, 3, 4, 5).reshape(bsz, T, N_Q_HEADS * HEAD_DIM)


def _axial_gqa(h, p, i):
    bsz, T, _ = h.shape
    q, k, v = _split(h, ATT_SPLITS)
    q = _rms_norm(q.reshape(bsz, T, N_Q_HEADS, HEAD_DIM), p['att_q_norm'][i])
    k = _rms_norm(k.reshape(bsz, T, N_KV_HEADS, HEAD_DIM), p['att_k_norm'][i])
    v = v.reshape(bsz, T, N_KV_HEADS, HEAD_DIM)
    cos, sin = _axial_rope(T)
    return _block_attention(_apply_rope(q, cos, sin), _apply_rope(k, cos, sin), v)


def _hybrid_mixer(hn, p, i):
    h = hn @ p['hyb_w_in'][i]
    y_a = _rwkv7_bidir(h[..., :RWKV_IN_WIDTH], p, i)
    y_b = _axial_gqa(h[..., RWKV_IN_WIDTH:], p, i)
    return jnp.concatenate([y_a, y_b], axis=-1) @ p['hyb_w_out'][i]


def _s5_scan(bu_re, bu_im, lam_re, lam_im, log_dt, reverse):
    dt = jnp.exp(log_dt.astype(jnp.float32))[:, None]
    lam_re = lam_re.astype(jnp.float32)
    lam_im = lam_im.astype(jnp.float32)
    mag = jnp.exp(lam_re * dt)
    lb_re, lb_im = mag * jnp.cos(lam_im * dt), mag * jnp.sin(lam_im * dt)
    nr, ni = lb_re - 1.0, lb_im
    den = jnp.square(lam_re) + jnp.square(lam_im)
    c_re = (nr * lam_re + ni * lam_im) / den
    c_im = (ni * lam_re - nr * lam_im) / den
    b_re = c_re * bu_re - c_im * bu_im
    b_im = c_re * bu_im + c_im * bu_re
    a_re = jnp.broadcast_to(lb_re, b_re.shape)
    a_im = jnp.broadcast_to(lb_im, b_re.shape)

    def combine(x, y):
        a1r, a1i, b1r, b1i = x
        a2r, a2i, b2r, b2i = y
        return (a2r * a1r - a2i * a1i, a2r * a1i + a2i * a1r,
                a2r * b1r - a2i * b1i + b2r, a2r * b1i + a2i * b1r + b2i)

    _, _, s_re, s_im = lax.associative_scan(combine, (a_re, a_im, b_re, b_im), reverse=reverse, axis=1)
    return s_re, s_im


def _s5_mixer(u, p, i):
    bsz, T, _ = u.shape
    f32 = jnp.float32
    ug = u.astype(f32).reshape(bsz, T, S5_GROUPS, S5_GROUP)
    bu_re = jnp.einsum('btgc,gpc->btgp', ug, p['s5_b_re'][i].astype(f32))
    bu_im = jnp.einsum('btgc,gpc->btgp', ug, p['s5_b_im'][i].astype(f32))
    y = u.astype(f32) * p['s5_d'][i].astype(f32)
    for sfx, rev in (('f', False), ('b', True)):
        s_re, s_im = _s5_scan(bu_re, bu_im, p['s5_lam_re_' + sfx][i], p['s5_lam_im_' + sfx][i],
                              p['s5_log_dt_' + sfx][i], rev)
        yc = (jnp.einsum('btgp,gcp->btgc', s_re, p['s5_c_re_' + sfx][i].astype(f32))
              - jnp.einsum('btgp,gcp->btgc', s_im, p['s5_c_im_' + sfx][i].astype(f32)))
        y = y + yc.reshape(bsz, T, D_MODEL)
    z = jax.nn.gelu(y)
    z = z * jax.nn.sigmoid(z @ p['s5_glu_w'][i].astype(f32) + p['s5_glu_b'][i].astype(f32))
    return z.astype(u.dtype)


def _trunk(x, p):
    for layer in range(DEPTH):
        i = layer // 2
        hn = _rms_norm(x, p['mix_norm'][layer])
        if layer % 2 == 0:
            x = x + _hybrid_mixer(hn, p, i)
        else:
            x = x + _s5_mixer(hn, p, i)
        x = x + _sqrelu_mlp(_rms_norm(x, p['ffn_norm'][layer]), p['ffn_up'][layer], p['ffn_down'][layer])
    return x


def setup_inputs(seed: int = 0) -> dict:
    key = jax.random.key(seed)
    ks = iter(jax.random.split(key, 48))

    def nrm(shape, scale):
        return scale * jax.random.normal(next(ks), shape, jnp.float32)

    def unif(shape, lo, hi):
        return jax.random.uniform(next(ks), shape, jnp.float32, lo, hi)

    E, O, L = N_EVEN, N_ODD, DEPTH
    G, P, GC = S5_GROUPS, S5_STATE, S5_GROUP
    n_idx = jnp.arange(P, dtype=jnp.float32)
    return {
        'x_prompt': nrm((BATCH, SEQ, D_MODEL), 1.0),
        'x_sample': nrm((DEC_BATCH, DEC_SEQ, D_MODEL), 1.0),
        'mix_norm': 1.0 + nrm((L, D_MODEL), 0.02),
        'ffn_norm': 1.0 + nrm((L, D_MODEL), 0.02),
        'ffn_up': nrm((L, D_MODEL, D_FF), D_MODEL ** -0.5),
        'ffn_down': nrm((L, D_FF, D_MODEL), D_FF ** -0.5),
        'hyb_w_in': nrm((E, D_MODEL, IN_WIDTH), D_MODEL ** -0.5),
        'hyb_shift_mu': unif((E, RWKV_IN_WIDTH), 0.0, 1.0),
        'rwkv_w0_f': unif((E, RWKV_DIM), -6.0, 1.0),
        'rwkv_w_up_f': nrm((E, DECAY_LORA, RWKV_DIM), 0.1),
        'rwkv_w0_b': unif((E, RWKV_DIM), -6.0, 1.0),
        'rwkv_w_up_b': nrm((E, DECAY_LORA, RWKV_DIM), 0.1),
        'rwkv_a0': nrm((E, RWKV_DIM), 0.1),
        'rwkv_a_up': nrm((E, ICLR_LORA, RWKV_DIM), ICLR_LORA ** -0.5),
        'rwkv_g_up': nrm((E, GATE_LORA, RWKV_DIM), GATE_LORA ** -0.5),
        'rwkv_k_k': 0.85 + nrm((E, RWKV_DIM), 0.02),
        'rwkv_k_a': 1.0 + nrm((E, RWKV_DIM), 0.02),
        'rwkv_r_k': nrm((E, RWKV_HEADS, RWKV_HEAD), 0.1),
        'rwkv_lnx_g': 1.0 + nrm((E, RWKV_DIM), 0.02),
        'rwkv_lnx_b': nrm((E, RWKV_DIM), 0.02),
        'att_q_norm': 1.0 + nrm((E, HEAD_DIM), 0.02),
        'att_k_norm': 1.0 + nrm((E, HEAD_DIM), 0.02),
        'hyb_w_out': nrm((E, D_MODEL, D_MODEL), D_MODEL ** -0.5),
        's5_lam_re_f': -0.5 + nrm((O, G, P), 0.01),
        's5_lam_im_f': jnp.pi * n_idx + nrm((O, G, P), 0.01),
        's5_log_dt_f': unif((O, G), float(np.log(1e-3)), float(np.log(1e-1))),
        's5_lam_re_b': -0.5 + nrm((O, G, P), 0.01),
        's5_lam_im_b': jnp.pi * n_idx + nrm((O, G, P), 0.01),
        's5_log_dt_b': unif((O, G), float(np.log(1e-3)), float(np.log(1e-1))),
        's5_b_re': nrm((O, G, P, GC), (2 * GC) ** -0.5),
        's5_b_im': nrm((O, G, P, GC), (2 * GC) ** -0.5),
        's5_c_re_f': nrm((O, G, GC, P), P ** -0.5),
        's5_c_im_f': nrm((O, G, GC, P), P ** -0.5),
        's5_c_re_b': nrm((O, G, GC, P), P ** -0.5),
        's5_c_im_b': nrm((O, G, GC, P), P ** -0.5),
        's5_d': nrm((O, D_MODEL), 1.0),
        's5_glu_w': nrm((O, D_MODEL, D_MODEL), D_MODEL ** -0.5),
        's5_glu_b': nrm((O, D_MODEL), 0.02),
    }


def reference(x_prompt, x_sample, mix_norm, ffn_norm, ffn_up, ffn_down, hyb_w_in, hyb_shift_mu,
              rwkv_w0_f, rwkv_w_up_f, rwkv_w0_b, rwkv_w_up_b, rwkv_a0, rwkv_a_up, rwkv_g_up,
              rwkv_k_k, rwkv_k_a, rwkv_r_k, rwkv_lnx_g, rwkv_lnx_b, att_q_norm, att_k_norm, hyb_w_out,
              s5_lam_re_f, s5_lam_im_f, s5_log_dt_f, s5_lam_re_b, s5_lam_im_b, s5_log_dt_b,
              s5_b_re, s5_b_im, s5_c_re_f, s5_c_im_f, s5_c_re_b, s5_c_im_b, s5_d, s5_glu_w, s5_glu_b):
    p = dict(mix_norm=mix_norm, ffn_norm=ffn_norm, ffn_up=ffn_up, ffn_down=ffn_down,
             hyb_w_in=hyb_w_in, hyb_shift_mu=hyb_shift_mu,
             rwkv_w0_f=rwkv_w0_f, rwkv_w_up_f=rwkv_w_up_f, rwkv_w0_b=rwkv_w0_b, rwkv_w_up_b=rwkv_w_up_b,
             rwkv_a0=rwkv_a0, rwkv_a_up=rwkv_a_up, rwkv_g_up=rwkv_g_up,
             rwkv_k_k=rwkv_k_k, rwkv_k_a=rwkv_k_a, rwkv_r_k=rwkv_r_k,
             rwkv_lnx_g=rwkv_lnx_g, rwkv_lnx_b=rwkv_lnx_b,
             att_q_norm=att_q_norm, att_k_norm=att_k_norm, hyb_w_out=hyb_w_out,
             s5_lam_re_f=s5_lam_re_f, s5_lam_im_f=s5_lam_im_f, s5_log_dt_f=s5_log_dt_f,
             s5_lam_re_b=s5_lam_re_b, s5_lam_im_b=s5_lam_im_b, s5_log_dt_b=s5_log_dt_b,
             s5_b_re=s5_b_re, s5_b_im=s5_b_im,
             s5_c_re_f=s5_c_re_f, s5_c_im_f=s5_c_im_f, s5_c_re_b=s5_c_re_b, s5_c_im_b=s5_c_im_b,
             s5_d=s5_d, s5_glu_w=s5_glu_w, s5_glu_b=s5_glu_b)
    y_prompt = _trunk(x_prompt, p)
    y_sample = _trunk(x_sample, p)
    return (y_prompt, y_sample)
```

```python
import functools
import math

import jax
import jax.numpy as jnp
from jax import lax
from jax.experimental import pallas as pl
from jax.experimental.pallas import tpu as pltpu

F32 = jnp.float32
BF16 = jnp.bfloat16

D_MODEL = 1024
D_FF = 4 * D_MODEL
RMS_EPS = 1e-6
GRID_W = 64
RWKV_DIM = 512
HEAD = 64
GN_EPS = 64e-5
N_KV = 2
ROPE_THETA = 10000.0
S5_GROUP = 16
S5_GROUPS = D_MODEL // S5_GROUP
S5_STATE = 64

WKV_CHUNK = 64
S5_CHUNK = 16
RW_COLS = 1920
ALL_COLS = 2688
EXP_M05 = math.exp(-0.5)
VMEM_LIMIT = 56 * 1024 * 1024


def _dot(a, b):
    return jnp.dot(a, b, preferred_element_type=F32)


def _dot_nt(a, b):
    return lax.dot_general(a, b, (((1,), (1,)), ((), ())), preferred_element_type=F32)


def _split2(x):
    hi = x.astype(BF16)
    lo = (x - hi.astype(F32)).astype(BF16)
    return hi, lo


def _seg_sum(x, ones_bd):
    hi, lo = _split2(x)
    return _dot(hi, ones_bd) + _dot(lo, ones_bd)


def _rms(x, gain):
    return x * lax.rsqrt(jnp.mean(x * x, axis=-1, keepdims=True) + RMS_EPS) * gain


def _sigmoid(x):
    return 1.0 / (1.0 + jnp.exp(-x))


def _params(sem):
    return pltpu.CompilerParams(dimension_semantics=sem, vmem_limit_bytes=VMEM_LIMIT)


def _pre0_kernel(x_ref, xp_ref, xn_ref, gain_ref, w_ref, mu_ref, wup_ref, par_ref, qg_ref, kg_ref,
                 cos_ref, sin_ref, ones_ref,
                 r_o, k_o, v_o, a_o, b_o, lf_o, lb_o, g_o, bon_o, q_o, ka_o, va_o,
                 *, tiles_per_seq, tm):
    pos = pl.program_id(0) % tiles_per_seq
    gain = gain_ref[...]
    hn = _rms(x_ref[...], gain).astype(BF16)
    H = _dot(hn, w_ref[...])
    w_rw = w_ref[:, :RW_COLS]
    h_prev = _dot(_rms(xp_ref[...], gain).astype(BF16), w_rw)[7:8]
    h_next = _dot(_rms(xn_ref[...], gain).astype(BF16), w_rw)[0:1]
    h_prev = jnp.where(pos == 0, 0.0, h_prev)
    h_next = jnp.where(pos == tiles_per_seq - 1, 0.0, h_next)
    Hr = H[:, :RW_COLS]
    row = lax.broadcasted_iota(jnp.int32, Hr.shape, 0)
    prev = jnp.where(row == 0, h_prev, pltpu.roll(Hr, 1, 0))
    nxt = jnp.where(row == tm - 1, h_next, pltpu.roll(Hr, tm - 1, 0))
    Hs = Hr + mu_ref[...] * (0.5 * (prev + nxt) - Hr)

    ones = ones_ref[...]
    par = par_ref[...]
    r = Hs[:, 0:512]
    k = Hs[:, 512:1024]
    v = Hs[:, 1024:1536]
    act = jnp.concatenate(
        [jnp.tanh(Hs[:, 1536:1664]), Hs[:, 1664:1792], _sigmoid(Hs[:, 1792:1920])], axis=1).astype(BF16)
    up = _dot(act, wup_ref[...])
    lf_o[...] = -EXP_M05 * _sigmoid(par[0:1] + up[:, 0:512])
    lb_o[...] = -EXP_M05 * _sigmoid(par[1:2] + up[:, 512:1024])
    a_sig = _sigmoid(par[2:3] + up[:, 1024:1536])
    g_o[...] = up[:, 1536:2048]
    kk = k * par[3:4]
    kk = kk / jnp.maximum(jnp.sqrt(_seg_sum(kk * kk, ones)), 1e-12)
    k2 = k * (1.0 + (a_sig - 1.0) * par[4:5])
    r_o[...] = r
    k_o[...] = k2
    v_o[...] = v
    a_o[...] = -kk
    b_o[...] = kk * a_sig
    bon_o[...] = _seg_sum(r * k2 * par[5:6], ones) * v

    qa = H[:, RW_COLS:RW_COLS + 512]
    ka = H[:, RW_COLS + 512:RW_COLS + 640]
    va = H[:, RW_COLS + 640:RW_COLS + 768]
    cos = cos_ref[...]
    sin = sin_ref[...]

    def rope(x, c, s):
        n = x.shape[1]
        lane = lax.broadcasted_iota(jnp.int32, x.shape, 1)
        swapped = jnp.where(lane % 2 == 0, pltpu.roll(x, n - 1, 1), pltpu.roll(x, 1, 1))
        return x * c + swapped * s

    qn = qa * lax.rsqrt(_seg_sum(qa * qa, ones) * (1.0 / HEAD) + RMS_EPS) * qg_ref[...]
    qr = rope(qn, jnp.concatenate([cos] * 4, axis=1), jnp.concatenate([sin] * 4, axis=1))
    q_o[...] = (qr * (HEAD ** -0.5)).astype(BF16)
    kn = ka * lax.rsqrt(_seg_sum(ka * ka, ones[:128, :128]) * (1.0 / HEAD) + RMS_EPS) * kg_ref[...]
    kr = rope(kn, cos, sin)
    lt64 = lax.broadcasted_iota(jnp.int32, kr.shape, 1) < HEAD

    def rep(x):
        sw = pltpu.roll(x, HEAD, 1)
        return jnp.concatenate([jnp.where(lt64, x, sw), jnp.where(lt64, sw, x)], axis=1).astype(BF16)

    ka_o[...] = rep(kr)
    va_o[...] = rep(va)


def _pre0(x2d, T, gain, w_all, mu_all, wup, par, qg, kg, cos_t, sin_t, ones_bd):
    N = x2d.shape[0]
    tm = min(256, T)
    tps = T // tm
    nt = N // tm
    t8 = tm // 8
    nb8 = N // 8
    const = lambda i: (0, 0)
    tile = lambda i: (i, 0)
    f512 = jax.ShapeDtypeStruct((N, 512), F32)
    out_shape = [f512] * 9 + [jax.ShapeDtypeStruct((N, 512), BF16),
                              jax.ShapeDtypeStruct((N, 256), BF16), jax.ShapeDtypeStruct((N, 256), BF16)]
    out_specs = [pl.BlockSpec((tm, 512), tile)] * 10 + [pl.BlockSpec((tm, 256), tile)] * 2
    return pl.pallas_call(
        functools.partial(_pre0_kernel, tiles_per_seq=tps, tm=tm),
        grid=(nt,),
        in_specs=[
            pl.BlockSpec((tm, D_MODEL), tile),
            pl.BlockSpec((8, D_MODEL), lambda i: (jnp.maximum(i * t8 - 1, 0), 0)),
            pl.BlockSpec((8, D_MODEL), lambda i: (jnp.minimum((i + 1) * t8, nb8 - 1), 0)),
            pl.BlockSpec((1, D_MODEL), const),
            pl.BlockSpec((D_MODEL, ALL_COLS), const),
            pl.BlockSpec((1, RW_COLS), const),
            pl.BlockSpec((384, 2048), const),
            pl.BlockSpec((8, 512), const),
            pl.BlockSpec((1, 512), const),
            pl.BlockSpec((1, 128), const),
            pl.BlockSpec((tm, 128), lambda i: (i % tps, 0)),
            pl.BlockSpec((tm, 128), lambda i: (i % tps, 0)),
            pl.BlockSpec((512, 512), const),
        ],
        out_specs=out_specs,
        out_shape=out_shape,
        compiler_params=_params(("parallel",)),
    )(x2d, x2d, x2d, gain, w_all, mu_all, wup, par, qg, kg, cos_t, sin_t, ones_bd)


def _wkv_direction(r, k, v, a, b, L, h_ref, fwd):
    C = WKV_CHUNK
    Q = 4 * HEAD
    ti = lax.broadcasted_iota(jnp.int32, (C, C), 0)
    si = lax.broadcasted_iota(jnp.int32, (C, C), 1)
    tri = jnp.where(si <= ti, 1.0, 0.0).astype(BF16)
    l1 = L.astype(BF16)
    rem = L - l1.astype(F32)
    l2 = rem.astype(BF16)
    l3 = (rem - l2.astype(F32)).astype(BF16)
    cs = _dot(tri, l1) + _dot(tri, l2) + _dot(tri, l3)
    total = cs[C - 1:C, :]
    if fwd:
        cs_incl = cs
        cs_excl = cs - L
    else:
        cs_incl = total - (cs - L)
        cs_excl = total - cs
    e_incl = jnp.exp(cs_incl)
    e_inv = jnp.exp(-cs_incl)
    e_rem = jnp.exp(total - cs_incl)
    a_t = a * jnp.exp(cs_excl)
    r_t = r * e_incl
    b_t = b * e_inv
    k_t = k * e_inv
    b_h = b * e_rem
    k_h = k * e_rem
    gam = jnp.exp(total)

    lane_q = lax.broadcasted_iota(jnp.int32, (C, Q), 1) // HEAD
    ri = lax.broadcasted_iota(jnp.int32, (Q, Q), 0)
    ci = lax.broadcasted_iota(jnp.int32, (Q, Q), 1)
    same = (ri // C) == (ci // C)
    if fwd:
        m_strict = same & ((ci % C) < (ri % C))
        m_incl = same & ((ci % C) <= (ri % C))
    else:
        m_strict = same & ((ci % C) > (ri % C))
        m_incl = same & ((ci % C) >= (ri % C))
    eye = ri == ci

    def stack(xq):
        return jnp.concatenate([jnp.where(lane_q == h, xq, 0.0) for h in range(4)], axis=0)

    def unstack(xs):
        return xs[0:C] + xs[C:2 * C] + xs[2 * C:3 * C] + xs[3 * C:4 * C]

    ys = []
    for q in range(RWKV_DIM // Q):
        sl = slice(q * Q, (q + 1) * Q)
        As = stack(a_t[:, sl]).astype(BF16)
        Rs = stack(r_t[:, sl]).astype(BF16)
        Bs = stack(b_t[:, sl]).astype(BF16)
        Ks = stack(k_t[:, sl]).astype(BF16)
        Vs = stack(v[:, sl]).astype(BF16)
        BhT = stack(b_h[:, sl]).T.astype(BF16)
        KhT = stack(k_h[:, sl]).T.astype(BF16)
        P = _dot_nt(jnp.concatenate([As, Rs], axis=0), jnp.concatenate([Bs, Ks], axis=0))
        A_ab = jnp.where(m_strict, P[:Q, :Q], 0.0)
        A_ak = jnp.where(m_strict, P[:Q, Q:], 0.0).astype(BF16)
        A_rb = jnp.where(m_incl, P[Q:, :Q], 0.0).astype(BF16)
        A_rk = jnp.where(m_incl, P[Q:, Q:], 0.0).astype(BF16)
        T = jnp.where(eye, 1.0, A_ab)
        Ap = A_ab.astype(BF16)
        for _ in range(int(math.log2(C)) - 1):
            Ap = _dot(Ap, Ap).astype(BF16)
            T = T + _dot(T.astype(BF16), Ap)
        Z = _dot(A_ak, Vs)
        X = _dot(T.astype(BF16), jnp.concatenate([As, Z.astype(BF16)], axis=1)).astype(BF16)
        W1 = _dot(A_rb, X)
        r_p = r_t[:, sl] + unstack(W1[:, :Q])
        y_p = unstack(W1[:, Q:] + _dot(A_rk, Vs))
        MN = _dot(BhT, X)
        M = jnp.where(eye, gam[:, sl], 0.0) + MN[:, :Q]
        Nn = MN[:, Q:] + _dot(KhT, Vs)
        h0 = h_ref[q]
        h_hi, h_lo = _split2(h0)
        lhs_hi, lhs_lo = _split2(jnp.concatenate([r_p, M], axis=0))
        seq = _dot(lhs_hi, h_hi) + _dot(lhs_hi, h_lo) + _dot(lhs_lo, h_hi)
        ys.append(seq[:C] + y_p)
        h_ref[q] = seq[C:] + Nn
    return jnp.concatenate(ys, axis=1)


def _wkv_kernel(rf, kf, vf, af, bf, lf, rb, kb, vb, ab, bb, lb, yf_o, yb_o, hf_ref, hb_ref):
    @pl.when(pl.program_id(1) == 0)
    def _():
        hf_ref[...] = jnp.zeros_like(hf_ref)
        hb_ref[...] = jnp.zeros_like(hb_ref)

    yf_o[...] = _wkv_direction(rf[...], kf[...], vf[...], af[...], bf[...], lf[...], hf_ref, True)
    yb_o[...] = _wkv_direction(rb[...], kb[...], vb[...], ab[...], bb[...], lb[...], hb_ref, False)


def _wkv(r, k, v, a, b, lf, lb, n_seq, T):
    C = WKV_CHUNK
    nc = T // C
    fw = lambda s, i: (s * nc + i, 0)
    bw = lambda s, i: (s * nc + nc - 1 - i, 0)
    spec_f = pl.BlockSpec((C, 512), fw)
    spec_b = pl.BlockSpec((C, 512), bw)
    shp = jax.ShapeDtypeStruct(r.shape, F32)
    return pl.pallas_call(
        _wkv_kernel,
        grid=(n_seq, nc),
        in_specs=[spec_f] * 6 + [spec_b] * 6,
        out_specs=[spec_f, spec_b],
        out_shape=[shp, shp],
        scratch_shapes=[pltpu.VMEM((2, 256, 256), F32), pltpu.VMEM((2, 256, 256), F32)],
        compiler_params=_params(("parallel", "arbitrary")),
    )(r, k, v, a, b, lf, r, k, v, a, b, lb)


def _attn_kernel(q_ref, k_ref, v_ref, o_ref, *, tq, tk, T):
    q = q_ref[...]
    lane = lax.broadcasted_iota(jnp.int32, (tq, 128), 1)
    lo = lane < HEAD
    zero = jnp.zeros((), BF16)
    qs = jnp.concatenate([
        jnp.where(lo, q[:, 0:128], zero), jnp.where(lo, zero, q[:, 0:128]),
        jnp.where(lo, q[:, 128:256], zero), jnp.where(lo, zero, q[:, 128:256])], axis=0)

    def body(j, carry):
        m, l, acc = carry
        off = pl.multiple_of(j * tk, tk)
        kc = k_ref[pl.ds(off, tk), :]
        vc = v_ref[pl.ds(off, tk), :]
        s = _dot_nt(qs, kc)
        m_new = jnp.maximum(m, jnp.max(s, axis=-1, keepdims=True))
        alpha = jnp.exp(m - m_new)
        p = jnp.exp(s - m_new)
        l = alpha * l + jnp.sum(p, axis=-1, keepdims=True)
        acc = alpha * acc + _dot(p.astype(BF16), vc)
        return m_new, l, acc

    m0 = jnp.full((4 * tq, 1), -1e30, F32)
    l0 = jnp.zeros((4 * tq, 1), F32)
    acc0 = jnp.zeros((4 * tq, 128), F32)
    _, l, acc = lax.fori_loop(0, T // tk, body, (m0, l0, acc0))
    o = acc / l
    lo_f = lane < HEAD
    o_ref[...] = jnp.concatenate([
        jnp.where(lo_f, o[0:tq], o[tq:2 * tq]), jnp.where(lo_f, o[2 * tq:3 * tq], o[3 * tq:4 * tq])], axis=1)


def _attention(q, k_rep, v_rep, n_seq, T):
    tq = min(128, T)
    tk = min(512, T)
    nq = T // tq
    return pl.pallas_call(
        functools.partial(_attn_kernel, tq=tq, tk=tk, T=T),
        grid=(n_seq, N_KV, nq),
        in_specs=[
            pl.BlockSpec((tq, 256), lambda s, h, i: (s * nq + i, h)),
            pl.BlockSpec((T, 128), lambda s, h, i: (s, h)),
            pl.BlockSpec((T, 128), lambda s, h, i: (s, h)),
        ],
        out_specs=pl.BlockSpec((tq, 256), lambda s, h, i: (s * nq + i, h)),
        out_shape=jax.ShapeDtypeStruct(q.shape, F32),
        compiler_params=_params(("parallel", "parallel", "arbitrary")),
    )(q, k_rep, v_rep)


def _post0_kernel(x_ref, yf_ref, yb_ref, g_ref, bon_ref, att_ref, ln_ref, wo_ref, ones_ref, o_ref):
    ones = ones_ref[...]
    y = yf_ref[...] + yb_ref[...]
    mean = _seg_sum(y, ones) * (1.0 / HEAD)
    d = y - mean
    var = _seg_sum(d * d, ones) * (1.0 / HEAD)
    yn = d * lax.rsqrt(var + GN_EPS) * ln_ref[0:1] + ln_ref[1:2]
    ya = ((yn + bon_ref[...]) * g_ref[...]).astype(BF16)
    mix = _dot(ya, wo_ref[0:512, :]) + _dot(att_ref[...].astype(BF16), wo_ref[512:1024, :])
    o_ref[...] = x_ref[...] + mix


def _post0(x2d, yf, yb, g, bon, att, ln, wo, ones_bd):
    N = x2d.shape[0]
    tm = min(512, N)
    tile = lambda i: (i, 0)
    const = lambda i: (0, 0)
    return pl.pallas_call(
        _post0_kernel,
        grid=(N // tm,),
        in_specs=[pl.BlockSpec((tm, D_MODEL), tile)] + [pl.BlockSpec((tm, 512), tile)] * 5 + [
            pl.BlockSpec((8, 512), const), pl.BlockSpec((D_MODEL, D_MODEL), const), pl.BlockSpec((512, 512), const)],
        out_specs=pl.BlockSpec((tm, D_MODEL), tile),
        out_shape=jax.ShapeDtypeStruct(x2d.shape, F32),
        compiler_params=_params(("parallel",)),
    )(x2d, yf, yb, g, bon, att, ln, wo, ones_bd)


def _mlp_kernel(x_ref, gain_ref, wu_ref, wd_ref, o_ref, hn_ref):
    @pl.when(pl.program_id(1) == 0)
    def _():
        x = x_ref[...]
        hn_ref[...] = _rms(x, gain_ref[...]).astype(BF16)
        o_ref[...] = x

    u = _dot(hn_ref[...], wu_ref[...])
    u = jnp.maximum(u, 0.0)
    o_ref[...] += _dot((u * u).astype(BF16), wd_ref[...])


def _mlp(x2d, gain, w_up, w_down):
    N = x2d.shape[0]
    tm = min(512, N)
    fc = 1024
    return pl.pallas_call(
        _mlp_kernel,
        grid=(N // tm, D_FF // fc),
        in_specs=[
            pl.BlockSpec((tm, D_MODEL), lambda i, j: (i, 0)),
            pl.BlockSpec((1, D_MODEL), lambda i, j: (0, 0)),
            pl.BlockSpec((D_MODEL, fc), lambda i, j: (0, j)),
            pl.BlockSpec((fc, D_MODEL), lambda i, j: (j, 0)),
        ],
        out_specs=pl.BlockSpec((tm, D_MODEL), lambda i, j: (i, 0)),
        out_shape=jax.ShapeDtypeStruct(x2d.shape, F32),
        scratch_shapes=[pltpu.VMEM((tm, D_MODEL), BF16)],
        compiler_params=_params(("parallel", "arbitrary")),
    )(x2d, gain, w_up, w_down)


def _norm_kernel(x_ref, gain_ref, o_ref):
    o_ref[...] = _rms(x_ref[...], gain_ref[...]).astype(o_ref.dtype)


def _norm_bf16(x2d, gain):
    N = x2d.shape[0]
    tm = min(1024, N)
    return pl.pallas_call(
        _norm_kernel,
        grid=(N // tm,),
        in_specs=[pl.BlockSpec((tm, D_MODEL), lambda i: (i, 0)), pl.BlockSpec((1, D_MODEL), lambda i: (0, 0))],
        out_specs=pl.BlockSpec((tm, D_MODEL), lambda i: (i, 0)),
        out_shape=jax.ShapeDtypeStruct(x2d.shape, BF16),
        compiler_params=_params(("parallel",)),
    )(x2d, gain)


def _s5_kernel(u_ref, w1_ref, w2_ref, lam_ref, y_ref, z_ref, p_ref, *, gb, nc):
    for g in range(gb):
        z = _dot(u_ref[0, g], w1_ref[g])
        y_ref[0, g] = z[:, 0:256]
        z_ref[g] = z[:, 256:768]

    lam = [lam_ref[g] for g in range(gb)]

    def step(i, carry):
        new = []
        f0 = pl.multiple_of(i * 8, 8)
        b0 = pl.multiple_of(nc - 8 - i * 8, 8)
        for g in range(gb):
            xf, xfs, xb, xbs = carry[g]
            zf8 = z_ref[g, pl.ds(f0, 8), 0:256]
            zb8 = z_ref[g, pl.ds(b0, 8), 256:512]
            a_f, bv_f, bs_f = lam[g][0:1], lam[g][1:2], lam[g][2:3]
            a_b, bv_b, bs_b = lam[g][3:4], lam[g][4:5], lam[g][5:6]
            rows_f = []
            rows_b = [None] * 8
            for j in range(8):
                rows_f.append(xf)
                xf, xfs = (a_f * xf + bv_f * xfs + zf8[j:j + 1, 0:128],
                           a_f * xfs + bs_f * xf + zf8[j:j + 1, 128:256])
                jb = 7 - j
                rows_b[jb] = xb
                xb, xbs = (a_b * xb + bv_b * xbs + zb8[jb:jb + 1, 0:128],
                           a_b * xbs + bs_b * xb + zb8[jb:jb + 1, 128:256])
            p_ref[g, pl.ds(f0, 8), 0:128] = jnp.concatenate(rows_f, axis=0)
            p_ref[g, pl.ds(b0, 8), 128:256] = jnp.concatenate(rows_b, axis=0)
            new.append((xf, xfs, xb, xbs))
        return tuple(new)

    zero = jnp.zeros((1, 128), F32)
    lax.fori_loop(0, nc // 8, step, tuple((zero, zero, zero, zero) for _ in range(gb)))
    for g in range(gb):
        y_ref[0, g] += _dot(p_ref[g].astype(BF16), w2_ref[g])


def _s5_core(u4, w1, w2, lam16):
    n_seq, G, nc, _ = u4.shape
    gb = 8
    return pl.pallas_call(
        functools.partial(_s5_kernel, gb=gb, nc=nc),
        grid=(n_seq, G // gb),
        in_specs=[
            pl.BlockSpec((1, gb, nc, 256), lambda s, g: (s, g, 0, 0)),
            pl.BlockSpec((gb, 256, 768), lambda s, g: (g, 0, 0)),
            pl.BlockSpec((gb, 256, 256), lambda s, g: (g, 0, 0)),
            pl.BlockSpec((gb, 8, 128), lambda s, g: (g, 0, 0)),
        ],
        out_specs=pl.BlockSpec((1, gb, nc, 256), lambda s, g: (s, g, 0, 0)),
        out_shape=jax.ShapeDtypeStruct(u4.shape, F32),
        scratch_shapes=[pltpu.VMEM((gb, nc, 512), F32), pltpu.VMEM((gb, nc, 256), F32)],
        compiler_params=_params(("parallel", "arbitrary")),
    )(u4, w1, w2, lam16)


def _post1_kernel(x_ref, ys_ref, gain_ref, d_ref, wg_ref, bg_ref, o_ref):
    x = x_ref[...]
    y = _rms(x, gain_ref[...]) * d_ref[...] + ys_ref[...]
    z = 0.5 * y * (1.0 + jnp.tanh(math.sqrt(2.0 / math.pi) * (y + 0.044715 * (y * y * y))))
    gate = _sigmoid(_dot(z.astype(BF16), wg_ref[...]) + bg_ref[...])
    o_ref[...] = x + z * gate


def _post1(x2d, ys, gain, d, wg, bg):
    N = x2d.shape[0]
    tm = min(512, N)
    tile = lambda i: (i, 0)
    const = lambda i: (0, 0)
    return pl.pallas_call(
        _post1_kernel,
        grid=(N // tm,),
        in_specs=[pl.BlockSpec((tm, D_MODEL), tile), pl.BlockSpec((tm, D_MODEL), tile),
                  pl.BlockSpec((1, D_MODEL), const), pl.BlockSpec((1, D_MODEL), const),
                  pl.BlockSpec((D_MODEL, D_MODEL), const), pl.BlockSpec((1, D_MODEL), const)],
        out_specs=pl.BlockSpec((tm, D_MODEL), tile),
        out_shape=jax.ShapeDtypeStruct(x2d.shape, F32),
        compiler_params=_params(("parallel",)),
    )(x2d, ys, gain, d, wg, bg)


def _s5_tables(p):
    C = S5_CHUNK
    b_c = (p['s5_b_re'][0] + 1j * p['s5_b_im'][0]).astype(jnp.complex64)
    j_idx = jnp.arange(C, dtype=F32)

    def direction(sfx):
        lam = (p['s5_lam_re_' + sfx][0] + 1j * p['s5_lam_im_' + sfx][0]).astype(jnp.complex64)
        dt = jnp.exp(p['s5_log_dt_' + sfx][0].astype(F32))[:, None]
        lam_dt = lam * dt
        coef = (jnp.exp(lam_dt) - 1.0) / lam
        pw = jnp.exp(lam_dt[None] * j_idx[:, None, None].astype(jnp.complex64))
        pw_next = jnp.exp(lam_dt[None] * (j_idx[:, None, None] + 1.0).astype(jnp.complex64))
        c_c = (p['s5_c_re_' + sfx][0] + 1j * p['s5_c_im_' + sfx][0]).astype(jnp.complex64)
        cb = coef[:, :, None] * b_c
        kern = jnp.real(jnp.einsum('gop,jgp,gpi->jgoi', c_c, pw, cb))
        return lam_dt, pw, pw_next, c_c, cb, kern

    _, pw_f, pwn_f, c_f, cb_f, kern_f = direction('f')
    _, pw_b, pwn_b, c_b, cb_b, kern_b = direction('b')
    lam16_f = pwn_f[C - 1]
    lam16_b = pwn_b[C - 1]

    ii = jnp.arange(C)[None, :]
    jj = jnp.arange(C)[:, None]
    lag_f = jnp.clip(ii - jj, 0, C - 1)
    lag_b = jnp.clip(jj - ii, 0, C - 1)
    kf = jnp.where((ii >= jj)[:, :, None, None, None], kern_f[lag_f], 0.0)
    kb = jnp.where((jj >= ii)[:, :, None, None, None], kern_b[lag_b], 0.0)
    m_loc = jnp.transpose(kf + kb, (2, 0, 4, 1, 3)).reshape(S5_GROUPS, C * S5_GROUP, C * S5_GROUP)

    def z_map(pw_sel, cb):
        w = jnp.transpose(pw_sel[:, :, :, None] * cb[None], (1, 0, 3, 2)).reshape(S5_GROUPS, C * S5_GROUP, S5_STATE)
        re, im = jnp.real(w), jnp.imag(w)
        return jnp.concatenate([re, im, im, re], axis=-1)

    zf = z_map(pw_f[::-1], cb_f)
    zb = z_map(pw_b, cb_b)
    w1 = jnp.concatenate([m_loc, zf, zb], axis=-1).astype(BF16)

    def out_map(c_c, pw_sel):
        e = jnp.transpose(c_c[:, None, :, :] * pw_sel.transpose(1, 0, 2)[:, :, None, :], (0, 3, 1, 2))
        e = e.reshape(S5_GROUPS, S5_STATE, C * S5_GROUP)
        return jnp.concatenate([jnp.real(e), -jnp.imag(e)], axis=1)

    w2 = jnp.concatenate([out_map(c_f, pwn_f), out_map(c_b, pwn_b[::-1])], axis=1).astype(BF16)

    def lam_rows(l16):
        re, im = jnp.real(l16), jnp.imag(l16)
        return [jnp.concatenate([re, re], -1), jnp.concatenate([-im, im], -1), jnp.concatenate([im, -im], -1)]

    rows = lam_rows(lam16_f) + lam_rows(lam16_b)
    rows += [jnp.zeros_like(rows[0])] * 2
    lam16 = jnp.stack(rows, axis=1).astype(F32)
    return w1, w2, lam16


def _rope_tables(T):
    rows = T // GRID_W
    row_ids = jnp.repeat(jnp.arange(rows, dtype=F32), GRID_W)
    col_ids = jnp.tile(jnp.arange(GRID_W, dtype=F32), rows)
    pairs = HEAD // 4
    inv_freq = ROPE_THETA ** (-jnp.arange(pairs, dtype=F32) / pairs)
    ang = jnp.concatenate([row_ids[:, None] * inv_freq, col_ids[:, None] * inv_freq], axis=-1)
    cos = jnp.repeat(jnp.cos(ang), 2, axis=-1)
    sin = jnp.repeat(jnp.sin(ang), 2, axis=-1)
    sign = jnp.tile(jnp.array([-1.0, 1.0], F32), HEAD // 2)
    return jnp.tile(cos, (1, 2)), jnp.tile(sin * sign, (1, 2))


def _layer0_weights(p, T):
    w_in = p['hyb_w_in'][0]
    zc = jnp.zeros((D_MODEL, 64), F32)
    w_all = jnp.concatenate([w_in[:, 0:1728], zc, w_in[:, 1728:2624]], axis=1).astype(BF16)
    mu = p['hyb_shift_mu'][0]
    mu_all = jnp.concatenate([mu[0:1728], jnp.zeros((64,), F32), mu[1728:1856]]).reshape(1, RW_COLS)
    wup = jnp.zeros((384, 2048), F32)
    wup = wup.at[0:64, 0:512].set(p['rwkv_w_up_f'][0])
    wup = wup.at[64:128, 512:1024].set(p['rwkv_w_up_b'][0])
    wup = wup.at[128:192, 1024:1536].set(p['rwkv_a_up'][0])
    wup = wup.at[256:384, 1536:2048].set(p['rwkv_g_up'][0])
    wup = wup.astype(BF16)
    zr = jnp.zeros((512,), F32)
    par = jnp.stack([p['rwkv_w0_f'][0], p['rwkv_w0_b'][0], p['rwkv_a0'][0], p['rwkv_k_k'][0],
                     p['rwkv_k_a'][0], p['rwkv_r_k'][0].reshape(-1), zr, zr]).astype(F32)
    qg = jnp.tile(p['att_q_norm'][0], 8).reshape(1, 512).astype(F32)
    kg = jnp.tile(p['att_k_norm'][0], 2).reshape(1, 128).astype(F32)
    cos_t, sin_t = _rope_tables(T)
    seg = jnp.arange(512) // HEAD
    ones_bd = (seg[:, None] == seg[None, :]).astype(BF16)
    ln = jnp.stack([p['rwkv_lnx_g'][0], p['rwkv_lnx_b'][0]] + [zr] * 6).astype(F32)
    return dict(w_all=w_all, mu_all=mu_all, wup=wup, par=par, qg=qg, kg=kg, cos_t=cos_t, sin_t=sin_t,
                ones_bd=ones_bd, ln=ln, wo=p['hyb_w_out'][0].astype(BF16))


def _row(v):
    return v.reshape(1, -1).astype(F32)


def _mixer0(x2d, p, n_seq, T):
    w = _layer0_weights(p, T)
    r, k, v, a, b, lf, lb, g, bon, q, k_rep, v_rep = _pre0(
        x2d, T, _row(p['mix_norm'][0]), w['w_all'], w['mu_all'], w['wup'], w['par'], w['qg'], w['kg'],
        w['cos_t'], w['sin_t'], w['ones_bd'])
    yf, yb = _wkv(r, k, v, a, b, lf, lb, n_seq, T)
    att = _attention(q, k_rep, v_rep, n_seq, T)
    return _post0(x2d, yf, yb, g, bon, att, w['ln'], w['wo'], w['ones_bd'])


def _mixer1(x2d, p, n_seq, T):
    N = x2d.shape[0]
    w1, w2, lam16 = _s5_tables(p)
    nc = T // S5_CHUNK
    hn = _norm_bf16(x2d, _row(p['mix_norm'][1]))
    u4 = hn.reshape(n_seq, nc, S5_CHUNK, S5_GROUPS, S5_GROUP).transpose(0, 3, 1, 2, 4)
    u4 = u4.reshape(n_seq, S5_GROUPS, nc, S5_CHUNK * S5_GROUP)
    y4 = _s5_core(u4, w1, w2, lam16)
    ys = y4.reshape(n_seq, S5_GROUPS, nc, S5_CHUNK, S5_GROUP).transpose(0, 2, 3, 1, 4).reshape(N, D_MODEL)
    return _post1(x2d, ys, _row(p['mix_norm'][1]), _row(p['s5_d'][0]),
                  p['s5_glu_w'][0].astype(BF16), _row(p['s5_glu_b'][0]))


def _ffn(x2d, p, layer):
    return _mlp(x2d, _row(p['ffn_norm'][layer]), p['ffn_up'][layer].astype(BF16), p['ffn_down'][layer].astype(BF16))


def _trunk(x, p):
    n_seq, T, _ = x.shape
    x2d = x.reshape(n_seq * T, D_MODEL)
    x2d = _ffn(_mixer0(x2d, p, n_seq, T), p, 0)
    x2d = _ffn(_mixer1(x2d, p, n_seq, T), p, 1)
    return x2d.reshape(n_seq, T, D_MODEL)


def kernel(x_prompt, x_sample, mix_norm, ffn_norm, ffn_up, ffn_down, hyb_w_in, hyb_shift_mu, rwkv_w0_f, rwkv_w_up_f, rwkv_w0_b, rwkv_w_up_b, rwkv_a0, rwkv_a_up, rwkv_g_up, rwkv_k_k, rwkv_k_a, rwkv_r_k, rwkv_lnx_g, rwkv_lnx_b, att_q_norm, att_k_norm, hyb_w_out, s5_lam_re_f, s5_lam_im_f, s5_log_dt_f, s5_lam_re_b, s5_lam_im_b, s5_log_dt_b, s5_b_re, s5_b_im, s5_c_re_f, s5_c_im_f, s5_c_re_b, s5_c_im_b, s5_d, s5_glu_w, s5_glu_b):
    p = dict(mix_norm=mix_norm, ffn_norm=ffn_norm, ffn_up=ffn_up, ffn_down=ffn_down,
             hyb_w_in=hyb_w_in, hyb_shift_mu=hyb_shift_mu,
             rwkv_w0_f=rwkv_w0_f, rwkv_w_up_f=rwkv_w_up_f, rwkv_w0_b=rwkv_w0_b, rwkv_w_up_b=rwkv_w_up_b,
             rwkv_a0=rwkv_a0, rwkv_a_up=rwkv_a_up, rwkv_g_up=rwkv_g_up,
             rwkv_k_k=rwkv_k_k, rwkv_k_a=rwkv_k_a, rwkv_r_k=rwkv_r_k,
             rwkv_lnx_g=rwkv_lnx_g, rwkv_lnx_b=rwkv_lnx_b,
             att_q_norm=att_q_norm, att_k_norm=att_k_norm, hyb_w_out=hyb_w_out,
             s5_lam_re_f=s5_lam_re_f, s5_lam_im_f=s5_lam_im_f, s5_log_dt_f=s5_log_dt_f,
             s5_lam_re_b=s5_lam_re_b, s5_lam_im_b=s5_lam_im_b, s5_log_dt_b=s5_log_dt_b,
             s5_b_re=s5_b_re, s5_b_im=s5_b_im,
             s5_c_re_f=s5_c_re_f, s5_c_im_f=s5_c_im_f, s5_c_re_b=s5_c_re_b, s5_c_im_b=s5_c_im_b,
             s5_d=s5_d, s5_glu_w=s5_glu_w, s5_glu_b=s5_glu_b)
    nb = x_prompt.shape[0]
    y = _trunk(jnp.concatenate([x_prompt, x_sample], axis=0), p)
    return (y[:nb], y[nb:])
```

```python
import functools
import math

import jax
import jax.numpy as jnp
from jax import lax
from jax.experimental import pallas as pl
from jax.experimental.pallas import tpu as pltpu

F32 = jnp.float32
BF16 = jnp.bfloat16

D_MODEL = 1024
D_FF = 4 * D_MODEL
RMS_EPS = 1e-6
GRID_W = 64
RWKV_DIM = 512
HEAD = 64
GN_EPS = 64e-5
N_KV = 2
ROPE_THETA = 10000.0
S5_GROUP = 16
S5_GROUPS = D_MODEL // S5_GROUP
S5_STATE = 64
S5_SLAB = 128 // S5_GROUP

WKV_CHUNK = 64
S5_CHUNK = 16
RW_COLS = 1920
ALL_COLS = 2688
EXP_M05 = math.exp(-0.5)
VMEM_LIMIT = 56 * 1024 * 1024


def _dot(a, b):
    return jnp.dot(a, b, preferred_element_type=F32)


def _dot_nt(a, b):
    return lax.dot_general(a, b, (((1,), (1,)), ((), ())), preferred_element_type=F32)


def _split2(x):
    hi = x.astype(BF16)
    lo = (x - hi.astype(F32)).astype(BF16)
    return hi, lo


def _seg_sum(x, ones_bd):
    hi, lo = _split2(x)
    return _dot(hi, ones_bd) + _dot(lo, ones_bd)


def _rms(x, gain):
    return x * lax.rsqrt(jnp.mean(x * x, axis=-1, keepdims=True) + RMS_EPS) * gain


def _sigmoid(x):
    return 1.0 / (1.0 + jnp.exp(-x))


def _params(sem):
    return pltpu.CompilerParams(dimension_semantics=sem, vmem_limit_bytes=VMEM_LIMIT)


def _pre0_kernel(x_ref, xp_ref, xn_ref, gain_ref, w_ref, mu_ref, wup_ref, par_ref, qg_ref, kg_ref,
                 cos_ref, sin_ref, ones_ref,
                 r_o, k_o, v_o, a_o, b_o, lf_o, lb_o, g_o, bon_o, q_o, ka_o, va_o,
                 *, tiles_per_seq, tm):
    pos = pl.program_id(0) % tiles_per_seq
    gain = gain_ref[...]
    hn = _rms(x_ref[...], gain).astype(BF16)
    H = _dot(hn, w_ref[...])
    w_rw = w_ref[:, :RW_COLS]
    h_prev = _dot(_rms(xp_ref[...], gain).astype(BF16), w_rw)[7:8]
    h_next = _dot(_rms(xn_ref[...], gain).astype(BF16), w_rw)[0:1]
    h_prev = jnp.where(pos == 0, 0.0, h_prev)
    h_next = jnp.where(pos == tiles_per_seq - 1, 0.0, h_next)
    Hr = H[:, :RW_COLS]
    row = lax.broadcasted_iota(jnp.int32, Hr.shape, 0)
    prev = jnp.where(row == 0, h_prev, pltpu.roll(Hr, 1, 0))
    nxt = jnp.where(row == tm - 1, h_next, pltpu.roll(Hr, tm - 1, 0))
    Hs = Hr + mu_ref[...] * (0.5 * (prev + nxt) - Hr)

    ones = ones_ref[...]
    par = par_ref[...]
    r = Hs[:, 0:512]
    k = Hs[:, 512:1024]
    v = Hs[:, 1024:1536]
    act = jnp.concatenate(
        [jnp.tanh(Hs[:, 1536:1664]), Hs[:, 1664:1792], _sigmoid(Hs[:, 1792:1920])], axis=1).astype(BF16)
    up = _dot(act, wup_ref[...])
    lf_o[...] = -EXP_M05 * _sigmoid(par[0:1] + up[:, 0:512])
    lb_o[...] = -EXP_M05 * _sigmoid(par[1:2] + up[:, 512:1024])
    a_sig = _sigmoid(par[2:3] + up[:, 1024:1536])
    g_o[...] = up[:, 1536:2048]
    kk = k * par[3:4]
    kk = kk / jnp.maximum(jnp.sqrt(_seg_sum(kk * kk, ones)), 1e-12)
    k2 = k * (1.0 + (a_sig - 1.0) * par[4:5])
    r_o[...] = r
    k_o[...] = k2
    v_o[...] = v
    a_o[...] = -kk
    b_o[...] = kk * a_sig
    bon_o[...] = _seg_sum(r * k2 * par[5:6], ones) * v

    qa = H[:, RW_COLS:RW_COLS + 512]
    ka = H[:, RW_COLS + 512:RW_COLS + 640]
    va = H[:, RW_COLS + 640:RW_COLS + 768]
    cos = cos_ref[...]
    sin = sin_ref[...]

    def rope(x, c, s):
        n = x.shape[1]
        lane = lax.broadcasted_iota(jnp.int32, x.shape, 1)
        swapped = jnp.where(lane % 2 == 0, pltpu.roll(x, n - 1, 1), pltpu.roll(x, 1, 1))
        return x * c + swapped * s

    qn = qa * lax.rsqrt(_seg_sum(qa * qa, ones) * (1.0 / HEAD) + RMS_EPS) * qg_ref[...]
    qr = rope(qn, jnp.concatenate([cos] * 4, axis=1), jnp.concatenate([sin] * 4, axis=1))
    q_o[...] = (qr * (HEAD ** -0.5)).astype(BF16)
    kn = ka * lax.rsqrt(_seg_sum(ka * ka, ones[:128, :128]) * (1.0 / HEAD) + RMS_EPS) * kg_ref[...]
    kr = rope(kn, cos, sin)
    lt64 = lax.broadcasted_iota(jnp.int32, kr.shape, 1) < HEAD

    def rep(x):
        sw = pltpu.roll(x, HEAD, 1)
        return jnp.concatenate([jnp.where(lt64, x, sw), jnp.where(lt64, sw, x)], axis=1).astype(BF16)

    ka_o[...] = rep(kr)
    va_o[0] = va.T.astype(BF16)


def _pre0(x2d, T, gain, w_all, mu_all, wup, par, qg, kg, cos_t, sin_t, ones_bd):
    N = x2d.shape[0]
    tm = min(256, T)
    tps = T // tm
    nt = N // tm
    t8 = tm // 8
    nb8 = N // 8
    const = lambda i: (0, 0)
    tile = lambda i: (i, 0)
    f512 = jax.ShapeDtypeStruct((N, 512), F32)
    out_shape = [f512] * 9 + [jax.ShapeDtypeStruct((N, 512), BF16), jax.ShapeDtypeStruct((N, 256), BF16),
                              jax.ShapeDtypeStruct((N // T, 2 * HEAD, T), BF16)]
    out_specs = [pl.BlockSpec((tm, 512), tile)] * 10 + [
        pl.BlockSpec((tm, 256), tile), pl.BlockSpec((1, 2 * HEAD, tm), lambda i: (i // tps, 0, i % tps))]
    return pl.pallas_call(
        functools.partial(_pre0_kernel, tiles_per_seq=tps, tm=tm),
        grid=(nt,),
        in_specs=[
            pl.BlockSpec((tm, D_MODEL), tile),
            pl.BlockSpec((8, D_MODEL), lambda i: (jnp.maximum(i * t8 - 1, 0), 0)),
            pl.BlockSpec((8, D_MODEL), lambda i: (jnp.minimum((i + 1) * t8, nb8 - 1), 0)),
            pl.BlockSpec((1, D_MODEL), const),
            pl.BlockSpec((D_MODEL, ALL_COLS), const),
            pl.BlockSpec((1, RW_COLS), const),
            pl.BlockSpec((384, 2048), const),
            pl.BlockSpec((8, 512), const),
            pl.BlockSpec((1, 512), const),
            pl.BlockSpec((1, 128), const),
            pl.BlockSpec((tm, 128), lambda i: (i % tps, 0)),
            pl.BlockSpec((tm, 128), lambda i: (i % tps, 0)),
            pl.BlockSpec((512, 512), const),
        ],
        out_specs=out_specs,
        out_shape=out_shape,
        compiler_params=_params(("parallel",)),
        name="pre0",
    )(x2d, x2d, x2d, gain, w_all, mu_all, wup, par, qg, kg, cos_t, sin_t, ones_bd)


def _wkv_direction(r, k, v, a, b, L, h_ref, fwd):
    C = WKV_CHUNK
    Q = 4 * HEAD
    ti = lax.broadcasted_iota(jnp.int32, (C, C), 0)
    si = lax.broadcasted_iota(jnp.int32, (C, C), 1)
    tri = jnp.where(si <= ti, 1.0, 0.0).astype(BF16)
    l1 = L.astype(BF16)
    rem = L - l1.astype(F32)
    l2 = rem.astype(BF16)
    l3 = (rem - l2.astype(F32)).astype(BF16)
    cs = _dot(tri, l1) + _dot(tri, l2) + _dot(tri, l3)
    total = cs[C - 1:C, :]
    if fwd:
        cs_incl = cs
        cs_excl = cs - L
    else:
        cs_incl = total - (cs - L)
        cs_excl = total - cs
    e_incl = jnp.exp(cs_incl)
    e_inv = jnp.exp(-cs_incl)
    e_rem = jnp.exp(total - cs_incl)
    a_t = a * jnp.exp(cs_excl)
    r_t = r * e_incl
    b_t = b * e_inv
    k_t = k * e_inv
    b_h = b * e_rem
    k_h = k * e_rem
    gam = jnp.exp(total)

    lane_q = lax.broadcasted_iota(jnp.int32, (C, Q), 1) // HEAD
    ri = lax.broadcasted_iota(jnp.int32, (Q, Q), 0)
    ci = lax.broadcasted_iota(jnp.int32, (Q, Q), 1)
    same = (ri // C) == (ci // C)
    if fwd:
        m_strict = same & ((ci % C) < (ri % C))
        m_incl = same & ((ci % C) <= (ri % C))
    else:
        m_strict = same & ((ci % C) > (ri % C))
        m_incl = same & ((ci % C) >= (ri % C))
    eye = ri == ci

    def stack(xq):
        return jnp.concatenate([jnp.where(lane_q == h, xq, 0.0) for h in range(4)], axis=0)

    def unstack(xs):
        return xs[0:C] + xs[C:2 * C] + xs[2 * C:3 * C] + xs[3 * C:4 * C]

    ys = []
    for q in range(RWKV_DIM // Q):
        sl = slice(q * Q, (q + 1) * Q)
        As = stack(a_t[:, sl]).astype(BF16)
        Rs = stack(r_t[:, sl]).astype(BF16)
        Bs = stack(b_t[:, sl]).astype(BF16)
        Ks = stack(k_t[:, sl]).astype(BF16)
        Vs = stack(v[:, sl]).astype(BF16)
        BhT = stack(b_h[:, sl]).T.astype(BF16)
        KhT = stack(k_h[:, sl]).T.astype(BF16)
        P = _dot_nt(jnp.concatenate([As, Rs], axis=0), jnp.concatenate([Bs, Ks], axis=0))
        A_ab = jnp.where(m_strict, P[:Q, :Q], 0.0)
        A_ak = jnp.where(m_strict, P[:Q, Q:], 0.0).astype(BF16)
        A_rb = jnp.where(m_incl, P[Q:, :Q], 0.0).astype(BF16)
        A_rk = jnp.where(m_incl, P[Q:, Q:], 0.0).astype(BF16)
        T = jnp.where(eye, 1.0, A_ab)
        Ap = A_ab.astype(BF16)
        for _ in range(int(math.log2(C)) - 1):
            Ap = _dot(Ap, Ap).astype(BF16)
            T = T + _dot(T.astype(BF16), Ap)
        Z = _dot(A_ak, Vs)
        X = _dot(T.astype(BF16), jnp.concatenate([As, Z.astype(BF16)], axis=1)).astype(BF16)
        W1 = _dot(A_rb, X)
        r_p = r_t[:, sl] + unstack(W1[:, :Q])
        y_p = unstack(W1[:, Q:] + _dot(A_rk, Vs))
        MN = _dot(BhT, X)
        M = jnp.where(eye, gam[:, sl], 0.0) + MN[:, :Q]
        Nn = MN[:, Q:] + _dot(KhT, Vs)
        h0 = h_ref[q]
        h_hi, h_lo = _split2(h0)
        lhs_hi, lhs_lo = _split2(jnp.concatenate([r_p, M], axis=0))
        seq = _dot(lhs_hi, h_hi) + _dot(lhs_hi, h_lo) + _dot(lhs_lo, h_hi)
        ys.append(seq[:C] + y_p)
        h_ref[q] = seq[C:] + Nn
    return jnp.concatenate(ys, axis=1)


def _wkv_kernel(rf, kf, vf, af, bf, lf, rb, kb, vb, ab, bb, lb, yf_o, yb_o, hf_ref, hb_ref):
    @pl.when(pl.program_id(1) == 0)
    def _():
        hf_ref[...] = jnp.zeros_like(hf_ref)
        hb_ref[...] = jnp.zeros_like(hb_ref)

    yf_o[...] = _wkv_direction(rf[...], kf[...], vf[...], af[...], bf[...], lf[...], hf_ref, True)
    yb_o[...] = _wkv_direction(rb[...], kb[...], vb[...], ab[...], bb[...], lb[...], hb_ref, False)


def _wkv(r, k, v, a, b, lf, lb, n_seq, T):
    C = WKV_CHUNK
    nc = T // C
    fw = lambda s, i: (s * nc + i, 0)
    bw = lambda s, i: (s * nc + nc - 1 - i, 0)
    spec_f = pl.BlockSpec((C, 512), fw)
    spec_b = pl.BlockSpec((C, 512), bw)
    shp = jax.ShapeDtypeStruct(r.shape, F32)
    return pl.pallas_call(
        _wkv_kernel,
        grid=(n_seq, nc),
        in_specs=[spec_f] * 6 + [spec_b] * 6,
        out_specs=[spec_f, spec_b],
        out_shape=[shp, shp],
        scratch_shapes=[pltpu.VMEM((2, 256, 256), F32), pltpu.VMEM((2, 256, 256), F32)],
        compiler_params=_params(("parallel", "arbitrary")),
        name="wkv",
    )(r, k, v, a, b, lf, r, k, v, a, b, lb)


def _attn_kernel(q_ref, k_ref, vt_ref, o_ref, s_ref, p_ref, acc_ref, *, tq, tk, T):
    q = q_ref[...]
    lo = lax.broadcasted_iota(jnp.int32, (tq, 128), 1) < HEAD
    zero = jnp.zeros((), BF16)
    qs = jnp.concatenate([
        jnp.where(lo, q[:, 0:128], zero), jnp.where(lo, zero, q[:, 0:128]),
        jnp.where(lo, q[:, 128:256], zero), jnp.where(lo, zero, q[:, 128:256])], axis=0)
    R = 4 * tq
    n = T // tk

    def scores(j):
        return _dot_nt(k_ref[pl.ds(pl.multiple_of(j * tk, tk), tk), :], qs)

    def values(j, p):
        return _dot(vt_ref[0, :, pl.ds(pl.multiple_of(j * tk, tk), tk)], p)

    s_ref[0] = scores(0)
    p_ref[1] = jnp.zeros((tk, R), BF16)
    acc_ref[...] = jnp.zeros((HEAD, R), F32)

    def step(j, b, carry):
        m, l, alpha_prev = carry
        s_ref[1 - b] = scores(jnp.minimum(j + 1, n - 1))
        acc_ref[...] = alpha_prev * acc_ref[...] + values(jnp.maximum(j - 1, 0), p_ref[1 - b])
        s = s_ref[b]
        m_new = jnp.maximum(m, jnp.max(s, axis=0, keepdims=True))
        alpha = jnp.exp(m - m_new)
        p = jnp.exp(s - m_new)
        p_ref[b] = p.astype(BF16)
        return m_new, alpha * l + jnp.sum(p, axis=0, keepdims=True), alpha

    def body(jj, carry):
        return step(2 * jj + 1, 1, step(2 * jj, 0, carry))

    m0 = jnp.full((1, R), -1e30, F32)
    _, l, alpha = lax.fori_loop(0, n // 2, body, (m0, jnp.zeros((1, R), F32), jnp.ones((1, R), F32)))
    acc = alpha * acc_ref[...] + values(n - 1, p_ref[1])
    ot = acc / l
    o01 = jnp.concatenate([ot[:, 0:tq], ot[:, tq:2 * tq]], axis=0).T
    o23 = jnp.concatenate([ot[:, 2 * tq:3 * tq], ot[:, 3 * tq:4 * tq]], axis=0).T
    o_ref[...] = jnp.concatenate([o01, o23], axis=1)


def _attention(q, k_rep, v_t, n_seq, T):
    tq = min(128, T)
    tk = min(512, T // 2)
    nq = T // tq
    assert (T // tk) % 2 == 0
    return pl.pallas_call(
        functools.partial(_attn_kernel, tq=tq, tk=tk, T=T),
        grid=(n_seq, N_KV, nq),
        in_specs=[
            pl.BlockSpec((tq, 256), lambda s, h, i: (s * nq + i, h)),
            pl.BlockSpec((T, 128), lambda s, h, i: (s, h)),
            pl.BlockSpec((1, HEAD, T), lambda s, h, i: (s, h, 0)),
        ],
        out_specs=pl.BlockSpec((tq, 256), lambda s, h, i: (s * nq + i, h)),
        out_shape=jax.ShapeDtypeStruct(q.shape, F32),
        scratch_shapes=[pltpu.VMEM((2, tk, 4 * tq), F32), pltpu.VMEM((2, tk, 4 * tq), BF16),
                        pltpu.VMEM((HEAD, 4 * tq), F32)],
        compiler_params=_params(("parallel", "parallel", "arbitrary")),
        name="attention",
    )(q, k_rep, v_t)


def _post0_kernel(x_ref, yf_ref, yb_ref, g_ref, bon_ref, att_ref, ln_ref, wo_ref, ones_ref, o_ref):
    ones = ones_ref[...]
    y = yf_ref[...] + yb_ref[...]
    mean = _seg_sum(y, ones) * (1.0 / HEAD)
    d = y - mean
    var = _seg_sum(d * d, ones) * (1.0 / HEAD)
    yn = d * lax.rsqrt(var + GN_EPS) * ln_ref[0:1] + ln_ref[1:2]
    ya = ((yn + bon_ref[...]) * g_ref[...]).astype(BF16)
    mix = _dot(ya, wo_ref[0:512, :]) + _dot(att_ref[...].astype(BF16), wo_ref[512:1024, :])
    o_ref[...] = x_ref[...] + mix


def _post0(x2d, yf, yb, g, bon, att, ln, wo, ones_bd):
    N = x2d.shape[0]
    tm = min(512, N)
    tile = lambda i: (i, 0)
    const = lambda i: (0, 0)
    return pl.pallas_call(
        _post0_kernel,
        grid=(N // tm,),
        in_specs=[pl.BlockSpec((tm, D_MODEL), tile)] + [pl.BlockSpec((tm, 512), tile)] * 5 + [
            pl.BlockSpec((8, 512), const), pl.BlockSpec((D_MODEL, D_MODEL), const), pl.BlockSpec((512, 512), const)],
        out_specs=pl.BlockSpec((tm, D_MODEL), tile),
        out_shape=jax.ShapeDtypeStruct(x2d.shape, F32),
        compiler_params=_params(("parallel",)),
        name="post0",
    )(x2d, yf, yb, g, bon, att, ln, wo, ones_bd)


def _mlp_kernel(x_ref, gain_ref, wu_ref, wd_ref, o_ref, hn_ref):
    @pl.when(pl.program_id(1) == 0)
    def _():
        x = x_ref[...]
        hn_ref[...] = _rms(x, gain_ref[...]).astype(BF16)
        o_ref[...] = x

    u = _dot(hn_ref[...], wu_ref[...])
    u = jnp.maximum(u, 0.0)
    o_ref[...] += _dot((u * u).astype(BF16), wd_ref[...])


def _mlp(x2d, gain, w_up, w_down):
    N = x2d.shape[0]
    tm = min(512, N)
    fc = 1024
    return pl.pallas_call(
        _mlp_kernel,
        grid=(N // tm, D_FF // fc),
        in_specs=[
            pl.BlockSpec((tm, D_MODEL), lambda i, j: (i, 0)),
            pl.BlockSpec((1, D_MODEL), lambda i, j: (0, 0)),
            pl.BlockSpec((D_MODEL, fc), lambda i, j: (0, j)),
            pl.BlockSpec((fc, D_MODEL), lambda i, j: (j, 0)),
        ],
        out_specs=pl.BlockSpec((tm, D_MODEL), lambda i, j: (i, 0)),
        out_shape=jax.ShapeDtypeStruct(x2d.shape, F32),
        scratch_shapes=[pltpu.VMEM((tm, D_MODEL), BF16)],
        compiler_params=_params(("parallel", "arbitrary")),
        name="mlp",
    )(x2d, gain, w_up, w_down)


def _norm_kernel(x_ref, gain_ref, o_ref):
    o_ref[...] = _rms(x_ref[...], gain_ref[...]).astype(o_ref.dtype)


def _norm_f32(x2d, gain):
    N = x2d.shape[0]
    tm = min(1024, N)
    return pl.pallas_call(
        _norm_kernel,
        grid=(N // tm,),
        in_specs=[pl.BlockSpec((tm, D_MODEL), lambda i: (i, 0)), pl.BlockSpec((1, D_MODEL), lambda i: (0, 0))],
        out_specs=pl.BlockSpec((tm, D_MODEL), lambda i: (i, 0)),
        out_shape=jax.ShapeDtypeStruct(x2d.shape, F32),
        compiler_params=_params(("parallel",)),
        name="s5_norm",
    )(x2d, gain)


def _cmul_add(ar, ai, br, bi, cr, ci):
    return ar * br - ai * bi + cr, ar * bi + ai * br + ci


def _s5_scan(zr, zi, lam, pw_r, pw_i, fwd, nc, xs_ref, tot_ref, car_ref, cb_ref):
    W = zr.shape[1]
    nt = nc // 8
    row = lax.broadcasted_iota(jnp.int32, (nc, W), 0) % 8

    def shift(x, s):
        if fwd:
            return jnp.where(row >= s, pltpu.roll(x, s, 0), 0.0)
        return jnp.where(row < 8 - s, pltpu.roll(x, nc - s, 0), 0.0)

    xr, xi = zr, zi
    for lvl, s in enumerate((1, 2, 4)):
        xr, xi = _cmul_add(lam[2 * lvl:2 * lvl + 1], lam[2 * lvl + 1:2 * lvl + 2], shift(xr, s), shift(xi, s), xr, xi)
    last = 7 if fwd else 0
    nl = W // 128
    for c, x in enumerate((xr, xi)):
        for k in range(nl):
            xs_ref[c, k] = x[:, k * 128:(k + 1) * 128]
            tot_ref[c, :, k * 128:(k + 1) * 128] = xs_ref[c, k, pl.ds(last, nt, stride=8), :]
    l8r, l8i = lam[6:7], lam[7:8]
    nb = nt // 8

    def step(b, carry):
        cr, ci = carry
        blk = b if fwd else nb - 1 - b
        r0 = pl.multiple_of(blk * 8, 8)
        tr8 = tot_ref[0, pl.ds(r0, 8), :]
        ti8 = tot_ref[1, pl.ds(r0, 8), :]
        rows_r = [None] * 8
        rows_i = [None] * 8
        for jj in range(8):
            r = jj if fwd else 7 - jj
            rows_r[r] = cr
            rows_i[r] = ci
            cr, ci = _cmul_add(l8r, l8i, cr, ci, tr8[r:r + 1], ti8[r:r + 1])
        car_ref[0, pl.ds(r0, 8), :] = jnp.concatenate(rows_r, axis=0)
        car_ref[1, pl.ds(r0, 8), :] = jnp.concatenate(rows_i, axis=0)
        return cr, ci

    zero = jnp.zeros((1, W), F32)
    lax.fori_loop(0, nb, step, (zero, zero))
    for c in range(2):
        for k in range(nl):
            car = car_ref[c, :, k * 128:(k + 1) * 128]
            for r in range(8):
                cb_ref[c, k, pl.ds(r, nt, stride=8), :] = car
    cbr = jnp.concatenate([cb_ref[0, k] for k in range(nl)], axis=1)
    cbi = jnp.concatenate([cb_ref[1, k] for k in range(nl)], axis=1)
    pr = jnp.concatenate([pw_r] * nt, axis=0)
    pi = jnp.concatenate([pw_i] * nt, axis=0)
    return _cmul_add(pr, pi, cbr, cbi, shift(xr, 1), shift(xi, 1))


def _s5_kernel(x_ref, g2_ref, wz_ref, w2_ref, lam_ref, pw_ref, y_ref,
               lhs_ref, p_ref, xs_ref, tot_ref, car_ref, cb_ref, *, nc):
    C = S5_CHUNK
    W = S5_SLAB * S5_STATE
    for j in range(C):
        lhs_ref[:, j * 128:(j + 1) * 128] = x_ref[pl.ds(j, nc, stride=C), :].astype(BF16)
    lhs = lhs_ref[...]
    for d in range(2):
        z = _dot(lhs, wz_ref[0, :, d * 2 * W:(d + 1) * 2 * W])
        pr, pi = _s5_scan(z[:, :W], z[:, W:], lam_ref[0, d], pw_ref[0, d, 0], pw_ref[0, d, 1], d == 0, nc,
                          xs_ref, tot_ref, car_ref, cb_ref)
        p_ref[:, d * 2 * W:d * 2 * W + W] = pr.astype(BF16)
        p_ref[:, d * 2 * W + W:(d + 1) * 2 * W] = pi.astype(BF16)
    pv = p_ref[...]
    for i in range(0, C, 2):
        w_loc = g2_ref[0, (C - 1 - i) * 128:(2 * C - 1 - i) * 128, :]
        y2 = _dot(lhs, w_loc) + _dot(pv, w2_ref[0, :, i * 128:(i + 2) * 128])
        y_ref[pl.ds(i, nc, stride=C), :] = y2[:, :128]
        y_ref[pl.ds(i + 1, nc, stride=C), :] = y2[:, 128:]


def _s5_core(hn, g2, wz, w2, lam, pw, n_seq, T):
    nc = T // S5_CHUNK
    nt = nc // 8
    W = S5_SLAB * S5_STATE
    n_slab = D_MODEL // 128
    once = pl.Buffered(1)
    slab = lambda c, s: (c, 0, 0)
    return pl.pallas_call(
        functools.partial(_s5_kernel, nc=nc),
        grid=(n_slab, n_seq),
        in_specs=[
            pl.BlockSpec((T, 128), lambda c, s: (s, c)),
            pl.BlockSpec((1, 2 * S5_CHUNK * 128, 256), slab, pipeline_mode=once),
            pl.BlockSpec((1, S5_CHUNK * 128, 4 * W), slab, pipeline_mode=once),
            pl.BlockSpec((1, 4 * W, S5_CHUNK * 128), slab, pipeline_mode=once),
            pl.BlockSpec((1, 2, 8, W), lambda c, s: (c, 0, 0, 0)),
            pl.BlockSpec((1, 2, 2, 8, W), lambda c, s: (c, 0, 0, 0, 0)),
        ],
        out_specs=pl.BlockSpec((T, 128), lambda c, s: (s, c)),
        out_shape=jax.ShapeDtypeStruct(hn.shape, F32),
        scratch_shapes=[
            pltpu.VMEM((nc, S5_CHUNK * 128), BF16), pltpu.VMEM((nc, 4 * W), BF16),
            pltpu.VMEM((2, W // 128, nc, 128), F32), pltpu.VMEM((2, nt, W), F32), pltpu.VMEM((2, nt, W), F32),
            pltpu.VMEM((2, W // 128, nc, 128), F32)],
        compiler_params=_params(("parallel", "arbitrary")),
        name="s5_core",
    )(hn, g2, wz, w2, lam, pw)


def _post1_kernel(x_ref, ys_ref, gain_ref, d_ref, wg_ref, bg_ref, o_ref):
    x = x_ref[...]
    y = _rms(x, gain_ref[...]) * d_ref[...] + ys_ref[...]
    z = 0.5 * y * (1.0 + jnp.tanh(math.sqrt(2.0 / math.pi) * (y + 0.044715 * (y * y * y))))
    gate = _sigmoid(_dot(z.astype(BF16), wg_ref[...]) + bg_ref[...])
    o_ref[...] = x + z * gate


def _post1(x2d, ys, gain, d, wg, bg):
    N = x2d.shape[0]
    tm = min(512, N)
    tile = lambda i: (i, 0)
    const = lambda i: (0, 0)
    return pl.pallas_call(
        _post1_kernel,
        grid=(N // tm,),
        in_specs=[pl.BlockSpec((tm, D_MODEL), tile), pl.BlockSpec((tm, D_MODEL), tile),
                  pl.BlockSpec((1, D_MODEL), const), pl.BlockSpec((1, D_MODEL), const),
                  pl.BlockSpec((D_MODEL, D_MODEL), const), pl.BlockSpec((1, D_MODEL), const)],
        out_specs=pl.BlockSpec((tm, D_MODEL), tile),
        out_shape=jax.ShapeDtypeStruct(x2d.shape, F32),
        compiler_params=_params(("parallel",)),
        name="post1",
    )(x2d, ys, gain, d, wg, bg)


def _s5_tables(p):
    C, G, P, SL = S5_CHUNK, S5_GROUPS, S5_STATE, S5_SLAB
    ns = G // SL
    eye = jnp.eye(SL, dtype=F32)
    b_re = p['s5_b_re'][0].astype(F32)
    b_im = p['s5_b_im'][0].astype(F32)
    steps = jnp.arange(C, dtype=F32)

    def direction(sfx):
        lr = p['s5_lam_re_' + sfx][0].astype(F32)
        li = p['s5_lam_im_' + sfx][0].astype(F32)
        dt = jnp.exp(p['s5_log_dt_' + sfx][0].astype(F32))[:, None]

        def power(k):
            k = k[:, None, None]
            mag = jnp.exp(lr * dt * k)
            return mag * jnp.cos(li * dt * k), mag * jnp.sin(li * dt * k)

        l1r, l1i = power(jnp.ones((1,), F32))
        nr, ni = l1r[0] - 1.0, l1i[0]
        den = lr * lr + li * li
        cr = (nr * lr + ni * li) / den
        ci = (ni * lr - nr * li) / den
        cb_r = cr[:, :, None] * b_re - ci[:, :, None] * b_im
        cb_i = cr[:, :, None] * b_im + ci[:, :, None] * b_re
        c_r = p['s5_c_re_' + sfx][0].astype(F32)
        c_i = p['s5_c_im_' + sfx][0].astype(F32)
        return power, cb_r, cb_i, c_r, c_i

    def kernels(power, cb_r, cb_i, c_r, c_i):
        pr, pi = power(steps)
        d_r = pr[..., None] * cb_r - pi[..., None] * cb_i
        d_i = pr[..., None] * cb_i + pi[..., None] * cb_r
        k = jnp.einsum('gop,lgpi->lgoi', c_r, d_r) - jnp.einsum('gop,lgpi->lgoi', c_i, d_i)
        k = k.reshape(C, ns, SL, S5_GROUP, S5_GROUP)
        return jnp.einsum('lsgoi,gh->lsgiho', k, eye).reshape(C, ns, 128, 128)

    def state_in(power, cb_r, cb_i, ks):
        pr, pi = power(ks)
        w_r = (pr[..., None] * cb_r - pi[..., None] * cb_i).reshape(C, ns, SL, P, S5_GROUP)
        w_i = (pr[..., None] * cb_i + pi[..., None] * cb_r).reshape(C, ns, SL, P, S5_GROUP)
        blk = lambda w: jnp.einsum('jsgpi,gh->sjgihp', w, eye).reshape(ns, C * 128, SL * P)
        return blk(w_r), blk(w_i)

    def state_out(power, c_r, c_i, ks):
        pr, pi = power(ks)
        e_r = (c_r[None] * pr[:, :, None, :] - c_i[None] * pi[:, :, None, :]).reshape(C, ns, SL, S5_GROUP, P)
        e_i = (c_r[None] * pi[:, :, None, :] + c_i[None] * pr[:, :, None, :]).reshape(C, ns, SL, S5_GROUP, P)
        blk = lambda e: jnp.einsum('isgop,gh->sgpiho', e, eye).reshape(ns, SL * P, C * 128)
        return blk(e_r), blk(-e_i)

    def scan_tables(power, ks_rows):
        lr_, li_ = power(C * jnp.array([1.0, 2.0, 4.0, 8.0], F32))
        lam = jnp.stack([lr_, li_], axis=1).reshape(8, ns, SL * P).transpose(1, 0, 2)
        pr, pi = power(C * ks_rows)
        pw = jnp.stack([pr.reshape(8, ns, SL * P), pi.reshape(8, ns, SL * P)], axis=0).transpose(2, 0, 1, 3)
        return lam, pw

    pf = direction('f')
    pb = direction('b')
    kf = kernels(*pf)
    kb = kernels(*pb)
    zero = jnp.zeros((1, ns, 128, 128), F32)
    gen = jnp.concatenate([kf[:0:-1], (kf[0] + kb[0])[None], kb[1:], zero], axis=0)
    gen_prev = jnp.concatenate([zero, gen[:-1]], axis=0)
    g2 = jnp.concatenate([gen, gen_prev], axis=-1).transpose(1, 0, 2, 3).reshape(ns, 2 * C * 128, 256)

    zf_r, zf_i = state_in(pf[0], pf[1], pf[2], (C - 1) - steps)
    zb_r, zb_i = state_in(pb[0], pb[1], pb[2], steps)
    wz = jnp.concatenate([zf_r, zf_i, zb_r, zb_i], axis=-1)
    of_r, of_i = state_out(pf[0], pf[3], pf[4], steps + 1.0)
    ob_r, ob_i = state_out(pb[0], pb[3], pb[4], C - steps)
    w2 = jnp.concatenate([of_r, of_i, ob_r, ob_i], axis=1)
    rows = jnp.arange(8, dtype=F32)
    lam_f, pw_f = scan_tables(pf[0], rows)
    lam_b, pw_b = scan_tables(pb[0], 7.0 - rows)
    lam = jnp.stack([lam_f, lam_b], axis=1)
    pw = jnp.stack([pw_f, pw_b], axis=1)
    return g2.astype(BF16), wz.astype(BF16), w2.astype(BF16), lam, pw


def _rope_tables(T):
    rows = T // GRID_W
    row_ids = jnp.repeat(jnp.arange(rows, dtype=F32), GRID_W)
    col_ids = jnp.tile(jnp.arange(GRID_W, dtype=F32), rows)
    pairs = HEAD // 4
    inv_freq = ROPE_THETA ** (-jnp.arange(pairs, dtype=F32) / pairs)
    ang = jnp.concatenate([row_ids[:, None] * inv_freq, col_ids[:, None] * inv_freq], axis=-1)
    cos = jnp.repeat(jnp.cos(ang), 2, axis=-1)
    sin = jnp.repeat(jnp.sin(ang), 2, axis=-1)
    sign = jnp.tile(jnp.array([-1.0, 1.0], F32), HEAD // 2)
    return jnp.tile(cos, (1, 2)), jnp.tile(sin * sign, (1, 2))


def _layer0_weights(p, T):
    w_in = p['hyb_w_in'][0]
    zc = jnp.zeros((D_MODEL, 64), F32)
    w_all = jnp.concatenate([w_in[:, 0:1728], zc, w_in[:, 1728:2624]], axis=1).astype(BF16)
    mu = p['hyb_shift_mu'][0]
    mu_all = jnp.concatenate([mu[0:1728], jnp.zeros((64,), F32), mu[1728:1856]]).reshape(1, RW_COLS)
    wup = jnp.zeros((384, 2048), F32)
    wup = wup.at[0:64, 0:512].set(p['rwkv_w_up_f'][0])
    wup = wup.at[64:128, 512:1024].set(p['rwkv_w_up_b'][0])
    wup = wup.at[128:192, 1024:1536].set(p['rwkv_a_up'][0])
    wup = wup.at[256:384, 1536:2048].set(p['rwkv_g_up'][0])
    wup = wup.astype(BF16)
    zr = jnp.zeros((512,), F32)
    par = jnp.stack([p['rwkv_w0_f'][0], p['rwkv_w0_b'][0], p['rwkv_a0'][0], p['rwkv_k_k'][0],
                     p['rwkv_k_a'][0], p['rwkv_r_k'][0].reshape(-1), zr, zr]).astype(F32)
    qg = jnp.tile(p['att_q_norm'][0], 8).reshape(1, 512).astype(F32)
    kg = jnp.tile(p['att_k_norm'][0], 2).reshape(1, 128).astype(F32)
    cos_t, sin_t = _rope_tables(T)
    seg = jnp.arange(512) // HEAD
    ones_bd = (seg[:, None] == seg[None, :]).astype(BF16)
    ln = jnp.stack([p['rwkv_lnx_g'][0], p['rwkv_lnx_b'][0]] + [zr] * 6).astype(F32)
    return dict(w_all=w_all, mu_all=mu_all, wup=wup, par=par, qg=qg, kg=kg, cos_t=cos_t, sin_t=sin_t,
                ones_bd=ones_bd, ln=ln, wo=p['hyb_w_out'][0].astype(BF16))


def _row(v):
    return v.reshape(1, -1).astype(F32)


def _mixer0(x2d, p, n_seq, T):
    w = _layer0_weights(p, T)
    r, k, v, a, b, lf, lb, g, bon, q, k_rep, v_t = _pre0(
        x2d, T, _row(p['mix_norm'][0]), w['w_all'], w['mu_all'], w['wup'], w['par'], w['qg'], w['kg'],
        w['cos_t'], w['sin_t'], w['ones_bd'])
    yf, yb = _wkv(r, k, v, a, b, lf, lb, n_seq, T)
    att = _attention(q, k_rep, v_t, n_seq, T)
    return _post0(x2d, yf, yb, g, bon, att, w['ln'], w['wo'], w['ones_bd'])


def _mixer1(x2d, p, n_seq, T):
    g2, wz, w2, lam, pw = _s5_tables(p)
    hn = _norm_f32(x2d, _row(p['mix_norm'][1]))
    ys = _s5_core(hn, g2, wz, w2, lam, pw, n_seq, T)
    return _post1(x2d, ys, _row(p['mix_norm'][1]), _row(p['s5_d'][0]),
                  p['s5_glu_w'][0].astype(BF16), _row(p['s5_glu_b'][0]))


def _ffn(x2d, p, layer):
    return _mlp(x2d, _row(p['ffn_norm'][layer]), p['ffn_up'][layer].astype(BF16), p['ffn_down'][layer].astype(BF16))


def _trunk(x, p):
    n_seq, T, _ = x.shape
    x2d = x.reshape(n_seq * T, D_MODEL)
    x2d = _ffn(_mixer0(x2d, p, n_seq, T), p, 0)
    x2d = _ffn(_mixer1(x2d, p, n_seq, T), p, 1)
    return x2d.reshape(n_seq, T, D_MODEL)


def kernel(x_prompt, x_sample, mix_norm, ffn_norm, ffn_up, ffn_down, hyb_w_in, hyb_shift_mu, rwkv_w0_f, rwkv_w_up_f, rwkv_w0_b, rwkv_w_up_b, rwkv_a0, rwkv_a_up, rwkv_g_up, rwkv_k_k, rwkv_k_a, rwkv_r_k, rwkv_lnx_g, rwkv_lnx_b, att_q_norm, att_k_norm, hyb_w_out, s5_lam_re_f, s5_lam_im_f, s5_log_dt_f, s5_lam_re_b, s5_lam_im_b, s5_log_dt_b, s5_b_re, s5_b_im, s5_c_re_f, s5_c_im_f, s5_c_re_b, s5_c_im_b, s5_d, s5_glu_w, s5_glu_b):
    p = dict(mix_norm=mix_norm, ffn_norm=ffn_norm, ffn_up=ffn_up, ffn_down=ffn_down,
             hyb_w_in=hyb_w_in, hyb_shift_mu=hyb_shift_mu,
             rwkv_w0_f=rwkv_w0_f, rwkv_w_up_f=rwkv_w_up_f, rwkv_w0_b=rwkv_w0_b, rwkv_w_up_b=rwkv_w_up_b,
             rwkv_a0=rwkv_a0, rwkv_a_up=rwkv_a_up, rwkv_g_up=rwkv_g_up,
             rwkv_k_k=rwkv_k_k, rwkv_k_a=rwkv_k_a, rwkv_r_k=rwkv_r_k,
             rwkv_lnx_g=rwkv_lnx_g, rwkv_lnx_b=rwkv_lnx_b,
             att_q_norm=att_q_norm, att_k_norm=att_k_norm, hyb_w_out=hyb_w_out,
             s5_lam_re_f=s5_lam_re_f, s5_lam_im_f=s5_lam_im_f, s5_log_dt_f=s5_log_dt_f,
             s5_lam_re_b=s5_lam_re_b, s5_lam_im_b=s5_lam_im_b, s5_log_dt_b=s5_log_dt_b,
             s5_b_re=s5_b_re, s5_b_im=s5_b_im,
             s5_c_re_f=s5_c_re_f, s5_c_im_f=s5_c_im_f, s5_c_re_b=s5_c_re_b, s5_c_im_b=s5_c_im_b,
             s5_d=s5_d, s5_glu_w=s5_glu_w, s5_glu_b=s5_glu_b)
    nb = x_prompt.shape[0]
    y = _trunk(jnp.concatenate([x_prompt, x_sample], axis=0), p)
    return (y[:nb], y[nb:])
```

```python
import functools
import math

import jax
import jax.numpy as jnp
from jax import lax
from jax.experimental import pallas as pl
from jax.experimental.pallas import tpu as pltpu

F32 = jnp.float32
BF16 = jnp.bfloat16

D_MODEL = 1024
D_FF = 4 * D_MODEL
RMS_EPS = 1e-6
GRID_W = 64
RWKV_DIM = 512
HEAD = 64
GN_EPS = 64e-5
N_KV = 2
ROPE_THETA = 10000.0
S5_GROUP = 16
S5_GROUPS = D_MODEL // S5_GROUP
S5_STATE = 64
S5_SLAB = 128 // S5_GROUP

WKV_CHUNK = 64
S5_CHUNK = 16
RW_COLS = 1920
ALL_COLS = 2688
EXP_M05 = math.exp(-0.5)
LOG2_E = math.log2(math.e)
V_ROWS = HEAD + 16
VMEM_LIMIT = 56 * 1024 * 1024


def _dot(a, b):
    return jnp.dot(a, b, preferred_element_type=F32)


def _dot_nt(a, b):
    return lax.dot_general(a, b, (((1,), (1,)), ((), ())), preferred_element_type=F32)


def _split2(x):
    hi = x.astype(BF16)
    lo = (x - hi.astype(F32)).astype(BF16)
    return hi, lo


def _seg_sum(x, ones_bd):
    hi, lo = _split2(x)
    return _dot(hi, ones_bd) + _dot(lo, ones_bd)


def _rms(x, gain):
    return x * lax.rsqrt(jnp.mean(x * x, axis=-1, keepdims=True) + RMS_EPS) * gain


def _sigmoid(x):
    return 1.0 / (1.0 + jnp.exp(-x))


def _params(sem):
    return pltpu.CompilerParams(dimension_semantics=sem, vmem_limit_bytes=VMEM_LIMIT)


def _pre0_kernel(x_ref, xp_ref, xn_ref, gain_ref, w_ref, mu_ref, wup_ref, par_ref, qg_ref, kg_ref,
                 cos_ref, sin_ref, ones_ref,
                 r_o, k_o, v_o, a_o, b_o, lf_o, lb_o, g_o, bon_o, q_o, ka_o, va_o,
                 *, tiles_per_seq, tm):
    pos = pl.program_id(0) % tiles_per_seq
    gain = gain_ref[...]
    hn = _rms(x_ref[...], gain).astype(BF16)
    H = _dot(hn, w_ref[...])
    w_rw = w_ref[:, :RW_COLS]
    h_prev = _dot(_rms(xp_ref[...], gain).astype(BF16), w_rw)[7:8]
    h_next = _dot(_rms(xn_ref[...], gain).astype(BF16), w_rw)[0:1]
    h_prev = jnp.where(pos == 0, 0.0, h_prev)
    h_next = jnp.where(pos == tiles_per_seq - 1, 0.0, h_next)
    Hr = H[:, :RW_COLS]
    row = lax.broadcasted_iota(jnp.int32, Hr.shape, 0)
    prev = jnp.where(row == 0, h_prev, pltpu.roll(Hr, 1, 0))
    nxt = jnp.where(row == tm - 1, h_next, pltpu.roll(Hr, tm - 1, 0))
    Hs = Hr + mu_ref[...] * (0.5 * (prev + nxt) - Hr)

    ones = ones_ref[...]
    par = par_ref[...]
    r = Hs[:, 0:512]
    k = Hs[:, 512:1024]
    v = Hs[:, 1024:1536]
    act = jnp.concatenate(
        [jnp.tanh(Hs[:, 1536:1664]), Hs[:, 1664:1792], _sigmoid(Hs[:, 1792:1920])], axis=1).astype(BF16)
    up = _dot(act, wup_ref[...])
    lf_o[...] = -EXP_M05 * _sigmoid(par[0:1] + up[:, 0:512])
    lb_o[...] = -EXP_M05 * _sigmoid(par[1:2] + up[:, 512:1024])
    a_sig = _sigmoid(par[2:3] + up[:, 1024:1536])
    g_o[...] = up[:, 1536:2048]
    kk = k * par[3:4]
    kk = kk / jnp.maximum(jnp.sqrt(_seg_sum(kk * kk, ones)), 1e-12)
    k2 = k * (1.0 + (a_sig - 1.0) * par[4:5])
    r_o[...] = r
    k_o[...] = k2
    v_o[...] = v
    a_o[...] = -kk
    b_o[...] = kk * a_sig
    bon_o[...] = _seg_sum(r * k2 * par[5:6], ones) * v

    qa = H[:, RW_COLS:RW_COLS + 512]
    ka = H[:, RW_COLS + 512:RW_COLS + 640]
    va = H[:, RW_COLS + 640:RW_COLS + 768]
    cos = cos_ref[...]
    sin = sin_ref[...]

    def rope(x, c, s):
        n = x.shape[1]
        lane = lax.broadcasted_iota(jnp.int32, x.shape, 1)
        swapped = jnp.where(lane % 2 == 0, pltpu.roll(x, n - 1, 1), pltpu.roll(x, 1, 1))
        return x * c + swapped * s

    qn = qa * lax.rsqrt(_seg_sum(qa * qa, ones) * (1.0 / HEAD) + RMS_EPS) * qg_ref[...]
    qr = rope(qn, jnp.concatenate([cos] * 4, axis=1), jnp.concatenate([sin] * 4, axis=1))
    q_o[...] = (qr * (HEAD ** -0.5 * LOG2_E)).astype(BF16)
    kn = ka * lax.rsqrt(_seg_sum(ka * ka, ones[:128, :128]) * (1.0 / HEAD) + RMS_EPS) * kg_ref[...]
    kr = rope(kn, cos, sin)
    lt64 = lax.broadcasted_iota(jnp.int32, kr.shape, 1) < HEAD

    def rep(x):
        sw = pltpu.roll(x, HEAD, 1)
        return jnp.concatenate([jnp.where(lt64, x, sw), jnp.where(lt64, sw, x)], axis=1).astype(BF16)

    ka_o[...] = rep(kr)
    vt = va.T.astype(BF16)
    one = jnp.ones((V_ROWS - HEAD, tm), BF16)
    va_o[0] = jnp.concatenate([vt[0:HEAD], one, vt[HEAD:2 * HEAD], one], axis=0)


def _pre0(x2d, T, gain, w_all, mu_all, wup, par, qg, kg, cos_t, sin_t, ones_bd):
    N = x2d.shape[0]
    tm = min(256, T)
    tps = T // tm
    nt = N // tm
    t8 = tm // 8
    nb8 = N // 8
    const = lambda i: (0, 0)
    tile = lambda i: (i, 0)
    f512 = jax.ShapeDtypeStruct((N, 512), F32)
    out_shape = [f512] * 9 + [jax.ShapeDtypeStruct((N, 512), BF16), jax.ShapeDtypeStruct((N, 256), BF16),
                              jax.ShapeDtypeStruct((N // T, 2 * V_ROWS, T), BF16)]
    out_specs = [pl.BlockSpec((tm, 512), tile)] * 10 + [
        pl.BlockSpec((tm, 256), tile), pl.BlockSpec((1, 2 * V_ROWS, tm), lambda i: (i // tps, 0, i % tps))]
    return pl.pallas_call(
        functools.partial(_pre0_kernel, tiles_per_seq=tps, tm=tm),
        grid=(nt,),
        in_specs=[
            pl.BlockSpec((tm, D_MODEL), tile),
            pl.BlockSpec((8, D_MODEL), lambda i: (jnp.maximum(i * t8 - 1, 0), 0)),
            pl.BlockSpec((8, D_MODEL), lambda i: (jnp.minimum((i + 1) * t8, nb8 - 1), 0)),
            pl.BlockSpec((1, D_MODEL), const),
            pl.BlockSpec((D_MODEL, ALL_COLS), const),
            pl.BlockSpec((1, RW_COLS), const),
            pl.BlockSpec((384, 2048), const),
            pl.BlockSpec((8, 512), const),
            pl.BlockSpec((1, 512), const),
            pl.BlockSpec((1, 128), const),
            pl.BlockSpec((tm, 128), lambda i: (i % tps, 0)),
            pl.BlockSpec((tm, 128), lambda i: (i % tps, 0)),
            pl.BlockSpec((512, 512), const),
        ],
        out_specs=out_specs,
        out_shape=out_shape,
        compiler_params=_params(("parallel",)),
        name="pre0",
    )(x2d, x2d, x2d, gain, w_all, mu_all, wup, par, qg, kg, cos_t, sin_t, ones_bd)


def _wkv_direction(r, k, v, a, b, L, h_ref, fwd):
    C = WKV_CHUNK
    Q = 4 * HEAD
    ti = lax.broadcasted_iota(jnp.int32, (C, C), 0)
    si = lax.broadcasted_iota(jnp.int32, (C, C), 1)
    tri = jnp.where(si <= ti, 1.0, 0.0).astype(BF16)
    l1 = L.astype(BF16)
    rem = L - l1.astype(F32)
    l2 = rem.astype(BF16)
    l3 = (rem - l2.astype(F32)).astype(BF16)
    cs = _dot(tri, l1) + _dot(tri, l2) + _dot(tri, l3)
    total = cs[C - 1:C, :]
    if fwd:
        cs_incl = cs
        cs_excl = cs - L
    else:
        cs_incl = total - (cs - L)
        cs_excl = total - cs
    e_incl = jnp.exp(cs_incl)
    e_inv = jnp.exp(-cs_incl)
    e_rem = jnp.exp(total - cs_incl)
    a_t = a * jnp.exp(cs_excl)
    r_t = r * e_incl
    b_t = b * e_inv
    k_t = k * e_inv
    b_h = b * e_rem
    k_h = k * e_rem
    gam = jnp.exp(total)

    lane_q = lax.broadcasted_iota(jnp.int32, (C, Q), 1) // HEAD
    ri = lax.broadcasted_iota(jnp.int32, (Q, Q), 0)
    ci = lax.broadcasted_iota(jnp.int32, (Q, Q), 1)
    same = (ri // C) == (ci // C)
    if fwd:
        m_strict = same & ((ci % C) < (ri % C))
        m_incl = same & ((ci % C) <= (ri % C))
    else:
        m_strict = same & ((ci % C) > (ri % C))
        m_incl = same & ((ci % C) >= (ri % C))
    eye = ri == ci

    def stack(xq):
        return jnp.concatenate([jnp.where(lane_q == h, xq, 0.0) for h in range(4)], axis=0)

    streams = []
    for q in range(RWKV_DIM // Q):
        sl = slice(q * Q, (q + 1) * Q)
        streams.append(dict(
            q=q, h_ref=h_ref, m_strict=m_strict, m_incl=m_incl, eye=eye, r_t=r_t[:, sl],
            gam_col=jnp.sum(jnp.where(eye, gam[:, sl], 0.0), axis=1, keepdims=True),
            As=stack(a_t[:, sl]).astype(BF16), Rs=stack(r_t[:, sl]).astype(BF16),
            Bs=stack(b_t[:, sl]).astype(BF16), Ks=stack(k_t[:, sl]).astype(BF16),
            Vs=stack(v[:, sl]).astype(BF16),
            BhT=stack(b_h[:, sl]).T.astype(BF16), KhT=stack(k_h[:, sl]).T.astype(BF16)))
    return streams


def _wkv_solve(streams):
    C = WKV_CHUNK
    Q = 4 * HEAD

    def unstack(xs):
        return xs[0:C] + xs[C:2 * C] + xs[2 * C:3 * C] + xs[3 * C:4 * C]

    for s in streams:
        P = _dot_nt(jnp.concatenate([s['As'], s['Rs']], axis=0), jnp.concatenate([s['Bs'], s['Ks']], axis=0))
        A_ab = jnp.where(s['m_strict'], P[:Q, :Q], 0.0)
        s['A_ak'] = jnp.where(s['m_strict'], P[:Q, Q:], 0.0).astype(BF16)
        s['A_rb'] = jnp.where(s['m_incl'], P[Q:, :Q], 0.0).astype(BF16)
        s['A_rk'] = jnp.where(s['m_incl'], P[Q:, Q:], 0.0).astype(BF16)
        s['T'] = jnp.where(s['eye'], 1.0, A_ab)
        s['Ap'] = A_ab.astype(BF16)
    for _ in range(int(math.log2(C)) - 1):
        for s in streams:
            s['Ap'] = _dot(s['Ap'], s['Ap']).astype(BF16)
        for s in streams:
            s['T'] = s['T'] + _dot(s['T'].astype(BF16), s['Ap'])
    for s in streams:
        s['Z'] = _dot(s['A_ak'], s['Vs']).astype(BF16)
        s['y_k'] = _dot(s['A_rk'], s['Vs'])
        s['n_k'] = _dot(s['KhT'], s['Vs'])
    for s in streams:
        s['X'] = _dot(s['T'].astype(BF16), jnp.concatenate([s['As'], s['Z']], axis=1)).astype(BF16)
    for s in streams:
        W1 = _dot(s['A_rb'], s['X'])
        MN = _dot(s['BhT'], s['X'])
        s['r_p'] = s['r_t'] + unstack(W1[:, :Q])
        s['y_p'] = unstack(W1[:, Q:] + s['y_k'])
        s['M'] = MN[:, :Q]
        s['N'] = MN[:, Q:] + s['n_k']
    ys = []
    for s in streams:
        h0 = s['h_ref'][s['q']]
        seq = _dot(jnp.concatenate([s['r_p'], s['M']], axis=0).astype(BF16), h0.astype(BF16))
        ys.append(seq[:C] + s['y_p'])
        s['h_ref'][s['q']] = s['gam_col'] * h0 + seq[C:] + s['N']
    return ys


def _wkv_kernel(rf, kf, vf, af, bf, lf, rb, kb, vb, ab, bb, lb, yf_o, yb_o, hf_ref, hb_ref):
    @pl.when(pl.program_id(1) == 0)
    def _():
        hf_ref[...] = jnp.zeros_like(hf_ref)
        hb_ref[...] = jnp.zeros_like(hb_ref)

    streams = (_wkv_direction(rf[...], kf[...], vf[...], af[...], bf[...], lf[...], hf_ref, True)
               + _wkv_direction(rb[...], kb[...], vb[...], ab[...], bb[...], lb[...], hb_ref, False))
    ys = _wkv_solve(streams)
    yf_o[...] = jnp.concatenate(ys[0:2], axis=1)
    yb_o[...] = jnp.concatenate(ys[2:4], axis=1)


def _wkv(r, k, v, a, b, lf, lb, n_seq, T):
    C = WKV_CHUNK
    nc = T // C
    fw = lambda s, i: (s * nc + i, 0)
    bw = lambda s, i: (s * nc + nc - 1 - i, 0)
    spec_f = pl.BlockSpec((C, 512), fw)
    spec_b = pl.BlockSpec((C, 512), bw)
    shp = jax.ShapeDtypeStruct(r.shape, F32)
    return pl.pallas_call(
        _wkv_kernel,
        grid=(n_seq, nc),
        in_specs=[spec_f] * 6 + [spec_b] * 6,
        out_specs=[spec_f, spec_b],
        out_shape=[shp, shp],
        scratch_shapes=[pltpu.VMEM((2, 256, 256), F32), pltpu.VMEM((2, 256, 256), F32)],
        compiler_params=_params(("parallel", "arbitrary")),
        name="wkv",
    )(r, k, v, a, b, lf, r, k, v, a, b, lb)


def _attn_kernel(q_ref, k_ref, vt_ref, o_ref, s0_ref, s1_ref, p0_ref, p1_ref, acc_ref, *, tq, tk, T):
    q = q_ref[...]
    lo = lax.broadcasted_iota(jnp.int32, (tq, 128), 1) < HEAD
    zero = jnp.zeros((), BF16)
    qs = jnp.concatenate([
        jnp.where(lo, q[:, 0:128], zero), jnp.where(lo, zero, q[:, 0:128]),
        jnp.where(lo, q[:, 128:256], zero), jnp.where(lo, zero, q[:, 128:256])], axis=0)
    R = 4 * tq
    n = T // tk

    def chunk(j):
        return pl.ds(j * tk if isinstance(j, int) else pl.multiple_of(j * tk, tk), tk)

    def scores(j):
        return _dot_nt(k_ref[chunk(j), :], qs)

    def values(j, p):
        return _dot(vt_ref[0, :, chunk(j)], p)

    s_buf = (s0_ref, s1_ref)
    p_buf = (p0_ref, p1_ref)
    s_buf[0][...] = scores(0)
    p_buf[1][...] = jnp.zeros((tk, R), BF16)
    acc_ref[...] = jnp.zeros((V_ROWS, R), F32)

    def step(j, b, carry, ahead=True):
        m, alpha_prev = carry
        if ahead:
            s_buf[1 - b][...] = scores(j + 1)
        prev = max(j - 1, 0) if isinstance(j, int) else jnp.maximum(j - 1, 0)
        acc_ref[...] = alpha_prev * acc_ref[...] + values(prev, p_buf[1 - b][...])
        s = s_buf[b][...]
        m_new = jnp.maximum(m, jnp.max(s, axis=0, keepdims=True))
        p_buf[b][...] = jnp.exp2(s - m_new).astype(BF16)
        return m_new, jnp.exp2(m - m_new)

    def body(i, carry):
        return step(2 * i + 1, 1, step(2 * i, 0, carry))

    carry = lax.fori_loop(0, n // 2 - 1, body, (jnp.full((1, R), -1e30, F32), jnp.ones((1, R), F32)))
    carry = step(n - 2, 0, carry)
    _, alpha = step(n - 1, 1, carry, ahead=False)
    acc = alpha * acc_ref[...] + values(n - 1, p_buf[1][...])
    ot = acc[0:HEAD] / acc[HEAD:HEAD + 1]
    o01 = jnp.concatenate([ot[:, 0:tq], ot[:, tq:2 * tq]], axis=0).T
    o23 = jnp.concatenate([ot[:, 2 * tq:3 * tq], ot[:, 3 * tq:4 * tq]], axis=0).T
    o_ref[...] = jnp.concatenate([o01, o23], axis=1)


def _attention(q, k_rep, v_t, n_seq, T):
    tq = min(256, T)
    tk = min(512, T // 2)
    nq = T // tq
    assert (T // tk) % 2 == 0
    return pl.pallas_call(
        functools.partial(_attn_kernel, tq=tq, tk=tk, T=T),
        grid=(n_seq, N_KV, nq),
        in_specs=[
            pl.BlockSpec((tq, 256), lambda s, h, i: (s * nq + i, h)),
            pl.BlockSpec((T, 128), lambda s, h, i: (s, h)),
            pl.BlockSpec((1, V_ROWS, T), lambda s, h, i: (s, h, 0)),
        ],
        out_specs=pl.BlockSpec((tq, 256), lambda s, h, i: (s * nq + i, h)),
        out_shape=jax.ShapeDtypeStruct(q.shape, F32),
        scratch_shapes=[pltpu.VMEM((tk, 4 * tq), F32)] * 2 + [pltpu.VMEM((tk, 4 * tq), BF16)] * 2 + [
            pltpu.VMEM((V_ROWS, 4 * tq), F32)],
        compiler_params=_params(("parallel", "parallel", "arbitrary")),
        name="attention",
    )(q, k_rep, v_t)


def _post0_kernel(x_ref, yf_ref, yb_ref, g_ref, bon_ref, att_ref, ln_ref, wo_ref, ones_ref, o_ref):
    ones = ones_ref[...]
    y = yf_ref[...] + yb_ref[...]
    mean = _seg_sum(y, ones) * (1.0 / HEAD)
    d = y - mean
    var = _seg_sum(d * d, ones) * (1.0 / HEAD)
    yn = d * lax.rsqrt(var + GN_EPS) * ln_ref[0:1] + ln_ref[1:2]
    ya = ((yn + bon_ref[...]) * g_ref[...]).astype(BF16)
    mix = _dot(ya, wo_ref[0:512, :]) + _dot(att_ref[...].astype(BF16), wo_ref[512:1024, :])
    o_ref[...] = x_ref[...] + mix


def _post0(x2d, yf, yb, g, bon, att, ln, wo, ones_bd):
    N = x2d.shape[0]
    tm = min(512, N)
    tile = lambda i: (i, 0)
    const = lambda i: (0, 0)
    return pl.pallas_call(
        _post0_kernel,
        grid=(N // tm,),
        in_specs=[pl.BlockSpec((tm, D_MODEL), tile)] + [pl.BlockSpec((tm, 512), tile)] * 5 + [
            pl.BlockSpec((8, 512), const), pl.BlockSpec((D_MODEL, D_MODEL), const), pl.BlockSpec((512, 512), const)],
        out_specs=pl.BlockSpec((tm, D_MODEL), tile),
        out_shape=jax.ShapeDtypeStruct(x2d.shape, F32),
        compiler_params=_params(("parallel",)),
        name="post0",
    )(x2d, yf, yb, g, bon, att, ln, wo, ones_bd)


def _mlp_kernel(x_ref, gain_ref, wu_ref, wd_ref, o_ref, hn_ref):
    @pl.when(pl.program_id(1) == 0)
    def _():
        x = x_ref[...]
        hn_ref[...] = _rms(x, gain_ref[...]).astype(BF16)
        o_ref[...] = x

    u = _dot(hn_ref[...], wu_ref[...])
    u = jnp.maximum(u, 0.0)
    o_ref[...] += _dot((u * u).astype(BF16), wd_ref[...])


def _mlp(x2d, gain, w_up, w_down):
    N = x2d.shape[0]
    tm = min(512, N)
    fc = 1024
    return pl.pallas_call(
        _mlp_kernel,
        grid=(N // tm, D_FF // fc),
        in_specs=[
            pl.BlockSpec((tm, D_MODEL), lambda i, j: (i, 0)),
            pl.BlockSpec((1, D_MODEL), lambda i, j: (0, 0)),
            pl.BlockSpec((D_MODEL, fc), lambda i, j: (0, j)),
            pl.BlockSpec((fc, D_MODEL), lambda i, j: (j, 0)),
        ],
        out_specs=pl.BlockSpec((tm, D_MODEL), lambda i, j: (i, 0)),
        out_shape=jax.ShapeDtypeStruct(x2d.shape, F32),
        scratch_shapes=[pltpu.VMEM((tm, D_MODEL), BF16)],
        compiler_params=_params(("parallel", "arbitrary")),
        name="mlp",
    )(x2d, gain, w_up, w_down)


def _norm_kernel(x_ref, gain_ref, o_ref):
    o_ref[...] = _rms(x_ref[...], gain_ref[...]).astype(o_ref.dtype)


def _norm_f32(x2d, gain):
    N = x2d.shape[0]
    tm = min(1024, N)
    return pl.pallas_call(
        _norm_kernel,
        grid=(N // tm,),
        in_specs=[pl.BlockSpec((tm, D_MODEL), lambda i: (i, 0)), pl.BlockSpec((1, D_MODEL), lambda i: (0, 0))],
        out_specs=pl.BlockSpec((tm, D_MODEL), lambda i: (i, 0)),
        out_shape=jax.ShapeDtypeStruct(x2d.shape, F32),
        compiler_params=_params(("parallel",)),
        name="s5_norm",
    )(x2d, gain)


def _cmul_add(ar, ai, br, bi, cr, ci):
    return ar * br - ai * bi + cr, ar * bi + ai * br + ci


def _s5_scan(zr, zi, lam, pw_r, pw_i, fwd, nc, xs_ref, tot_ref, car_ref, cb_ref):
    W = zr.shape[1]
    nt = nc // 8
    row = lax.broadcasted_iota(jnp.int32, (nc, W), 0) % 8

    def shift(x, s):
        if fwd:
            return jnp.where(row >= s, pltpu.roll(x, s, 0), 0.0)
        return jnp.where(row < 8 - s, pltpu.roll(x, nc - s, 0), 0.0)

    xr, xi = zr, zi
    for lvl, s in enumerate((1, 2, 4)):
        xr, xi = _cmul_add(lam[2 * lvl:2 * lvl + 1], lam[2 * lvl + 1:2 * lvl + 2], shift(xr, s), shift(xi, s), xr, xi)
    last = 7 if fwd else 0
    nl = W // 128
    for c, x in enumerate((xr, xi)):
        for k in range(nl):
            xs_ref[c, k] = x[:, k * 128:(k + 1) * 128]
            tot_ref[c, :, k * 128:(k + 1) * 128] = xs_ref[c, k, pl.ds(last, nt, stride=8), :]
    l8r, l8i = lam[6:7], lam[7:8]
    nb = nt // 8

    def step(b, carry):
        cr, ci = carry
        blk = b if fwd else nb - 1 - b
        r0 = pl.multiple_of(blk * 8, 8)
        tr8 = tot_ref[0, pl.ds(r0, 8), :]
        ti8 = tot_ref[1, pl.ds(r0, 8), :]
        rows_r = [None] * 8
        rows_i = [None] * 8
        for jj in range(8):
            r = jj if fwd else 7 - jj
            rows_r[r] = cr
            rows_i[r] = ci
            cr, ci = _cmul_add(l8r, l8i, cr, ci, tr8[r:r + 1], ti8[r:r + 1])
        car_ref[0, pl.ds(r0, 8), :] = jnp.concatenate(rows_r, axis=0)
        car_ref[1, pl.ds(r0, 8), :] = jnp.concatenate(rows_i, axis=0)
        return cr, ci

    zero = jnp.zeros((1, W), F32)
    lax.fori_loop(0, nb, step, (zero, zero))
    for c in range(2):
        for k in range(nl):
            car = car_ref[c, :, k * 128:(k + 1) * 128]
            for r in range(8):
                cb_ref[c, k, pl.ds(r, nt, stride=8), :] = car
    cbr = jnp.concatenate([cb_ref[0, k] for k in range(nl)], axis=1)
    cbi = jnp.concatenate([cb_ref[1, k] for k in range(nl)], axis=1)
    pr = jnp.concatenate([pw_r] * nt, axis=0)
    pi = jnp.concatenate([pw_i] * nt, axis=0)
    return _cmul_add(pr, pi, cbr, cbi, shift(xr, 1), shift(xi, 1))


def _s5_kernel(x_ref, g2_ref, wz_ref, w2_ref, lam_ref, pw_ref, y_ref,
               lhs_ref, p_ref, xs_ref, tot_ref, car_ref, cb_ref, *, nc):
    C = S5_CHUNK
    W = S5_SLAB * S5_STATE
    for j in range(C):
        lhs_ref[:, j * 128:(j + 1) * 128] = x_ref[pl.ds(j, nc, stride=C), :].astype(BF16)
    lhs = lhs_ref[...]
    for d in range(2):
        z = _dot(lhs, wz_ref[0, :, d * 2 * W:(d + 1) * 2 * W])
        pr, pi = _s5_scan(z[:, :W], z[:, W:], lam_ref[0, d], pw_ref[0, d, 0], pw_ref[0, d, 1], d == 0, nc,
                          xs_ref, tot_ref, car_ref, cb_ref)
        p_ref[:, d * 2 * W:d * 2 * W + W] = pr.astype(BF16)
        p_ref[:, d * 2 * W + W:(d + 1) * 2 * W] = pi.astype(BF16)
    pv = p_ref[...]
    for i in range(0, C, 2):
        w_loc = g2_ref[0, (C - 1 - i) * 128:(2 * C - 1 - i) * 128, :]
        y2 = _dot(lhs, w_loc) + _dot(pv, w2_ref[0, :, i * 128:(i + 2) * 128])
        y_ref[pl.ds(i, nc, stride=C), :] = y2[:, :128]
        y_ref[pl.ds(i + 1, nc, stride=C), :] = y2[:, 128:]


def _s5_core(hn, g2, wz, w2, lam, pw, n_seq, T):
    nc = T // S5_CHUNK
    nt = nc // 8
    W = S5_SLAB * S5_STATE
    n_slab = D_MODEL // 128
    once = pl.Buffered(1)
    slab = lambda c, s: (c, 0, 0)
    return pl.pallas_call(
        functools.partial(_s5_kernel, nc=nc),
        grid=(n_slab, n_seq),
        in_specs=[
            pl.BlockSpec((T, 128), lambda c, s: (s, c)),
            pl.BlockSpec((1, 2 * S5_CHUNK * 128, 256), slab, pipeline_mode=once),
            pl.BlockSpec((1, S5_CHUNK * 128, 4 * W), slab, pipeline_mode=once),
            pl.BlockSpec((1, 4 * W, S5_CHUNK * 128), slab, pipeline_mode=once),
            pl.BlockSpec((1, 2, 8, W), lambda c, s: (c, 0, 0, 0)),
            pl.BlockSpec((1, 2, 2, 8, W), lambda c, s: (c, 0, 0, 0, 0)),
        ],
        out_specs=pl.BlockSpec((T, 128), lambda c, s: (s, c)),
        out_shape=jax.ShapeDtypeStruct(hn.shape, F32),
        scratch_shapes=[
            pltpu.VMEM((nc, S5_CHUNK * 128), BF16), pltpu.VMEM((nc, 4 * W), BF16),
            pltpu.VMEM((2, W // 128, nc, 128), F32), pltpu.VMEM((2, nt, W), F32), pltpu.VMEM((2, nt, W), F32),
            pltpu.VMEM((2, W // 128, nc, 128), F32)],
        compiler_params=_params(("parallel", "arbitrary")),
        name="s5_core",
    )(hn, g2, wz, w2, lam, pw)


def _post1_kernel(x_ref, ys_ref, gain_ref, d_ref, wg_ref, bg_ref, o_ref):
    x = x_ref[...]
    y = _rms(x, gain_ref[...]) * d_ref[...] + ys_ref[...]
    z = 0.5 * y * (1.0 + jnp.tanh(math.sqrt(2.0 / math.pi) * (y + 0.044715 * (y * y * y))))
    gate = _sigmoid(_dot(z.astype(BF16), wg_ref[...]) + bg_ref[...])
    o_ref[...] = x + z * gate


def _post1(x2d, ys, gain, d, wg, bg):
    N = x2d.shape[0]
    tm = min(512, N)
    tile = lambda i: (i, 0)
    const = lambda i: (0, 0)
    return pl.pallas_call(
        _post1_kernel,
        grid=(N // tm,),
        in_specs=[pl.BlockSpec((tm, D_MODEL), tile), pl.BlockSpec((tm, D_MODEL), tile),
                  pl.BlockSpec((1, D_MODEL), const), pl.BlockSpec((1, D_MODEL), const),
                  pl.BlockSpec((D_MODEL, D_MODEL), const), pl.BlockSpec((1, D_MODEL), const)],
        out_specs=pl.BlockSpec((tm, D_MODEL), tile),
        out_shape=jax.ShapeDtypeStruct(x2d.shape, F32),
        compiler_params=_params(("parallel",)),
        name="post1",
    )(x2d, ys, gain, d, wg, bg)


def _s5_tables(p):
    C, G, P, SL = S5_CHUNK, S5_GROUPS, S5_STATE, S5_SLAB
    ns = G // SL
    eye = jnp.eye(SL, dtype=F32)
    b_re = p['s5_b_re'][0].astype(F32)
    b_im = p['s5_b_im'][0].astype(F32)
    steps = jnp.arange(C, dtype=F32)

    def direction(sfx):
        lr = p['s5_lam_re_' + sfx][0].astype(F32)
        li = p['s5_lam_im_' + sfx][0].astype(F32)
        dt = jnp.exp(p['s5_log_dt_' + sfx][0].astype(F32))[:, None]

        def power(k):
            k = k[:, None, None]
            mag = jnp.exp(lr * dt * k)
            return mag * jnp.cos(li * dt * k), mag * jnp.sin(li * dt * k)

        l1r, l1i = power(jnp.ones((1,), F32))
        nr, ni = l1r[0] - 1.0, l1i[0]
        den = lr * lr + li * li
        cr = (nr * lr + ni * li) / den
        ci = (ni * lr - nr * li) / den
        cb_r = cr[:, :, None] * b_re - ci[:, :, None] * b_im
        cb_i = cr[:, :, None] * b_im + ci[:, :, None] * b_re
        c_r = p['s5_c_re_' + sfx][0].astype(F32)
        c_i = p['s5_c_im_' + sfx][0].astype(F32)
        return power, cb_r, cb_i, c_r, c_i

    def kernels(power, cb_r, cb_i, c_r, c_i):
        pr, pi = power(steps)
        d_r = pr[..., None] * cb_r - pi[..., None] * cb_i
        d_i = pr[..., None] * cb_i + pi[..., None] * cb_r
        k = jnp.einsum('gop,lgpi->lgio', c_r, d_r) - jnp.einsum('gop,lgpi->lgio', c_i, d_i)
        k = k.reshape(C, ns, SL, S5_GROUP, 1, S5_GROUP)
        return jnp.where(eye[:, None, :, None] > 0, k, 0.0).reshape(C, ns, 128, 128)

    def state_in(power, cb_r, cb_i, ks):
        pr, pi = power(ks)
        w_r = (pr[..., None] * cb_r - pi[..., None] * cb_i).reshape(C, ns, SL, P, S5_GROUP)
        w_i = (pr[..., None] * cb_i + pi[..., None] * cb_r).reshape(C, ns, SL, P, S5_GROUP)
        return [w_r, w_i]

    def state_out(power, c_r, c_i, ks):
        pr, pi = power(ks)
        e_r = (c_r[None] * pr[:, :, None, :] - c_i[None] * pi[:, :, None, :]).reshape(C, ns, SL, S5_GROUP, P)
        e_i = (c_r[None] * pi[:, :, None, :] + c_i[None] * pr[:, :, None, :]).reshape(C, ns, SL, S5_GROUP, P)
        return [e_r, -e_i]

    def scan_tables(power, ks_rows):
        lr_, li_ = power(C * jnp.array([1.0, 2.0, 4.0, 8.0], F32))
        lam = jnp.stack([lr_, li_], axis=1).reshape(8, ns, SL * P).transpose(1, 0, 2)
        pr, pi = power(C * ks_rows)
        pw = jnp.stack([pr.reshape(8, ns, SL * P), pi.reshape(8, ns, SL * P)], axis=0).transpose(2, 0, 1, 3)
        return lam, pw

    pf = direction('f')
    pb = direction('b')
    kf = kernels(*pf)
    kb = kernels(*pb)
    zero = jnp.zeros((1, ns, 128, 128), F32)
    gen = jnp.concatenate([kf[:0:-1], (kf[0] + kb[0])[None], kb[1:], zero], axis=0)
    gen_prev = jnp.concatenate([zero, gen[:-1]], axis=0)
    g2 = jnp.concatenate([gen, gen_prev], axis=-1).transpose(1, 0, 2, 3).reshape(ns, 2 * C * 128, 256)

    diag = eye[:, None, None, :, None] > 0
    w4 = jnp.stack(state_in(pf[0], pf[1], pf[2], (C - 1) - steps) + state_in(pb[0], pb[1], pb[2], steps))
    w4 = w4.transpose(2, 1, 3, 5, 0, 4)[:, :, :, :, :, None, :]
    wz = jnp.where(diag, w4, 0.0).astype(BF16).reshape(ns, C * 128, 4 * SL * P)
    e4 = jnp.stack(state_out(pf[0], pf[3], pf[4], steps + 1.0) + state_out(pb[0], pb[3], pb[4], C - steps))
    e4 = e4.transpose(2, 0, 3, 5, 1, 4)[:, :, :, :, :, None, :]
    w2 = jnp.where(diag, e4, 0.0).astype(BF16).reshape(ns, 4 * SL * P, C * 128)
    rows = jnp.arange(8, dtype=F32)
    lam_f, pw_f = scan_tables(pf[0], rows)
    lam_b, pw_b = scan_tables(pb[0], 7.0 - rows)
    lam = jnp.stack([lam_f, lam_b], axis=1)
    pw = jnp.stack([pw_f, pw_b], axis=1)
    return g2.astype(BF16), wz, w2, lam, pw


def _rope_tables(T):
    rows = T // GRID_W
    row_ids = jnp.repeat(jnp.arange(rows, dtype=F32), GRID_W)
    col_ids = jnp.tile(jnp.arange(GRID_W, dtype=F32), rows)
    pairs = HEAD // 4
    inv_freq = ROPE_THETA ** (-jnp.arange(pairs, dtype=F32) / pairs)
    ang = jnp.concatenate([row_ids[:, None] * inv_freq, col_ids[:, None] * inv_freq], axis=-1)
    cos = jnp.repeat(jnp.cos(ang), 2, axis=-1)
    sin = jnp.repeat(jnp.sin(ang), 2, axis=-1)
    sign = jnp.tile(jnp.array([-1.0, 1.0], F32), HEAD // 2)
    return jnp.tile(cos, (1, 2)), jnp.tile(sin * sign, (1, 2))


def _layer0_weights(p, T):
    w_in = p['hyb_w_in'][0]
    zc = jnp.zeros((D_MODEL, 64), F32)
    w_all = jnp.concatenate([w_in[:, 0:1728], zc, w_in[:, 1728:2624]], axis=1).astype(BF16)
    mu = p['hyb_shift_mu'][0]
    mu_all = jnp.concatenate([mu[0:1728], jnp.zeros((64,), F32), mu[1728:1856]]).reshape(1, RW_COLS)
    wup = jnp.zeros((384, 2048), F32)
    wup = wup.at[0:64, 0:512].set(p['rwkv_w_up_f'][0])
    wup = wup.at[64:128, 512:1024].set(p['rwkv_w_up_b'][0])
    wup = wup.at[128:192, 1024:1536].set(p['rwkv_a_up'][0])
    wup = wup.at[256:384, 1536:2048].set(p['rwkv_g_up'][0])
    wup = wup.astype(BF16)
    zr = jnp.zeros((512,), F32)
    par = jnp.stack([p['rwkv_w0_f'][0], p['rwkv_w0_b'][0], p['rwkv_a0'][0], p['rwkv_k_k'][0],
                     p['rwkv_k_a'][0], p['rwkv_r_k'][0].reshape(-1), zr, zr]).astype(F32)
    qg = jnp.tile(p['att_q_norm'][0], 8).reshape(1, 512).astype(F32)
    kg = jnp.tile(p['att_k_norm'][0], 2).reshape(1, 128).astype(F32)
    cos_t, sin_t = _rope_tables(T)
    seg = jnp.arange(512) // HEAD
    ones_bd = (seg[:, None] == seg[None, :]).astype(BF16)
    ln = jnp.stack([p['rwkv_lnx_g'][0], p['rwkv_lnx_b'][0]] + [zr] * 6).astype(F32)
    return dict(w_all=w_all, mu_all=mu_all, wup=wup, par=par, qg=qg, kg=kg, cos_t=cos_t, sin_t=sin_t,
                ones_bd=ones_bd, ln=ln, wo=p['hyb_w_out'][0].astype(BF16))


def _row(v):
    return v.reshape(1, -1).astype(F32)


def _mixer0(x2d, p, w, n_seq, T):
    r, k, v, a, b, lf, lb, g, bon, q, k_rep, v_t = _pre0(
        x2d, T, _row(p['mix_norm'][0]), w['w_all'], w['mu_all'], w['wup'], w['par'], w['qg'], w['kg'],
        w['cos_t'], w['sin_t'], w['ones_bd'])
    yf, yb = _wkv(r, k, v, a, b, lf, lb, n_seq, T)
    att = _attention(q, k_rep, v_t, n_seq, T)
    return _post0(x2d, yf, yb, g, bon, att, w['ln'], w['wo'], w['ones_bd'])


def _mixer1(x2d, p, s5, n_seq, T):
    g2, wz, w2, lam, pw = s5
    hn = _norm_f32(x2d, _row(p['mix_norm'][1]))
    ys = _s5_core(hn, g2, wz, w2, lam, pw, n_seq, T)
    return _post1(x2d, ys, _row(p['mix_norm'][1]), _row(p['s5_d'][0]),
                  p['s5_glu_w'][0].astype(BF16), _row(p['s5_glu_b'][0]))


def _ffn(x2d, p, ffn_w, layer):
    return _mlp(x2d, _row(p['ffn_norm'][layer]), ffn_w[layer][0], ffn_w[layer][1])


def _prepare(p, T):
    ffn_w = [(p['ffn_up'][l].astype(BF16), p['ffn_down'][l].astype(BF16)) for l in range(2)]
    return _layer0_weights(p, T), _s5_tables(p), ffn_w


def _trunk(x, p, prep=None):
    n_seq, T, _ = x.shape
    w0, s5, ffn_w = _prepare(p, T) if prep is None else prep
    x2d = x.reshape(n_seq * T, D_MODEL)
    x2d = _ffn(_mixer0(x2d, p, w0, n_seq, T), p, ffn_w, 0)
    x2d = _ffn(_mixer1(x2d, p, s5, n_seq, T), p, ffn_w, 1)
    return x2d.reshape(n_seq, T, D_MODEL)


def kernel(x_prompt, x_sample, mix_norm, ffn_norm, ffn_up, ffn_down, hyb_w_in, hyb_shift_mu, rwkv_w0_f, rwkv_w_up_f, rwkv_w0_b, rwkv_w_up_b, rwkv_a0, rwkv_a_up, rwkv_g_up, rwkv_k_k, rwkv_k_a, rwkv_r_k, rwkv_lnx_g, rwkv_lnx_b, att_q_norm, att_k_norm, hyb_w_out, s5_lam_re_f, s5_lam_im_f, s5_log_dt_f, s5_lam_re_b, s5_lam_im_b, s5_log_dt_b, s5_b_re, s5_b_im, s5_c_re_f, s5_c_im_f, s5_c_re_b, s5_c_im_b, s5_d, s5_glu_w, s5_glu_b):
    p = dict(mix_norm=mix_norm, ffn_norm=ffn_norm, ffn_up=ffn_up, ffn_down=ffn_down,
             hyb_w_in=hyb_w_in, hyb_shift_mu=hyb_shift_mu,
             rwkv_w0_f=rwkv_w0_f, rwkv_w_up_f=rwkv_w_up_f, rwkv_w0_b=rwkv_w0_b, rwkv_w_up_b=rwkv_w_up_b,
             rwkv_a0=rwkv_a0, rwkv_a_up=rwkv_a_up, rwkv_g_up=rwkv_g_up,
             rwkv_k_k=rwkv_k_k, rwkv_k_a=rwkv_k_a, rwkv_r_k=rwkv_r_k,
             rwkv_lnx_g=rwkv_lnx_g, rwkv_lnx_b=rwkv_lnx_b,
             att_q_norm=att_q_norm, att_k_norm=att_k_norm, hyb_w_out=hyb_w_out,
             s5_lam_re_f=s5_lam_re_f, s5_lam_im_f=s5_lam_im_f, s5_log_dt_f=s5_log_dt_f,
             s5_lam_re_b=s5_lam_re_b, s5_lam_im_b=s5_lam_im_b, s5_log_dt_b=s5_log_dt_b,
             s5_b_re=s5_b_re, s5_b_im=s5_b_im,
             s5_c_re_f=s5_c_re_f, s5_c_im_f=s5_c_im_f, s5_c_re_b=s5_c_re_b, s5_c_im_b=s5_c_im_b,
             s5_d=s5_d, s5_glu_w=s5_glu_w, s5_glu_b=s5_glu_b)
    assert x_prompt.shape[1] == x_sample.shape[1]
    prep = _prepare(p, x_prompt.shape[1])
    return (_trunk(x_prompt, p, prep), _trunk(x_sample, p, prep))
```

```python
import functools
import math

import jax
import jax.numpy as jnp
from jax import lax
from jax.experimental import pallas as pl
from jax.experimental.pallas import tpu as pltpu

F32 = jnp.float32
BF16 = jnp.bfloat16

D_MODEL = 1024
D_FF = 4 * D_MODEL
RMS_EPS = 1e-6
GRID_W = 64
RWKV_DIM = 512
HEAD = 64
GN_EPS = 64e-5
N_KV = 2
ROPE_THETA = 10000.0
S5_GROUP = 16
S5_GROUPS = D_MODEL // S5_GROUP
S5_STATE = 64
S5_SLAB = 128 // S5_GROUP

WKV_CHUNK = 64
S5_CHUNK = 16
RW_COLS = 1920
ALL_COLS = 2688
EXP_M05 = math.exp(-0.5)
LOG2_E = math.log2(math.e)
V_ROWS = HEAD + 16
VMEM_LIMIT = 56 * 1024 * 1024


def _dot(a, b):
    return jnp.dot(a, b, preferred_element_type=F32)


def _dot_nt(a, b):
    return lax.dot_general(a, b, (((1,), (1,)), ((), ())), preferred_element_type=F32)


def _split2(x):
    hi = x.astype(BF16)
    lo = (x - hi.astype(F32)).astype(BF16)
    return hi, lo


def _seg_sum(x, ones_bd):
    hi, lo = _split2(x)
    return _dot(hi, ones_bd) + _dot(lo, ones_bd)


def _rms(x, gain):
    return x * lax.rsqrt(jnp.mean(x * x, axis=-1, keepdims=True) + RMS_EPS) * gain


def _sigmoid(x):
    return 1.0 / (1.0 + jnp.exp(-x))


def _params(sem):
    return pltpu.CompilerParams(dimension_semantics=sem, vmem_limit_bytes=VMEM_LIMIT)


def _pre0_kernel(x_ref, xp_ref, xn_ref, gain_ref, w_ref, mu_ref, wup_ref, par_ref, qg_ref, kg_ref,
                 cos_ref, sin_ref, ones_ref,
                 r_o, k_o, v_o, a_o, b_o, lf_o, lb_o, g_o, bon_o, q_o, ka_o, va_o,
                 *, tiles_per_seq, tm):
    pos = pl.program_id(0) % tiles_per_seq
    gain = gain_ref[...]
    xp = jnp.where(pos == 0, 0.0, xp_ref[...])
    xn = jnp.where(pos == tiles_per_seq - 1, 0.0, xn_ref[...])
    x_all = jnp.concatenate([xp, x_ref[...], xn], axis=0)
    H_all = _dot(_rms(x_all, gain).astype(BF16), w_ref[...])
    H = H_all[8:8 + tm]
    Hr = H[:, :RW_COLS]
    Hr_all = H_all[:, :RW_COLS]
    prev = pltpu.roll(Hr_all, 1, 0)[8:8 + tm]
    nxt = pltpu.roll(Hr_all, tm + 15, 0)[8:8 + tm]
    Hs = Hr + mu_ref[...] * (0.5 * (prev + nxt) - Hr)

    ones = ones_ref[...]
    par = par_ref[...]
    r = Hs[:, 0:512]
    k = Hs[:, 512:1024]
    v = Hs[:, 1024:1536]
    act = jnp.concatenate(
        [jnp.tanh(Hs[:, 1536:1664]), Hs[:, 1664:1792], _sigmoid(Hs[:, 1792:1920])], axis=1).astype(BF16)
    up = _dot(act, wup_ref[...])
    lf_o[...] = -EXP_M05 * _sigmoid(par[0:1] + up[:, 0:512])
    lb_o[...] = -EXP_M05 * _sigmoid(par[1:2] + up[:, 512:1024])
    a_sig = _sigmoid(par[2:3] + up[:, 1024:1536])
    g_o[...] = up[:, 1536:2048]
    kk = k * par[3:4]
    kk = kk / jnp.maximum(jnp.sqrt(_seg_sum(kk * kk, ones)), 1e-12)
    k2 = k * (1.0 + (a_sig - 1.0) * par[4:5])
    r_o[...] = r
    k_o[...] = k2
    v_o[...] = v
    a_o[...] = -kk
    b_o[...] = kk * a_sig
    bon_o[...] = _seg_sum(r * k2 * par[5:6], ones) * v

    qa = H[:, RW_COLS:RW_COLS + 512]
    ka = H[:, RW_COLS + 512:RW_COLS + 640]
    va = H[:, RW_COLS + 640:RW_COLS + 768]
    cos = cos_ref[...]
    sin = sin_ref[...]

    def rope(x, c, s):
        n = x.shape[1]
        lane = lax.broadcasted_iota(jnp.int32, x.shape, 1)
        swapped = jnp.where(lane % 2 == 0, pltpu.roll(x, n - 1, 1), pltpu.roll(x, 1, 1))
        return x * c + swapped * s

    qn = qa * lax.rsqrt(_seg_sum(qa * qa, ones) * (1.0 / HEAD) + RMS_EPS) * qg_ref[...]
    qr = rope(qn, jnp.concatenate([cos] * 4, axis=1), jnp.concatenate([sin] * 4, axis=1))
    q_o[...] = (qr * (HEAD ** -0.5 * LOG2_E)).astype(BF16)
    kn = ka * lax.rsqrt(_seg_sum(ka * ka, ones[:128, :128]) * (1.0 / HEAD) + RMS_EPS) * kg_ref[...]
    kr = rope(kn, cos, sin)
    lt64 = lax.broadcasted_iota(jnp.int32, kr.shape, 1) < HEAD

    def rep(x):
        sw = pltpu.roll(x, HEAD, 1)
        return jnp.concatenate([jnp.where(lt64, x, sw), jnp.where(lt64, sw, x)], axis=1).astype(BF16)

    ka_o[...] = rep(kr)
    vt = va.T.astype(BF16)
    one = jnp.ones((V_ROWS - HEAD, tm), BF16)
    va_o[0] = jnp.concatenate([vt[0:HEAD], one, vt[HEAD:2 * HEAD], one], axis=0)


def _pre0(x2d, T, gain, w_all, mu_all, wup, par, qg, kg, cos_t, sin_t, ones_bd):
    N = x2d.shape[0]
    tm = min(256, T)
    tps = T // tm
    nt = N // tm
    t8 = tm // 8
    nb8 = N // 8
    const = lambda i: (0, 0)
    tile = lambda i: (i, 0)
    f512 = jax.ShapeDtypeStruct((N, 512), F32)
    out_shape = [f512] * 9 + [jax.ShapeDtypeStruct((N, 512), BF16), jax.ShapeDtypeStruct((N, 256), BF16),
                              jax.ShapeDtypeStruct((N // T, 2 * V_ROWS, T), BF16)]
    out_specs = [pl.BlockSpec((tm, 512), tile)] * 10 + [
        pl.BlockSpec((tm, 256), tile), pl.BlockSpec((1, 2 * V_ROWS, tm), lambda i: (i // tps, 0, i % tps))]
    return pl.pallas_call(
        functools.partial(_pre0_kernel, tiles_per_seq=tps, tm=tm),
        grid=(nt,),
        in_specs=[
            pl.BlockSpec((tm, D_MODEL), tile),
            pl.BlockSpec((8, D_MODEL), lambda i: (jnp.maximum(i * t8 - 1, 0), 0)),
            pl.BlockSpec((8, D_MODEL), lambda i: (jnp.minimum((i + 1) * t8, nb8 - 1), 0)),
            pl.BlockSpec((1, D_MODEL), const),
            pl.BlockSpec((D_MODEL, ALL_COLS), const),
            pl.BlockSpec((1, RW_COLS), const),
            pl.BlockSpec((384, 2048), const),
            pl.BlockSpec((8, 512), const),
            pl.BlockSpec((1, 512), const),
            pl.BlockSpec((1, 128), const),
            pl.BlockSpec((tm, 128), lambda i: (i % tps, 0)),
            pl.BlockSpec((tm, 128), lambda i: (i % tps, 0)),
            pl.BlockSpec((512, 512), const),
        ],
        out_specs=out_specs,
        out_shape=out_shape,
        compiler_params=_params(("parallel",)),
        name="pre0",
    )(x2d, x2d, x2d, gain, w_all, mu_all, wup, par, qg, kg, cos_t, sin_t, ones_bd)


def _wkv_direction(r, k, v, a, b, L, h_ref, fwd):
    C = WKV_CHUNK
    Q = 4 * HEAD
    ti = lax.broadcasted_iota(jnp.int32, (C, C), 0)
    si = lax.broadcasted_iota(jnp.int32, (C, C), 1)
    tri = jnp.where(si <= ti, 1.0, 0.0).astype(BF16)
    l1 = L.astype(BF16)
    rem = L - l1.astype(F32)
    l2 = rem.astype(BF16)
    l3 = (rem - l2.astype(F32)).astype(BF16)
    cs = _dot(tri, l1) + _dot(tri, l2) + _dot(tri, l3)
    total = cs[C - 1:C, :]
    if fwd:
        cs_incl = cs
        cs_excl = cs - L
    else:
        cs_incl = total - (cs - L)
        cs_excl = total - cs
    e_incl = jnp.exp(cs_incl)
    e_inv = jnp.exp(-cs_incl)
    e_rem = jnp.exp(total - cs_incl)
    a_t = a * jnp.exp(cs_excl)
    r_t = r * e_incl
    b_t = b * e_inv
    k_t = k * e_inv
    b_h = b * e_rem
    k_h = k * e_rem
    gam = jnp.exp(total)

    lane_q = lax.broadcasted_iota(jnp.int32, (C, Q), 1) // HEAD
    ri = lax.broadcasted_iota(jnp.int32, (Q, Q), 0)
    ci = lax.broadcasted_iota(jnp.int32, (Q, Q), 1)
    same = (ri // C) == (ci // C)
    if fwd:
        m_strict = same & ((ci % C) < (ri % C))
        m_incl = same & ((ci % C) <= (ri % C))
    else:
        m_strict = same & ((ci % C) > (ri % C))
        m_incl = same & ((ci % C) >= (ri % C))
    eye = ri == ci

    def stack(xq):
        return jnp.concatenate([jnp.where(lane_q == h, xq, 0.0) for h in range(4)], axis=0)

    streams = []
    for q in range(RWKV_DIM // Q):
        sl = slice(q * Q, (q + 1) * Q)
        streams.append(dict(
            q=q, h_ref=h_ref, m_strict=m_strict, m_incl=m_incl, eye=eye, r_t=r_t[:, sl],
            gam_col=jnp.sum(jnp.where(eye, gam[:, sl], 0.0), axis=1, keepdims=True),
            As=stack(a_t[:, sl]).astype(BF16), Rs=stack(r_t[:, sl]).astype(BF16),
            Bs=stack(b_t[:, sl]).astype(BF16), Ks=stack(k_t[:, sl]).astype(BF16),
            Vs=stack(v[:, sl]).astype(BF16),
            BhT=stack(b_h[:, sl]).T.astype(BF16), KhT=stack(k_h[:, sl]).T.astype(BF16)))
    return streams


def _wkv_solve(streams):
    C = WKV_CHUNK
    Q = 4 * HEAD

    def unstack(xs):
        return xs[0:C] + xs[C:2 * C] + xs[2 * C:3 * C] + xs[3 * C:4 * C]

    for s in streams:
        P = _dot_nt(jnp.concatenate([s['As'], s['Rs']], axis=0), jnp.concatenate([s['Bs'], s['Ks']], axis=0))
        A_ab = jnp.where(s['m_strict'], P[:Q, :Q], 0.0)
        s['A_ak'] = jnp.where(s['m_strict'], P[:Q, Q:], 0.0).astype(BF16)
        s['A_rb'] = jnp.where(s['m_incl'], P[Q:, :Q], 0.0).astype(BF16)
        s['A_rk'] = jnp.where(s['m_incl'], P[Q:, Q:], 0.0).astype(BF16)
        s['T'] = jnp.where(s['eye'], 1.0, A_ab)
        s['Ap'] = A_ab.astype(BF16)
    for _ in range(int(math.log2(C)) - 1):
        for s in streams:
            s['Ap'] = _dot(s['Ap'], s['Ap']).astype(BF16)
        for s in streams:
            s['T'] = s['T'] + _dot(s['T'].astype(BF16), s['Ap'])
    for s in streams:
        s['Z'] = _dot(s['A_ak'], s['Vs']).astype(BF16)
        s['y_k'] = _dot(s['A_rk'], s['Vs'])
        s['n_k'] = _dot(s['KhT'], s['Vs'])
    for s in streams:
        s['X'] = _dot(s['T'].astype(BF16), jnp.concatenate([s['As'], s['Z']], axis=1)).astype(BF16)
    for s in streams:
        W1 = _dot(s['A_rb'], s['X'])
        MN = _dot(s['BhT'], s['X'])
        s['r_p'] = s['r_t'] + unstack(W1[:, :Q])
        s['y_p'] = unstack(W1[:, Q:] + s['y_k'])
        s['M'] = MN[:, :Q]
        s['N'] = MN[:, Q:] + s['n_k']
    ys = []
    for s in streams:
        h0 = s['h_ref'][s['q']]
        seq = _dot(jnp.concatenate([s['r_p'], s['M']], axis=0).astype(BF16), h0.astype(BF16))
        ys.append(seq[:C] + s['y_p'])
        s['h_ref'][s['q']] = s['gam_col'] * h0 + seq[C:] + s['N']
    return ys


def _wkv_kernel(rf, kf, vf, af, bf, lf, rb, kb, vb, ab, bb, lb, yf_o, yb_o, hf_ref, hb_ref):
    @pl.when(pl.program_id(1) == 0)
    def _():
        hf_ref[...] = jnp.zeros_like(hf_ref)
        hb_ref[...] = jnp.zeros_like(hb_ref)

    streams = (_wkv_direction(rf[...], kf[...], vf[...], af[...], bf[...], lf[...], hf_ref, True)
               + _wkv_direction(rb[...], kb[...], vb[...], ab[...], bb[...], lb[...], hb_ref, False))
    ys = _wkv_solve(streams)
    yf_o[...] = jnp.concatenate(ys[0:2], axis=1)
    yb_o[...] = jnp.concatenate(ys[2:4], axis=1)


def _wkv(r, k, v, a, b, lf, lb, n_seq, T):
    C = WKV_CHUNK
    nc = T // C
    fw = lambda s, i: (s * nc + i, 0)
    bw = lambda s, i: (s * nc + nc - 1 - i, 0)
    spec_f = pl.BlockSpec((C, 512), fw)
    spec_b = pl.BlockSpec((C, 512), bw)
    shp = jax.ShapeDtypeStruct(r.shape, F32)
    return pl.pallas_call(
        _wkv_kernel,
        grid=(n_seq, nc),
        in_specs=[spec_f] * 6 + [spec_b] * 6,
        out_specs=[spec_f, spec_b],
        out_shape=[shp, shp],
        scratch_shapes=[pltpu.VMEM((2, 256, 256), F32), pltpu.VMEM((2, 256, 256), F32)],
        compiler_params=_params(("parallel", "arbitrary")),
        name="wkv",
    )(r, k, v, a, b, lf, r, k, v, a, b, lb)


def _attn_kernel(q_ref, k_ref, vt_ref, o_ref, s0_ref, s1_ref, p0_ref, p1_ref, acc_ref, *, tq, tk, T):
    q = q_ref[...].astype(F32)
    lo = lax.broadcasted_iota(jnp.int32, (tq, 128), 1) < HEAD
    qs = jnp.concatenate([
        jnp.where(lo, q[:, 0:128], 0.0), jnp.where(lo, 0.0, q[:, 0:128]),
        jnp.where(lo, q[:, 128:256], 0.0), jnp.where(lo, 0.0, q[:, 128:256])], axis=0)
    qst = qs.T.astype(BF16)
    R = 4 * tq
    n = T // tk

    def chunk(j):
        return pl.ds(j * tk if isinstance(j, int) else pl.multiple_of(j * tk, tk), tk)

    def scores(j):
        return _dot(k_ref[chunk(j), :], qst)

    def values(j, p):
        return _dot(vt_ref[0, :, chunk(j)], p)

    s_buf = (s0_ref, s1_ref)
    p_buf = (p0_ref, p1_ref)
    s_buf[0][...] = scores(0)
    p_buf[1][...] = jnp.zeros((tk, R), BF16)
    acc_ref[...] = jnp.zeros((V_ROWS, R), F32)

    def step(j, b, carry, ahead=True):
        m, alpha_prev = carry
        if ahead:
            s_buf[1 - b][...] = scores(j + 1)
        prev = max(j - 1, 0) if isinstance(j, int) else jnp.maximum(j - 1, 0)
        acc_ref[...] = alpha_prev * acc_ref[...] + values(prev, p_buf[1 - b][...])
        s = s_buf[b][...]
        m_new = jnp.maximum(m, jnp.max(s, axis=0, keepdims=True))
        p_buf[b][...] = jnp.exp2(s - m_new).astype(BF16)
        return m_new, jnp.exp2(m - m_new)

    def body(i, carry):
        return step(2 * i + 1, 1, step(2 * i, 0, carry))

    carry = lax.fori_loop(0, n // 2 - 1, body, (jnp.full((1, R), -1e30, F32), jnp.ones((1, R), F32)))
    carry = step(n - 2, 0, carry)
    _, alpha = step(n - 1, 1, carry, ahead=False)
    acc = alpha * acc_ref[...] + values(n - 1, p_buf[1][...])
    ot = acc[0:HEAD] / acc[HEAD:HEAD + 1]
    o01 = jnp.concatenate([ot[:, 0:tq], ot[:, tq:2 * tq]], axis=0).T
    o23 = jnp.concatenate([ot[:, 2 * tq:3 * tq], ot[:, 3 * tq:4 * tq]], axis=0).T
    o_ref[...] = jnp.concatenate([o01, o23], axis=1)


def _attention(q, k_rep, v_t, n_seq, T):
    tq = min(256, T)
    tk = min(512, T // 2)
    nq = T // tq
    assert (T // tk) % 2 == 0
    return pl.pallas_call(
        functools.partial(_attn_kernel, tq=tq, tk=tk, T=T),
        grid=(n_seq, N_KV, nq),
        in_specs=[
            pl.BlockSpec((tq, 256), lambda s, h, i: (s * nq + i, h)),
            pl.BlockSpec((T, 128), lambda s, h, i: (s, h)),
            pl.BlockSpec((1, V_ROWS, T), lambda s, h, i: (s, h, 0)),
        ],
        out_specs=pl.BlockSpec((tq, 256), lambda s, h, i: (s * nq + i, h)),
        out_shape=jax.ShapeDtypeStruct(q.shape, F32),
        scratch_shapes=[pltpu.VMEM((tk, 4 * tq), F32)] * 2 + [pltpu.VMEM((tk, 4 * tq), BF16)] * 2 + [
            pltpu.VMEM((V_ROWS, 4 * tq), F32)],
        compiler_params=_params(("parallel", "parallel", "arbitrary")),
        name="attention",
    )(q, k_rep, v_t)


def _post0_kernel(x_ref, yf_ref, yb_ref, g_ref, bon_ref, att_ref, ln_ref, wo_ref, ones_ref, o_ref):
    ones = ones_ref[...]
    y = yf_ref[...] + yb_ref[...]
    mean = _seg_sum(y, ones) * (1.0 / HEAD)
    d = y - mean
    var = _seg_sum(d * d, ones) * (1.0 / HEAD)
    yn = d * lax.rsqrt(var + GN_EPS) * ln_ref[0:1] + ln_ref[1:2]
    ya = ((yn + bon_ref[...]) * g_ref[...]).astype(BF16)
    mix = _dot(ya, wo_ref[0:512, :]) + _dot(att_ref[...].astype(BF16), wo_ref[512:1024, :])
    o_ref[...] = x_ref[...] + mix


def _post0(x2d, yf, yb, g, bon, att, ln, wo, ones_bd):
    N = x2d.shape[0]
    tm = min(512, N)
    tile = lambda i: (i, 0)
    const = lambda i: (0, 0)
    return pl.pallas_call(
        _post0_kernel,
        grid=(N // tm,),
        in_specs=[pl.BlockSpec((tm, D_MODEL), tile)] + [pl.BlockSpec((tm, 512), tile)] * 5 + [
            pl.BlockSpec((8, 512), const), pl.BlockSpec((D_MODEL, D_MODEL), const), pl.BlockSpec((512, 512), const)],
        out_specs=pl.BlockSpec((tm, D_MODEL), tile),
        out_shape=jax.ShapeDtypeStruct(x2d.shape, F32),
        compiler_params=_params(("parallel",)),
        name="post0",
    )(x2d, yf, yb, g, bon, att, ln, wo, ones_bd)


def _mlp_kernel(x_ref, gain_ref, wu_ref, wd_ref, o_ref, hn_ref):
    @pl.when(pl.program_id(1) == 0)
    def _():
        x = x_ref[...]
        hn_ref[...] = _rms(x, gain_ref[...]).astype(BF16)
        o_ref[...] = x

    u = _dot(hn_ref[...], wu_ref[...])
    u = jnp.maximum(u, 0.0)
    o_ref[...] += _dot((u * u).astype(BF16), wd_ref[...])


def _mlp(x2d, gain, w_up, w_down):
    N = x2d.shape[0]
    tm = min(512, N)
    fc = 1024
    return pl.pallas_call(
        _mlp_kernel,
        grid=(N // tm, D_FF // fc),
        in_specs=[
            pl.BlockSpec((tm, D_MODEL), lambda i, j: (i, 0)),
            pl.BlockSpec((1, D_MODEL), lambda i, j: (0, 0)),
            pl.BlockSpec((D_MODEL, fc), lambda i, j: (0, j)),
            pl.BlockSpec((fc, D_MODEL), lambda i, j: (j, 0)),
        ],
        out_specs=pl.BlockSpec((tm, D_MODEL), lambda i, j: (i, 0)),
        out_shape=jax.ShapeDtypeStruct(x2d.shape, F32),
        scratch_shapes=[pltpu.VMEM((tm, D_MODEL), BF16)],
        compiler_params=_params(("parallel", "arbitrary")),
        name="mlp",
    )(x2d, gain, w_up, w_down)


def _norm_kernel(x_ref, gain_ref, o_ref):
    o_ref[...] = _rms(x_ref[...], gain_ref[...]).astype(o_ref.dtype)


def _norm_f32(x2d, gain):
    N = x2d.shape[0]
    tm = min(1024, N)
    return pl.pallas_call(
        _norm_kernel,
        grid=(N // tm,),
        in_specs=[pl.BlockSpec((tm, D_MODEL), lambda i: (i, 0)), pl.BlockSpec((1, D_MODEL), lambda i: (0, 0))],
        out_specs=pl.BlockSpec((tm, D_MODEL), lambda i: (i, 0)),
        out_shape=jax.ShapeDtypeStruct(x2d.shape, F32),
        compiler_params=_params(("parallel",)),
        name="s5_norm",
    )(x2d, gain)


def _cmul_add(ar, ai, br, bi, cr, ci):
    return ar * br - ai * bi + cr, ar * bi + ai * br + ci


def _s5_scan(zr, zi, lam, pw_r, pw_i, fwd, nc, xs_ref, cb_ref):
    W = zr.shape[1]
    nt = nc // 8
    row = lax.broadcasted_iota(jnp.int32, (nc, W), 0) % 8

    def shift(x, s):
        if fwd:
            return jnp.where(row >= s, pltpu.roll(x, s, 0), 0.0)
        return jnp.where(row < 8 - s, pltpu.roll(x, nc - s, 0), 0.0)

    xr, xi = zr, zi
    for lvl, s in enumerate((1, 2, 4)):
        xr, xi = _cmul_add(lam[2 * lvl:2 * lvl + 1], lam[2 * lvl + 1:2 * lvl + 2], shift(xr, s), shift(xi, s), xr, xi)
        yield
    last = 7 if fwd else 0
    nl = W // 128
    tot = []
    for c, x in enumerate((xr, xi)):
        for k in range(nl):
            xs_ref[c, k] = x[:, k * 128:(k + 1) * 128]
        tot.append(jnp.concatenate(
            [xs_ref[c, k, pl.ds(last, nt, stride=8), :] for k in range(nl)], axis=1))
    yield
    l8r, l8i = lam[6:7], lam[7:8]
    cr = ci = jnp.zeros((1, W), F32)
    rows_r = [None] * nt
    rows_i = [None] * nt
    for n, t in enumerate(range(nt) if fwd else range(nt - 1, -1, -1)):
        rows_r[t] = cr
        rows_i[t] = ci
        cr, ci = _cmul_add(l8r, l8i, cr, ci, tot[0][t:t + 1], tot[1][t:t + 1])
        if n % 16 == 15:
            yield
    for c, rows in enumerate((rows_r, rows_i)):
        car = jnp.concatenate(rows, axis=0)
        for k in range(nl):
            for r in range(8):
                cb_ref[c, k, pl.ds(r, nt, stride=8), :] = car[:, k * 128:(k + 1) * 128]
    yield
    cbr = jnp.concatenate([cb_ref[0, k] for k in range(nl)], axis=1)
    cbi = jnp.concatenate([cb_ref[1, k] for k in range(nl)], axis=1)
    pr = jnp.concatenate([pw_r] * nt, axis=0)
    pi = jnp.concatenate([pw_i] * nt, axis=0)
    return _cmul_add(pr, pi, cbr, cbi, shift(xr, 1), shift(xi, 1))


def _s5_kernel(x_ref, g2_ref, wz_ref, w2_ref, lam_ref, pw_ref, y_ref, lhs_ref, p_ref, xs_ref, cb_ref, *, nc):
    C = S5_CHUNK
    W = S5_SLAB * S5_STATE
    for j in range(C):
        lhs_ref[:, j * 128:(j + 1) * 128] = x_ref[pl.ds(j, nc, stride=C), :].astype(BF16)
    lhs = lhs_ref[...]
    scans = []
    for d in range(2):
        z = _dot(lhs, wz_ref[0, :, d * 2 * W:(d + 1) * 2 * W])
        scans.append(_s5_scan(z[:, :W], z[:, W:], lam_ref[0, d], pw_ref[0, d, 0], pw_ref[0, d, 1], d == 0, nc,
                              xs_ref.at[d], cb_ref.at[d]))
    pairs = list(range(0, C, 2))
    y_loc = {}
    states = [None, None]
    live = [0, 1]
    todo = list(pairs)
    while live or todo:
        for d in list(live):
            try:
                next(scans[d])
            except StopIteration as done:
                states[d] = done.value
                live.remove(d)
        if todo:
            i = todo.pop(0)
            y_loc[i] = _dot(lhs, g2_ref[0, (C - 1 - i) * 128:(2 * C - 1 - i) * 128, :])
    for d, (pr, pi) in enumerate(states):
        p_ref[:, d * 2 * W:d * 2 * W + W] = pr.astype(BF16)
        p_ref[:, d * 2 * W + W:(d + 1) * 2 * W] = pi.astype(BF16)
    pv = p_ref[...]
    for i in pairs:
        y2 = y_loc[i] + _dot(pv, w2_ref[0, :, i * 128:(i + 2) * 128])
        y_ref[pl.ds(i, nc, stride=C), :] = y2[:, :128]
        y_ref[pl.ds(i + 1, nc, stride=C), :] = y2[:, 128:]


def _s5_core(hn, g2, wz, w2, lam, pw, n_seq, T):
    nc = T // S5_CHUNK
    nt = nc // 8
    W = S5_SLAB * S5_STATE
    n_slab = D_MODEL // 128
    once = pl.Buffered(1)
    slab = lambda c, s: (c, 0, 0)
    return pl.pallas_call(
        functools.partial(_s5_kernel, nc=nc),
        grid=(n_slab, n_seq),
        in_specs=[
            pl.BlockSpec((T, 128), lambda c, s: (s, c)),
            pl.BlockSpec((1, 2 * S5_CHUNK * 128, 256), slab, pipeline_mode=once),
            pl.BlockSpec((1, S5_CHUNK * 128, 4 * W), slab, pipeline_mode=once),
            pl.BlockSpec((1, 4 * W, S5_CHUNK * 128), slab, pipeline_mode=once),
            pl.BlockSpec((1, 2, 8, W), lambda c, s: (c, 0, 0, 0)),
            pl.BlockSpec((1, 2, 2, 8, W), lambda c, s: (c, 0, 0, 0, 0)),
        ],
        out_specs=pl.BlockSpec((T, 128), lambda c, s: (s, c)),
        out_shape=jax.ShapeDtypeStruct(hn.shape, F32),
        scratch_shapes=[
            pltpu.VMEM((nc, S5_CHUNK * 128), BF16), pltpu.VMEM((nc, 4 * W), BF16),
            pltpu.VMEM((2, 2, W // 128, nc, 128), F32), pltpu.VMEM((2, 2, W // 128, nc, 128), F32)],
        compiler_params=_params(("parallel", "arbitrary")),
        name="s5_core",
    )(hn, g2, wz, w2, lam, pw)


def _post1_kernel(x_ref, ys_ref, gain_ref, d_ref, wg_ref, bg_ref, o_ref):
    x = x_ref[...]
    y = _rms(x, gain_ref[...]) * d_ref[...] + ys_ref[...]
    z = 0.5 * y * (1.0 + jnp.tanh(math.sqrt(2.0 / math.pi) * (y + 0.044715 * (y * y * y))))
    gate = _sigmoid(_dot(z.astype(BF16), wg_ref[...]) + bg_ref[...])
    o_ref[...] = x + z * gate


def _post1(x2d, ys, gain, d, wg, bg):
    N = x2d.shape[0]
    tm = min(512, N)
    tile = lambda i: (i, 0)
    const = lambda i: (0, 0)
    return pl.pallas_call(
        _post1_kernel,
        grid=(N // tm,),
        in_specs=[pl.BlockSpec((tm, D_MODEL), tile), pl.BlockSpec((tm, D_MODEL), tile),
                  pl.BlockSpec((1, D_MODEL), const), pl.BlockSpec((1, D_MODEL), const),
                  pl.BlockSpec((D_MODEL, D_MODEL), const), pl.BlockSpec((1, D_MODEL), const)],
        out_specs=pl.BlockSpec((tm, D_MODEL), tile),
        out_shape=jax.ShapeDtypeStruct(x2d.shape, F32),
        compiler_params=_params(("parallel",)),
        name="post1",
    )(x2d, ys, gain, d, wg, bg)


def _s5_tables(p):
    C, G, P, SL = S5_CHUNK, S5_GROUPS, S5_STATE, S5_SLAB
    ns = G // SL
    eye = jnp.eye(SL, dtype=F32)
    b_re = p['s5_b_re'][0].astype(F32)
    b_im = p['s5_b_im'][0].astype(F32)
    steps = jnp.arange(C, dtype=F32)

    def direction(sfx):
        lr = p['s5_lam_re_' + sfx][0].astype(F32)
        li = p['s5_lam_im_' + sfx][0].astype(F32)
        dt = jnp.exp(p['s5_log_dt_' + sfx][0].astype(F32))[:, None]

        def power(k):
            k = k[:, None, None]
            mag = jnp.exp(lr * dt * k)
            return mag * jnp.cos(li * dt * k), mag * jnp.sin(li * dt * k)

        l1r, l1i = power(jnp.ones((1,), F32))
        nr, ni = l1r[0] - 1.0, l1i[0]
        den = lr * lr + li * li
        cr = (nr * lr + ni * li) / den
        ci = (ni * lr - nr * li) / den
        cb_r = cr[:, :, None] * b_re - ci[:, :, None] * b_im
        cb_i = cr[:, :, None] * b_im + ci[:, :, None] * b_re
        c_r = p['s5_c_re_' + sfx][0].astype(F32)
        c_i = p['s5_c_im_' + sfx][0].astype(F32)
        return power, cb_r, cb_i, c_r, c_i

    def kernels(power, cb_r, cb_i, c_r, c_i):
        pr, pi = power(steps)
        d_r = pr[..., None] * cb_r - pi[..., None] * cb_i
        d_i = pr[..., None] * cb_i + pi[..., None] * cb_r
        k = jnp.einsum('gop,lgpi->lgio', c_r, d_r) - jnp.einsum('gop,lgpi->lgio', c_i, d_i)
        k = k.reshape(C, ns, SL, S5_GROUP, 1, S5_GROUP)
        return jnp.where(eye[:, None, :, None] > 0, k, 0.0).reshape(C, ns, 128, 128)

    def state_in(power, cb_r, cb_i, ks):
        pr, pi = power(ks)
        w_r = (pr[..., None] * cb_r - pi[..., None] * cb_i).reshape(C, ns, SL, P, S5_GROUP)
        w_i = (pr[..., None] * cb_i + pi[..., None] * cb_r).reshape(C, ns, SL, P, S5_GROUP)
        return [w_r, w_i]

    def state_out(power, c_r, c_i, ks):
        pr, pi = power(ks)
        e_r = (c_r[None] * pr[:, :, None, :] - c_i[None] * pi[:, :, None, :]).reshape(C, ns, SL, S5_GROUP, P)
        e_i = (c_r[None] * pi[:, :, None, :] + c_i[None] * pr[:, :, None, :]).reshape(C, ns, SL, S5_GROUP, P)
        return [e_r, -e_i]

    def scan_tables(power, ks_rows):
        lr_, li_ = power(C * jnp.array([1.0, 2.0, 4.0, 8.0], F32))
        lam = jnp.stack([lr_, li_], axis=1).reshape(8, ns, SL * P).transpose(1, 0, 2)
        pr, pi = power(C * ks_rows)
        pw = jnp.stack([pr.reshape(8, ns, SL * P), pi.reshape(8, ns, SL * P)], axis=0).transpose(2, 0, 1, 3)
        return lam, pw

    pf = direction('f')
    pb = direction('b')
    kf = kernels(*pf)
    kb = kernels(*pb)
    zero = jnp.zeros((1, ns, 128, 128), F32)
    gen = jnp.concatenate([kf[:0:-1], (kf[0] + kb[0])[None], kb[1:], zero], axis=0)
    gen_prev = jnp.concatenate([zero, gen[:-1]], axis=0)
    g2 = jnp.concatenate([gen, gen_prev], axis=-1).transpose(1, 0, 2, 3).reshape(ns, 2 * C * 128, 256)

    diag = eye[:, None, None, :, None] > 0
    w4 = jnp.stack(state_in(pf[0], pf[1], pf[2], (C - 1) - steps) + state_in(pb[0], pb[1], pb[2], steps))
    w4 = w4.transpose(2, 1, 3, 5, 0, 4)[:, :, :, :, :, None, :]
    wz = jnp.where(diag, w4, 0.0).astype(BF16).reshape(ns, C * 128, 4 * SL * P)
    e4 = jnp.stack(state_out(pf[0], pf[3], pf[4], steps + 1.0) + state_out(pb[0], pb[3], pb[4], C - steps))
    e4 = e4.transpose(2, 0, 3, 5, 1, 4)[:, :, :, :, :, None, :]
    w2 = jnp.where(diag, e4, 0.0).astype(BF16).reshape(ns, 4 * SL * P, C * 128)
    rows = jnp.arange(8, dtype=F32)
    lam_f, pw_f = scan_tables(pf[0], rows)
    lam_b, pw_b = scan_tables(pb[0], 7.0 - rows)
    lam = jnp.stack([lam_f, lam_b], axis=1)
    pw = jnp.stack([pw_f, pw_b], axis=1)
    return g2.astype(BF16), wz, w2, lam, pw


def _rope_tables(T):
    rows = T // GRID_W
    row_ids = jnp.repeat(jnp.arange(rows, dtype=F32), GRID_W)
    col_ids = jnp.tile(jnp.arange(GRID_W, dtype=F32), rows)
    pairs = HEAD // 4
    inv_freq = ROPE_THETA ** (-jnp.arange(pairs, dtype=F32) / pairs)
    ang = jnp.concatenate([row_ids[:, None] * inv_freq, col_ids[:, None] * inv_freq], axis=-1)
    cos = jnp.repeat(jnp.cos(ang), 2, axis=-1)
    sin = jnp.repeat(jnp.sin(ang), 2, axis=-1)
    sign = jnp.tile(jnp.array([-1.0, 1.0], F32), HEAD // 2)
    return jnp.tile(cos, (1, 2)), jnp.tile(sin * sign, (1, 2))


def _layer0_weights(p, T):
    w_in = p['hyb_w_in'][0]
    zc = jnp.zeros((D_MODEL, 64), F32)
    w_all = jnp.concatenate([w_in[:, 0:1728], zc, w_in[:, 1728:2624]], axis=1).astype(BF16)
    mu = p['hyb_shift_mu'][0]
    mu_all = jnp.concatenate([mu[0:1728], jnp.zeros((64,), F32), mu[1728:1856]]).reshape(1, RW_COLS)
    wup = jnp.zeros((384, 2048), F32)
    wup = wup.at[0:64, 0:512].set(p['rwkv_w_up_f'][0])
    wup = wup.at[64:128, 512:1024].set(p['rwkv_w_up_b'][0])
    wup = wup.at[128:192, 1024:1536].set(p['rwkv_a_up'][0])
    wup = wup.at[256:384, 1536:2048].set(p['rwkv_g_up'][0])
    wup = wup.astype(BF16)
    zr = jnp.zeros((512,), F32)
    par = jnp.stack([p['rwkv_w0_f'][0], p['rwkv_w0_b'][0], p['rwkv_a0'][0], p['rwkv_k_k'][0],
                     p['rwkv_k_a'][0], p['rwkv_r_k'][0].reshape(-1), zr, zr]).astype(F32)
    qg = jnp.tile(p['att_q_norm'][0], 8).reshape(1, 512).astype(F32)
    kg = jnp.tile(p['att_k_norm'][0], 2).reshape(1, 128).astype(F32)
    cos_t, sin_t = _rope_tables(T)
    seg = jnp.arange(512) // HEAD
    ones_bd = (seg[:, None] == seg[None, :]).astype(BF16)
    ln = jnp.stack([p['rwkv_lnx_g'][0], p['rwkv_lnx_b'][0]] + [zr] * 6).astype(F32)
    return dict(w_all=w_all, mu_all=mu_all, wup=wup, par=par, qg=qg, kg=kg, cos_t=cos_t, sin_t=sin_t,
                ones_bd=ones_bd, ln=ln, wo=p['hyb_w_out'][0].astype(BF16))


def _row(v):
    return v.reshape(1, -1).astype(F32)


def _mixer0(x2d, p, w, n_seq, T):
    r, k, v, a, b, lf, lb, g, bon, q, k_rep, v_t = _pre0(
        x2d, T, _row(p['mix_norm'][0]), w['w_all'], w['mu_all'], w['wup'], w['par'], w['qg'], w['kg'],
        w['cos_t'], w['sin_t'], w['ones_bd'])
    yf, yb = _wkv(r, k, v, a, b, lf, lb, n_seq, T)
    att = _attention(q, k_rep, v_t, n_seq, T)
    return _post0(x2d, yf, yb, g, bon, att, w['ln'], w['wo'], w['ones_bd'])


def _mixer1(x2d, p, s5, n_seq, T):
    g2, wz, w2, lam, pw = s5
    hn = _norm_f32(x2d, _row(p['mix_norm'][1]))
    ys = _s5_core(hn, g2, wz, w2, lam, pw, n_seq, T)
    return _post1(x2d, ys, _row(p['mix_norm'][1]), _row(p['s5_d'][0]),
                  p['s5_glu_w'][0].astype(BF16), _row(p['s5_glu_b'][0]))


def _ffn(x2d, p, ffn_w, layer):
    return _mlp(x2d, _row(p['ffn_norm'][layer]), ffn_w[layer][0], ffn_w[layer][1])


def _prepare(p, T):
    ffn_w = [(p['ffn_up'][l].astype(BF16), p['ffn_down'][l].astype(BF16)) for l in range(2)]
    return _layer0_weights(p, T), _s5_tables(p), ffn_w


def _trunk(x, p, prep=None):
    n_seq, T, _ = x.shape
    w0, s5, ffn_w = _prepare(p, T) if prep is None else prep
    x2d = x.reshape(n_seq * T, D_MODEL)
    x2d = _ffn(_mixer0(x2d, p, w0, n_seq, T), p, ffn_w, 0)
    x2d = _ffn(_mixer1(x2d, p, s5, n_seq, T), p, ffn_w, 1)
    return x2d.reshape(n_seq, T, D_MODEL)


def kernel(x_prompt, x_sample, mix_norm, ffn_norm, ffn_up, ffn_down, hyb_w_in, hyb_shift_mu, rwkv_w0_f, rwkv_w_up_f, rwkv_w0_b, rwkv_w_up_b, rwkv_a0, rwkv_a_up, rwkv_g_up, rwkv_k_k, rwkv_k_a, rwkv_r_k, rwkv_lnx_g, rwkv_lnx_b, att_q_norm, att_k_norm, hyb_w_out, s5_lam_re_f, s5_lam_im_f, s5_log_dt_f, s5_lam_re_b, s5_lam_im_b, s5_log_dt_b, s5_b_re, s5_b_im, s5_c_re_f, s5_c_im_f, s5_c_re_b, s5_c_im_b, s5_d, s5_glu_w, s5_glu_b):
    p = dict(mix_norm=mix_norm, ffn_norm=ffn_norm, ffn_up=ffn_up, ffn_down=ffn_down,
             hyb_w_in=hyb_w_in, hyb_shift_mu=hyb_shift_mu,
             rwkv_w0_f=rwkv_w0_f, rwkv_w_up_f=rwkv_w_up_f, rwkv_w0_b=rwkv_w0_b, rwkv_w_up_b=rwkv_w_up_b,
             rwkv_a0=rwkv_a0, rwkv_a_up=rwkv_a_up, rwkv_g_up=rwkv_g_up,
             rwkv_k_k=rwkv_k_k, rwkv_k_a=rwkv_k_a, rwkv_r_k=rwkv_r_k,
             rwkv_lnx_g=rwkv_lnx_g, rwkv_lnx_b=rwkv_lnx_b,
             att_q_norm=att_q_norm, att_k_norm=att_k_norm, hyb_w_out=hyb_w_out,
             s5_lam_re_f=s5_lam_re_f, s5_lam_im_f=s5_lam_im_f, s5_log_dt_f=s5_log_dt_f,
             s5_lam_re_b=s5_lam_re_b, s5_lam_im_b=s5_lam_im_b, s5_log_dt_b=s5_log_dt_b,
             s5_b_re=s5_b_re, s5_b_im=s5_b_im,
             s5_c_re_f=s5_c_re_f, s5_c_im_f=s5_c_im_f, s5_c_re_b=s5_c_re_b, s5_c_im_b=s5_c_im_b,
             s5_d=s5_d, s5_glu_w=s5_glu_w, s5_glu_b=s5_glu_b)
    assert x_prompt.shape[1] == x_sample.shape[1]
    prep = _prepare(p, x_prompt.shape[1])
    return (_trunk(x_prompt, p, prep), _trunk(x_sample, p, prep))
```

```python
import functools
import math

import jax
import jax.numpy as jnp
from jax import lax
from jax.experimental import pallas as pl
from jax.experimental.pallas import tpu as pltpu

F32 = jnp.float32
BF16 = jnp.bfloat16

D_MODEL = 1024
D_FF = 4 * D_MODEL
RMS_EPS = 1e-6
GRID_W = 64
RWKV_DIM = 512
HEAD = 64
GN_EPS = 64e-5
N_KV = 2
ROPE_THETA = 10000.0
S5_GROUP = 16
S5_GROUPS = D_MODEL // S5_GROUP
S5_STATE = 64
S5_SLAB = 128 // S5_GROUP

WKV_CHUNK = 64
S5_CHUNK = 16
RW_COLS = 1920
ALL_COLS = 2688
EXP_M05 = math.exp(-0.5)
LOG2_E = math.log2(math.e)
V_ROWS = HEAD + 16
VMEM_LIMIT = 56 * 1024 * 1024


def _dot(a, b):
    return jnp.dot(a, b, preferred_element_type=F32)


def _dot_nt(a, b):
    return lax.dot_general(a, b, (((1,), (1,)), ((), ())), preferred_element_type=F32)


def _split2(x):
    hi = x.astype(BF16)
    lo = (x - hi.astype(F32)).astype(BF16)
    return hi, lo


def _seg_sum(x, ones_bd):
    hi, lo = _split2(x)
    return _dot(hi, ones_bd) + _dot(lo, ones_bd)


def _rms(x, gain):
    return x * lax.rsqrt(jnp.mean(x * x, axis=-1, keepdims=True) + RMS_EPS) * gain


def _sigmoid(x):
    return 1.0 / (1.0 + jnp.exp(-x))


def _params(sem):
    return pltpu.CompilerParams(dimension_semantics=sem, vmem_limit_bytes=VMEM_LIMIT)


def _pre0_kernel(x_ref, xp_ref, xn_ref, gain_ref, w_ref, mu_ref, wup_ref, par_ref, qg_ref, kg_ref,
                 cos_ref, sin_ref, ones_ref,
                 r_o, k_o, v_o, a_o, b_o, lf_o, lb_o, g_o, bon_o, q_o, ka_o, va_o,
                 *, tiles_per_seq, tm):
    pos = pl.program_id(0) % tiles_per_seq
    gain = gain_ref[...]
    xp = jnp.where(pos == 0, 0.0, xp_ref[...])
    xn = jnp.where(pos == tiles_per_seq - 1, 0.0, xn_ref[...])
    x_all = jnp.concatenate([xp, x_ref[...], xn], axis=0)
    H_all = _dot(_rms(x_all, gain).astype(BF16), w_ref[...])
    H = H_all[8:8 + tm]
    Hr = H[:, :RW_COLS]
    Hr_all = H_all[:, :RW_COLS]
    prev = pltpu.roll(Hr_all, 1, 0)[8:8 + tm]
    nxt = pltpu.roll(Hr_all, tm + 15, 0)[8:8 + tm]
    Hs = Hr + mu_ref[...] * (0.5 * (prev + nxt) - Hr)

    ones = ones_ref[...]
    par = par_ref[...]
    r = Hs[:, 0:512]
    k = Hs[:, 512:1024]
    v = Hs[:, 1024:1536]
    act = jnp.concatenate(
        [jnp.tanh(Hs[:, 1536:1664]), Hs[:, 1664:1792], _sigmoid(Hs[:, 1792:1920])], axis=1).astype(BF16)
    up = _dot(act, wup_ref[...])
    lf_o[...] = -EXP_M05 * _sigmoid(par[0:1] + up[:, 0:512])
    lb_o[...] = -EXP_M05 * _sigmoid(par[1:2] + up[:, 512:1024])
    a_sig = _sigmoid(par[2:3] + up[:, 1024:1536])
    g_o[...] = up[:, 1536:2048]
    kk = k * par[3:4]
    kk = kk / jnp.maximum(jnp.sqrt(_seg_sum(kk * kk, ones)), 1e-12)
    k2 = k * (1.0 + (a_sig - 1.0) * par[4:5])
    r_o[...] = r
    k_o[...] = k2
    v_o[...] = v
    a_o[...] = -kk
    b_o[...] = kk * a_sig
    bon_o[...] = _seg_sum(r * k2 * par[5:6], ones) * v

    qa = H[:, RW_COLS:RW_COLS + 512]
    ka = H[:, RW_COLS + 512:RW_COLS + 640]
    va = H[:, RW_COLS + 640:RW_COLS + 768]
    cos = cos_ref[...]
    sin = sin_ref[...]

    def rope(x, c, s):
        n = x.shape[1]
        lane = lax.broadcasted_iota(jnp.int32, x.shape, 1)
        swapped = jnp.where(lane % 2 == 0, pltpu.roll(x, n - 1, 1), pltpu.roll(x, 1, 1))
        return x * c + swapped * s

    qn = qa * lax.rsqrt(_seg_sum(qa * qa, ones) * (1.0 / HEAD) + RMS_EPS) * qg_ref[...]
    qr = rope(qn, jnp.concatenate([cos] * 4, axis=1), jnp.concatenate([sin] * 4, axis=1))
    q_o[...] = (qr * (HEAD ** -0.5 * LOG2_E)).astype(BF16)
    kn = ka * lax.rsqrt(_seg_sum(ka * ka, ones[:128, :128]) * (1.0 / HEAD) + RMS_EPS) * kg_ref[...]
    kr = rope(kn, cos, sin)
    lt64 = lax.broadcasted_iota(jnp.int32, kr.shape, 1) < HEAD

    def rep(x):
        sw = pltpu.roll(x, HEAD, 1)
        return jnp.concatenate([jnp.where(lt64, x, sw), jnp.where(lt64, sw, x)], axis=1).astype(BF16)

    ka_o[...] = rep(kr)
    vt = va.T.astype(BF16)
    one = jnp.ones((V_ROWS - HEAD, tm), BF16)
    va_o[0] = jnp.concatenate([vt[0:HEAD], one, vt[HEAD:2 * HEAD], one], axis=0)


def _pre0(x2d, T, gain, w_all, mu_all, wup, par, qg, kg, cos_t, sin_t, ones_bd):
    N = x2d.shape[0]
    tm = min(256, T)
    tps = T // tm
    nt = N // tm
    t8 = tm // 8
    nb8 = N // 8
    const = lambda i: (0, 0)
    tile = lambda i: (i, 0)
    f512 = jax.ShapeDtypeStruct((N, 512), F32)
    out_shape = [f512] * 9 + [jax.ShapeDtypeStruct((N, 512), BF16), jax.ShapeDtypeStruct((N, 256), BF16),
                              jax.ShapeDtypeStruct((N // T, 2 * V_ROWS, T), BF16)]
    out_specs = [pl.BlockSpec((tm, 512), tile)] * 10 + [
        pl.BlockSpec((tm, 256), tile), pl.BlockSpec((1, 2 * V_ROWS, tm), lambda i: (i // tps, 0, i % tps))]
    return pl.pallas_call(
        functools.partial(_pre0_kernel, tiles_per_seq=tps, tm=tm),
        grid=(nt,),
        in_specs=[
            pl.BlockSpec((tm, D_MODEL), tile),
            pl.BlockSpec((8, D_MODEL), lambda i: (jnp.maximum(i * t8 - 1, 0), 0)),
            pl.BlockSpec((8, D_MODEL), lambda i: (jnp.minimum((i + 1) * t8, nb8 - 1), 0)),
            pl.BlockSpec((1, D_MODEL), const),
            pl.BlockSpec((D_MODEL, ALL_COLS), const),
            pl.BlockSpec((1, RW_COLS), const),
            pl.BlockSpec((384, 2048), const),
            pl.BlockSpec((8, 512), const),
            pl.BlockSpec((1, 512), const),
            pl.BlockSpec((1, 128), const),
            pl.BlockSpec((tm, 128), lambda i: (i % tps, 0)),
            pl.BlockSpec((tm, 128), lambda i: (i % tps, 0)),
            pl.BlockSpec((512, 512), const),
        ],
        out_specs=out_specs,
        out_shape=out_shape,
        compiler_params=_params(("parallel",)),
        name="pre0",
    )(x2d, x2d, x2d, gain, w_all, mu_all, wup, par, qg, kg, cos_t, sin_t, ones_bd)


def _wkv_direction(r, k, v, a, b, L, h_ref, fwd):
    C = WKV_CHUNK
    Q = 4 * HEAD
    ti = lax.broadcasted_iota(jnp.int32, (C, C), 0)
    si = lax.broadcasted_iota(jnp.int32, (C, C), 1)
    tri = jnp.where(si <= ti, 1.0, 0.0).astype(BF16)
    l1 = L.astype(BF16)
    rem = L - l1.astype(F32)
    l2 = rem.astype(BF16)
    l3 = (rem - l2.astype(F32)).astype(BF16)
    cs = _dot(tri, l1) + _dot(tri, l2) + _dot(tri, l3)
    total = cs[C - 1:C, :]
    if fwd:
        cs_incl = cs
        cs_excl = cs - L
    else:
        cs_incl = total - (cs - L)
        cs_excl = total - cs
    e_incl = jnp.exp(cs_incl)
    e_inv = jnp.exp(-cs_incl)
    e_rem = jnp.exp(total - cs_incl)
    a_t = a * jnp.exp(cs_excl)
    r_t = r * e_incl
    b_t = b * e_inv
    k_t = k * e_inv
    b_h = b * e_rem
    k_h = k * e_rem
    gam = jnp.exp(total)

    lane_q = lax.broadcasted_iota(jnp.int32, (C, Q), 1) // HEAD
    ri = lax.broadcasted_iota(jnp.int32, (Q, Q), 0)
    ci = lax.broadcasted_iota(jnp.int32, (Q, Q), 1)
    same = (ri // C) == (ci // C)
    if fwd:
        m_strict = same & ((ci % C) < (ri % C))
        m_incl = same & ((ci % C) <= (ri % C))
    else:
        m_strict = same & ((ci % C) > (ri % C))
        m_incl = same & ((ci % C) >= (ri % C))
    eye = ri == ci

    def stack(xq):
        return jnp.concatenate([jnp.where(lane_q == h, xq, 0.0) for h in range(4)], axis=0)

    streams = []
    for q in range(RWKV_DIM // Q):
        sl = slice(q * Q, (q + 1) * Q)
        streams.append(dict(
            q=q, h_ref=h_ref, m_strict=m_strict, m_incl=m_incl, eye=eye, r_t=r_t[:, sl],
            gam_col=jnp.sum(jnp.where(eye, gam[:, sl], 0.0), axis=1, keepdims=True),
            As=stack(a_t[:, sl]).astype(BF16), Rs=stack(r_t[:, sl]).astype(BF16),
            Bs=stack(b_t[:, sl]).astype(BF16), Ks=stack(k_t[:, sl]).astype(BF16),
            Vs=stack(v[:, sl]).astype(BF16),
            BhT=stack(b_h[:, sl]).T.astype(BF16), KhT=stack(k_h[:, sl]).T.astype(BF16)))
    return streams


def _wkv_solve(streams):
    C = WKV_CHUNK
    Q = 4 * HEAD

    def unstack(xs):
        return xs[0:C] + xs[C:2 * C] + xs[2 * C:3 * C] + xs[3 * C:4 * C]

    for s in streams:
        P = _dot_nt(jnp.concatenate([s['As'], s['Rs']], axis=0), jnp.concatenate([s['Bs'], s['Ks']], axis=0))
        A_ab = jnp.where(s['m_strict'], P[:Q, :Q], 0.0)
        s['A_ak'] = jnp.where(s['m_strict'], P[:Q, Q:], 0.0).astype(BF16)
        s['A_rb'] = jnp.where(s['m_incl'], P[Q:, :Q], 0.0).astype(BF16)
        s['A_rk'] = jnp.where(s['m_incl'], P[Q:, Q:], 0.0).astype(BF16)
        s['T'] = jnp.where(s['eye'], 1.0, A_ab)
        s['Ap'] = A_ab.astype(BF16)
    for _ in range(int(math.log2(C)) - 1):
        for s in streams:
            s['Ap'] = _dot(s['Ap'], s['Ap']).astype(BF16)
        for s in streams:
            s['T'] = s['T'] + _dot(s['T'].astype(BF16), s['Ap'])
    for s in streams:
        s['Z'] = _dot(s['A_ak'], s['Vs']).astype(BF16)
        s['y_k'] = _dot(s['A_rk'], s['Vs'])
        s['n_k'] = _dot(s['KhT'], s['Vs'])
    for s in streams:
        s['X'] = _dot(s['T'].astype(BF16), jnp.concatenate([s['As'], s['Z']], axis=1)).astype(BF16)
    for s in streams:
        W1 = _dot(s['A_rb'], s['X'])
        MN = _dot(s['BhT'], s['X'])
        s['r_p'] = s['r_t'] + unstack(W1[:, :Q])
        s['y_p'] = unstack(W1[:, Q:] + s['y_k'])
        s['M'] = MN[:, :Q]
        s['N'] = MN[:, Q:] + s['n_k']
    ys = []
    for s in streams:
        h0 = s['h_ref'][s['q']]
        seq = _dot(jnp.concatenate([s['r_p'], s['M']], axis=0).astype(BF16), h0.astype(BF16))
        ys.append(seq[:C] + s['y_p'])
        s['h_ref'][s['q']] = s['gam_col'] * h0 + seq[C:] + s['N']
    return ys


def _wkv_kernel(rf, kf, vf, af, bf, lf, rb, kb, vb, ab, bb, lb, yf_o, yb_o, hf_ref, hb_ref):
    @pl.when(pl.program_id(1) == 0)
    def _():
        hf_ref[...] = jnp.zeros_like(hf_ref)
        hb_ref[...] = jnp.zeros_like(hb_ref)

    streams = (_wkv_direction(rf[...], kf[...], vf[...], af[...], bf[...], lf[...], hf_ref, True)
               + _wkv_direction(rb[...], kb[...], vb[...], ab[...], bb[...], lb[...], hb_ref, False))
    ys = _wkv_solve(streams)
    yf_o[...] = jnp.concatenate(ys[0:2], axis=1)
    yb_o[...] = jnp.concatenate(ys[2:4], axis=1)


def _wkv(r, k, v, a, b, lf, lb, n_seq, T):
    C = WKV_CHUNK
    nc = T // C
    fw = lambda s, i: (s * nc + i, 0)
    bw = lambda s, i: (s * nc + nc - 1 - i, 0)
    spec_f = pl.BlockSpec((C, 512), fw)
    spec_b = pl.BlockSpec((C, 512), bw)
    shp = jax.ShapeDtypeStruct(r.shape, F32)
    return pl.pallas_call(
        _wkv_kernel,
        grid=(n_seq, nc),
        in_specs=[spec_f] * 6 + [spec_b] * 6,
        out_specs=[spec_f, spec_b],
        out_shape=[shp, shp],
        scratch_shapes=[pltpu.VMEM((2, 256, 256), F32), pltpu.VMEM((2, 256, 256), F32)],
        compiler_params=_params(("parallel", "arbitrary")),
        name="wkv",
    )(r, k, v, a, b, lf, r, k, v, a, b, lb)


def _attn_kernel(q_ref, k_ref, vt_ref, o_ref, s0_ref, s1_ref, p0_ref, p1_ref, acc_ref, *, tq, tk, T):
    q = q_ref[...].astype(F32)
    lo = lax.broadcasted_iota(jnp.int32, (tq, 128), 1) < HEAD
    qs = jnp.concatenate([
        jnp.where(lo, q[:, 0:128], 0.0), jnp.where(lo, 0.0, q[:, 0:128]),
        jnp.where(lo, q[:, 128:256], 0.0), jnp.where(lo, 0.0, q[:, 128:256])], axis=0)
    qst = qs.T.astype(BF16)
    R = 4 * tq
    n = T // tk

    def chunk(j):
        return pl.ds(j * tk if isinstance(j, int) else pl.multiple_of(j * tk, tk), tk)

    def scores(j):
        return _dot(k_ref[chunk(j), :], qst)

    def values(j, p):
        return _dot(vt_ref[0, :, chunk(j)], p)

    s_buf = (s0_ref, s1_ref)
    p_buf = (p0_ref, p1_ref)
    s_buf[0][...] = scores(0)
    p_buf[1][...] = jnp.zeros((tk, R), BF16)
    acc_ref[...] = jnp.zeros((V_ROWS, R), F32)

    def step(j, b, carry, ahead=True):
        m, alpha_prev = carry
        if ahead:
            s_buf[1 - b][...] = scores(j + 1)
        prev = max(j - 1, 0) if isinstance(j, int) else jnp.maximum(j - 1, 0)
        acc_ref[...] = alpha_prev * acc_ref[...] + values(prev, p_buf[1 - b][...])
        s = s_buf[b][...]
        m_new = jnp.maximum(m, jnp.max(s, axis=0, keepdims=True))
        p_buf[b][...] = jnp.exp2(s - m_new).astype(BF16)
        return m_new, jnp.exp2(m - m_new)

    def body(i, carry):
        return step(2 * i + 1, 1, step(2 * i, 0, carry))

    carry = lax.fori_loop(0, n // 2 - 1, body, (jnp.full((1, R), -1e30, F32), jnp.ones((1, R), F32)))
    carry = step(n - 2, 0, carry)
    _, alpha = step(n - 1, 1, carry, ahead=False)
    acc = alpha * acc_ref[...] + values(n - 1, p_buf[1][...])
    ot = acc[0:HEAD] / acc[HEAD:HEAD + 1]
    o01 = jnp.concatenate([ot[:, 0:tq], ot[:, tq:2 * tq]], axis=0).T
    o23 = jnp.concatenate([ot[:, 2 * tq:3 * tq], ot[:, 3 * tq:4 * tq]], axis=0).T
    o_ref[...] = jnp.concatenate([o01, o23], axis=1)


def _attention(q, k_rep, v_t, n_seq, T):
    tq = min(512, T)
    tk = min(512, T // 2)
    nq = T // tq
    assert (T // tk) % 2 == 0
    return pl.pallas_call(
        functools.partial(_attn_kernel, tq=tq, tk=tk, T=T),
        grid=(n_seq, N_KV, nq),
        in_specs=[
            pl.BlockSpec((tq, 256), lambda s, h, i: (s * nq + i, h)),
            pl.BlockSpec((T, 128), lambda s, h, i: (s, h)),
            pl.BlockSpec((1, V_ROWS, T), lambda s, h, i: (s, h, 0)),
        ],
        out_specs=pl.BlockSpec((tq, 256), lambda s, h, i: (s * nq + i, h)),
        out_shape=jax.ShapeDtypeStruct(q.shape, F32),
        scratch_shapes=[pltpu.VMEM((tk, 4 * tq), F32)] * 2 + [pltpu.VMEM((tk, 4 * tq), BF16)] * 2 + [
            pltpu.VMEM((V_ROWS, 4 * tq), F32)],
        compiler_params=_params(("parallel", "parallel", "arbitrary")),
        name="attention",
    )(q, k_rep, v_t)


def _post0_kernel(x_ref, yf_ref, yb_ref, g_ref, bon_ref, att_ref, ln_ref, wo_ref, ones_ref, o_ref):
    ones = ones_ref[...]
    y = yf_ref[...] + yb_ref[...]
    mean = _seg_sum(y, ones) * (1.0 / HEAD)
    d = y - mean
    var = _seg_sum(d * d, ones) * (1.0 / HEAD)
    yn = d * lax.rsqrt(var + GN_EPS) * ln_ref[0:1] + ln_ref[1:2]
    ya = ((yn + bon_ref[...]) * g_ref[...]).astype(BF16)
    mix = _dot(ya, wo_ref[0:512, :]) + _dot(att_ref[...].astype(BF16), wo_ref[512:1024, :])
    o_ref[...] = x_ref[...] + mix


def _post0(x2d, yf, yb, g, bon, att, ln, wo, ones_bd):
    N = x2d.shape[0]
    tm = min(512, N)
    tile = lambda i: (i, 0)
    const = lambda i: (0, 0)
    return pl.pallas_call(
        _post0_kernel,
        grid=(N // tm,),
        in_specs=[pl.BlockSpec((tm, D_MODEL), tile)] + [pl.BlockSpec((tm, 512), tile)] * 5 + [
            pl.BlockSpec((8, 512), const), pl.BlockSpec((D_MODEL, D_MODEL), const), pl.BlockSpec((512, 512), const)],
        out_specs=pl.BlockSpec((tm, D_MODEL), tile),
        out_shape=jax.ShapeDtypeStruct(x2d.shape, F32),
        compiler_params=_params(("parallel",)),
        name="post0",
    )(x2d, yf, yb, g, bon, att, ln, wo, ones_bd)


def _mlp_kernel(x_ref, gain_ref, wu_ref, wd_ref, o_ref, hn_ref):
    @pl.when(pl.program_id(1) == 0)
    def _():
        x = x_ref[...]
        hn_ref[...] = _rms(x, gain_ref[...]).astype(BF16)
        o_ref[...] = x

    u = _dot(hn_ref[...], wu_ref[...])
    u = jnp.maximum(u, 0.0)
    o_ref[...] += _dot((u * u).astype(BF16), wd_ref[...])


def _mlp(x2d, gain, w_up, w_down):
    N = x2d.shape[0]
    tm = min(512, N)
    fc = 1024
    return pl.pallas_call(
        _mlp_kernel,
        grid=(N // tm, D_FF // fc),
        in_specs=[
            pl.BlockSpec((tm, D_MODEL), lambda i, j: (i, 0)),
            pl.BlockSpec((1, D_MODEL), lambda i, j: (0, 0)),
            pl.BlockSpec((D_MODEL, fc), lambda i, j: (0, j)),
            pl.BlockSpec((fc, D_MODEL), lambda i, j: (j, 0)),
        ],
        out_specs=pl.BlockSpec((tm, D_MODEL), lambda i, j: (i, 0)),
        out_shape=jax.ShapeDtypeStruct(x2d.shape, F32),
        scratch_shapes=[pltpu.VMEM((tm, D_MODEL), BF16)],
        compiler_params=_params(("parallel", "arbitrary")),
        name="mlp",
    )(x2d, gain, w_up, w_down)


def _norm_kernel(x_ref, gain_ref, o_ref):
    o_ref[...] = _rms(x_ref[...], gain_ref[...]).astype(o_ref.dtype)


def _norm_f32(x2d, gain):
    N = x2d.shape[0]
    tm = min(1024, N)
    return pl.pallas_call(
        _norm_kernel,
        grid=(N // tm,),
        in_specs=[pl.BlockSpec((tm, D_MODEL), lambda i: (i, 0)), pl.BlockSpec((1, D_MODEL), lambda i: (0, 0))],
        out_specs=pl.BlockSpec((tm, D_MODEL), lambda i: (i, 0)),
        out_shape=jax.ShapeDtypeStruct(x2d.shape, F32),
        compiler_params=_params(("parallel",)),
        name="s5_norm",
    )(x2d, gain)


def _cmul_add(ar, ai, br, bi, cr, ci):
    return ar * br - ai * bi + cr, ar * bi + ai * br + ci


S5_SEGMENTS = 8


def _s5_scan(zs_ref, ps_ref, lam, pw_ref, fwd, nc):
    nl = zs_ref.shape[1]
    W = nl * 128
    ls = nc // S5_SEGMENTS
    ns = S5_SEGMENTS

    def gather(c, k):
        return jnp.concatenate([zs_ref[c, l, pl.ds(k, ns, stride=ls), :] for l in range(nl)], axis=1)

    def load(c, k):
        return jnp.concatenate([ps_ref[c, l, k * ns:(k + 1) * ns, :] for l in range(nl)], axis=1)

    def store(c, k, v):
        for l in range(nl):
            ps_ref[c, l, k * ns:(k + 1) * ns, :] = v[:, l * 128:(l + 1) * 128]

    xr = xi = jnp.zeros((ns, W), F32)
    for k in (range(ls) if fwd else range(ls - 1, -1, -1)):
        store(0, k, xr)
        store(1, k, xi)
        xr, xi = _cmul_add(lam[0:1], lam[1:2], xr, xi, gather(0, k), gather(1, k))
    cr = ci = jnp.zeros((1, W), F32)
    rows_r = [None] * S5_SEGMENTS
    rows_i = [None] * S5_SEGMENTS
    for s in (range(S5_SEGMENTS) if fwd else range(S5_SEGMENTS - 1, -1, -1)):
        rows_r[s] = cr
        rows_i[s] = ci
        cr, ci = _cmul_add(lam[2:3], lam[3:4], cr, ci, xr[s:s + 1], xi[s:s + 1])
    car_r = jnp.concatenate(rows_r, axis=0)
    car_i = jnp.concatenate(rows_i, axis=0)
    for k in range(ls):
        pr, pi = _cmul_add(pw_ref[0, k:k + 1, :], pw_ref[1, k:k + 1, :], car_r, car_i, load(0, k), load(1, k))
        store(0, k, pr)
        store(1, k, pi)


def _s5_kernel(x_ref, g2_ref, wz_ref, w2_ref, lam_ref, pw_ref, y_ref, lhs_ref, zs_ref, ps_ref, *, nc):
    C = S5_CHUNK
    W = S5_SLAB * S5_STATE
    nl = W // 128
    for j in range(C):
        lhs_ref[:, j * 128:(j + 1) * 128] = x_ref[pl.ds(j, nc, stride=C), :].astype(BF16)
    lhs = lhs_ref[...]
    for d in range(2):
        z = _dot(lhs, wz_ref[0, :, d * 2 * W:(d + 1) * 2 * W])
        for c in range(2):
            for l in range(nl):
                zs_ref[d, c, l] = z[:, c * W + l * 128:c * W + (l + 1) * 128]
        _s5_scan(zs_ref.at[d], ps_ref.at[d], lam_ref[0, d], pw_ref.at[0, d], d == 0, nc)
    ls = nc // S5_SEGMENTS

    def chunk_order(ref):
        return jnp.concatenate([ref[pl.ds(s, ls, stride=S5_SEGMENTS), :] for s in range(S5_SEGMENTS)], axis=0)

    pv = jnp.concatenate([chunk_order(ps_ref.at[d, c, l]) for d in range(2) for c in range(2) for l in range(nl)],
                         axis=1).astype(BF16)
    for i in range(0, C, 2):
        w_loc = g2_ref[0, (C - 1 - i) * 128:(2 * C - 1 - i) * 128, :]
        y2 = _dot(lhs, w_loc) + _dot(pv, w2_ref[0, :, i * 128:(i + 2) * 128])
        y_ref[pl.ds(i, nc, stride=C), :] = y2[:, :128]
        y_ref[pl.ds(i + 1, nc, stride=C), :] = y2[:, 128:]


def _s5_core(hn, g2, wz, w2, lam, pw, n_seq, T):
    nc = T // S5_CHUNK
    ls = nc // S5_SEGMENTS
    W = S5_SLAB * S5_STATE
    n_slab = D_MODEL // 128
    once = pl.Buffered(1)
    slab = lambda c, s: (c, 0, 0)
    return pl.pallas_call(
        functools.partial(_s5_kernel, nc=nc),
        grid=(n_slab, n_seq),
        in_specs=[
            pl.BlockSpec((T, 128), lambda c, s: (s, c)),
            pl.BlockSpec((1, 2 * S5_CHUNK * 128, 256), slab, pipeline_mode=once),
            pl.BlockSpec((1, S5_CHUNK * 128, 4 * W), slab, pipeline_mode=once),
            pl.BlockSpec((1, 4 * W, S5_CHUNK * 128), slab, pipeline_mode=once),
            pl.BlockSpec((1, 2, 8, W), lambda c, s: (c, 0, 0, 0)),
            pl.BlockSpec((1, 2, 2, ls, W), lambda c, s: (c, 0, 0, 0, 0)),
        ],
        out_specs=pl.BlockSpec((T, 128), lambda c, s: (s, c)),
        out_shape=jax.ShapeDtypeStruct(hn.shape, F32),
        scratch_shapes=[
            pltpu.VMEM((nc, S5_CHUNK * 128), BF16),
            pltpu.VMEM((2, 2, W // 128, nc, 128), F32), pltpu.VMEM((2, 2, W // 128, nc, 128), F32)],
        compiler_params=_params(("parallel", "arbitrary")),
        name="s5_core",
    )(hn, g2, wz, w2, lam, pw)


def _post1_kernel(x_ref, ys_ref, gain_ref, d_ref, wg_ref, bg_ref, o_ref):
    x = x_ref[...]
    y = _rms(x, gain_ref[...]) * d_ref[...] + ys_ref[...]
    z = 0.5 * y * (1.0 + jnp.tanh(math.sqrt(2.0 / math.pi) * (y + 0.044715 * (y * y * y))))
    gate = _sigmoid(_dot(z.astype(BF16), wg_ref[...]) + bg_ref[...])
    o_ref[...] = x + z * gate


def _post1(x2d, ys, gain, d, wg, bg):
    N = x2d.shape[0]
    tm = min(512, N)
    tile = lambda i: (i, 0)
    const = lambda i: (0, 0)
    return pl.pallas_call(
        _post1_kernel,
        grid=(N // tm,),
        in_specs=[pl.BlockSpec((tm, D_MODEL), tile), pl.BlockSpec((tm, D_MODEL), tile),
                  pl.BlockSpec((1, D_MODEL), const), pl.BlockSpec((1, D_MODEL), const),
                  pl.BlockSpec((D_MODEL, D_MODEL), const), pl.BlockSpec((1, D_MODEL), const)],
        out_specs=pl.BlockSpec((tm, D_MODEL), tile),
        out_shape=jax.ShapeDtypeStruct(x2d.shape, F32),
        compiler_params=_params(("parallel",)),
        name="post1",
    )(x2d, ys, gain, d, wg, bg)


def _s5_tables(p, T):
    C, G, P, SL = S5_CHUNK, S5_GROUPS, S5_STATE, S5_SLAB
    ns = G // SL
    ls = T // C // S5_SEGMENTS
    eye = jnp.eye(SL, dtype=F32)
    b_re = p['s5_b_re'][0].astype(F32)
    b_im = p['s5_b_im'][0].astype(F32)
    steps = jnp.arange(C, dtype=F32)

    def direction(sfx):
        lr = p['s5_lam_re_' + sfx][0].astype(F32)
        li = p['s5_lam_im_' + sfx][0].astype(F32)
        dt = jnp.exp(p['s5_log_dt_' + sfx][0].astype(F32))[:, None]

        def power(k):
            k = k[:, None, None]
            mag = jnp.exp(lr * dt * k)
            return mag * jnp.cos(li * dt * k), mag * jnp.sin(li * dt * k)

        l1r, l1i = power(jnp.ones((1,), F32))
        nr, ni = l1r[0] - 1.0, l1i[0]
        den = lr * lr + li * li
        cr = (nr * lr + ni * li) / den
        ci = (ni * lr - nr * li) / den
        cb_r = cr[:, :, None] * b_re - ci[:, :, None] * b_im
        cb_i = cr[:, :, None] * b_im + ci[:, :, None] * b_re
        c_r = p['s5_c_re_' + sfx][0].astype(F32)
        c_i = p['s5_c_im_' + sfx][0].astype(F32)
        return power, cb_r, cb_i, c_r, c_i

    def kernels(power, cb_r, cb_i, c_r, c_i):
        pr, pi = power(steps)
        d_r = pr[..., None] * cb_r - pi[..., None] * cb_i
        d_i = pr[..., None] * cb_i + pi[..., None] * cb_r
        k = jnp.einsum('gop,lgpi->lgio', c_r, d_r) - jnp.einsum('gop,lgpi->lgio', c_i, d_i)
        k = k.reshape(C, ns, SL, S5_GROUP, 1, S5_GROUP)
        return jnp.where(eye[:, None, :, None] > 0, k, 0.0).reshape(C, ns, 128, 128)

    def state_in(power, cb_r, cb_i, ks):
        pr, pi = power(ks)
        w_r = (pr[..., None] * cb_r - pi[..., None] * cb_i).reshape(C, ns, SL, P, S5_GROUP)
        w_i = (pr[..., None] * cb_i + pi[..., None] * cb_r).reshape(C, ns, SL, P, S5_GROUP)
        return [w_r, w_i]

    def state_out(power, c_r, c_i, ks):
        pr, pi = power(ks)
        e_r = (c_r[None] * pr[:, :, None, :] - c_i[None] * pi[:, :, None, :]).reshape(C, ns, SL, S5_GROUP, P)
        e_i = (c_r[None] * pi[:, :, None, :] + c_i[None] * pr[:, :, None, :]).reshape(C, ns, SL, S5_GROUP, P)
        return [e_r, -e_i]

    def scan_tables(power, ks_rows):
        lr_, li_ = power(C * jnp.array([1.0, ls], F32))
        lam = jnp.stack([lr_, li_], axis=1).reshape(4, ns, SL * P)
        lam = jnp.concatenate([lam, jnp.zeros_like(lam)], axis=0).transpose(1, 0, 2)
        pr, pi = power(C * ks_rows)
        pw = jnp.stack([pr.reshape(ls, ns, SL * P), pi.reshape(ls, ns, SL * P)], axis=0).transpose(2, 0, 1, 3)
        return lam, pw

    def expand(x, row_group, col_group):
        src = jnp.arange(2048) // (col_group * SL) * col_group + jnp.arange(2048) % col_group
        spread = (jnp.arange(256)[:, None] == src[None, :]).astype(BF16)
        full = jnp.einsum('srk,kc->src', x.astype(BF16), spread, preferred_element_type=F32)
        same = ((jnp.arange(2048) // row_group) % SL)[:, None] == ((jnp.arange(2048) // col_group) % SL)[None, :]
        return jnp.where(same, full, 0.0).astype(BF16)

    pf = direction('f')
    pb = direction('b')
    kf = kernels(*pf)
    kb = kernels(*pb)
    zero = jnp.zeros((1, ns, 128, 128), F32)
    gen = jnp.concatenate([kf[:0:-1], (kf[0] + kb[0])[None], kb[1:], zero], axis=0)
    gen_prev = jnp.concatenate([zero, gen[:-1]], axis=0)
    g2 = jnp.concatenate([gen, gen_prev], axis=-1).transpose(1, 0, 2, 3).reshape(ns, 2 * C * 128, 256)

    w4 = jnp.stack(state_in(pf[0], pf[1], pf[2], (C - 1) - steps) + state_in(pb[0], pb[1], pb[2], steps))
    wz = expand(w4.transpose(2, 1, 3, 5, 0, 4).reshape(ns, C * 128, 4 * P), S5_GROUP, P)
    e4 = jnp.stack(state_out(pf[0], pf[3], pf[4], steps + 1.0) + state_out(pb[0], pb[3], pb[4], C - steps))
    w2 = expand(e4.transpose(2, 0, 3, 5, 1, 4).reshape(ns, 4 * SL * P, C * S5_GROUP), P, S5_GROUP)
    pos = jnp.arange(ls, dtype=F32)
    lam_f, pw_f = scan_tables(pf[0], pos)
    lam_b, pw_b = scan_tables(pb[0], (ls - 1.0) - pos)
    lam = jnp.stack([lam_f, lam_b], axis=1)
    pw = jnp.stack([pw_f, pw_b], axis=1)
    return g2.astype(BF16), wz, w2, lam, pw


def _rope_tables(T):
    rows = T // GRID_W
    row_ids = jnp.repeat(jnp.arange(rows, dtype=F32), GRID_W)
    col_ids = jnp.tile(jnp.arange(GRID_W, dtype=F32), rows)
    pairs = HEAD // 4
    inv_freq = ROPE_THETA ** (-jnp.arange(pairs, dtype=F32) / pairs)
    ang = jnp.concatenate([row_ids[:, None] * inv_freq, col_ids[:, None] * inv_freq], axis=-1)
    cos = jnp.repeat(jnp.cos(ang), 2, axis=-1)
    sin = jnp.repeat(jnp.sin(ang), 2, axis=-1)
    sign = jnp.tile(jnp.array([-1.0, 1.0], F32), HEAD // 2)
    return jnp.tile(cos, (1, 2)), jnp.tile(sin * sign, (1, 2))


def _layer0_weights(p, T):
    w_in = p['hyb_w_in'][0]
    zc = jnp.zeros((D_MODEL, 64), F32)
    w_all = jnp.concatenate([w_in[:, 0:1728], zc, w_in[:, 1728:2624]], axis=1).astype(BF16)
    mu = p['hyb_shift_mu'][0]
    mu_all = jnp.concatenate([mu[0:1728], jnp.zeros((64,), F32), mu[1728:1856]]).reshape(1, RW_COLS)
    wup = jnp.zeros((384, 2048), F32)
    wup = wup.at[0:64, 0:512].set(p['rwkv_w_up_f'][0])
    wup = wup.at[64:128, 512:1024].set(p['rwkv_w_up_b'][0])
    wup = wup.at[128:192, 1024:1536].set(p['rwkv_a_up'][0])
    wup = wup.at[256:384, 1536:2048].set(p['rwkv_g_up'][0])
    wup = wup.astype(BF16)
    zr = jnp.zeros((512,), F32)
    par = jnp.stack([p['rwkv_w0_f'][0], p['rwkv_w0_b'][0], p['rwkv_a0'][0], p['rwkv_k_k'][0],
                     p['rwkv_k_a'][0], p['rwkv_r_k'][0].reshape(-1), zr, zr]).astype(F32)
    qg = jnp.tile(p['att_q_norm'][0], 8).reshape(1, 512).astype(F32)
    kg = jnp.tile(p['att_k_norm'][0], 2).reshape(1, 128).astype(F32)
    cos_t, sin_t = _rope_tables(T)
    seg = jnp.arange(512) // HEAD
    ones_bd = (seg[:, None] == seg[None, :]).astype(BF16)
    ln = jnp.stack([p['rwkv_lnx_g'][0], p['rwkv_lnx_b'][0]] + [zr] * 6).astype(F32)
    return dict(w_all=w_all, mu_all=mu_all, wup=wup, par=par, qg=qg, kg=kg, cos_t=cos_t, sin_t=sin_t,
                ones_bd=ones_bd, ln=ln, wo=p['hyb_w_out'][0].astype(BF16))


def _row(v):
    return v.reshape(1, -1).astype(F32)


def _mixer0(x2d, p, w, n_seq, T):
    r, k, v, a, b, lf, lb, g, bon, q, k_rep, v_t = _pre0(
        x2d, T, _row(p['mix_norm'][0]), w['w_all'], w['mu_all'], w['wup'], w['par'], w['qg'], w['kg'],
        w['cos_t'], w['sin_t'], w['ones_bd'])
    yf, yb = _wkv(r, k, v, a, b, lf, lb, n_seq, T)
    att = _attention(q, k_rep, v_t, n_seq, T)
    return _post0(x2d, yf, yb, g, bon, att, w['ln'], w['wo'], w['ones_bd'])


def _mixer1(x2d, p, s5, n_seq, T):
    g2, wz, w2, lam, pw = s5
    hn = _norm_f32(x2d, _row(p['mix_norm'][1]))
    ys = _s5_core(hn, g2, wz, w2, lam, pw, n_seq, T)
    return _post1(x2d, ys, _row(p['mix_norm'][1]), _row(p['s5_d'][0]),
                  p['s5_glu_w'][0].astype(BF16), _row(p['s5_glu_b'][0]))


def _ffn(x2d, p, ffn_w, layer):
    return _mlp(x2d, _row(p['ffn_norm'][layer]), ffn_w[layer][0], ffn_w[layer][1])


def _prepare(p, T):
    ffn_w = [(p['ffn_up'][l].astype(BF16), p['ffn_down'][l].astype(BF16)) for l in range(2)]
    return _layer0_weights(p, T), _s5_tables(p, T), ffn_w


def _trunk(x, p, prep=None):
    n_seq, T, _ = x.shape
    w0, s5, ffn_w = _prepare(p, T) if prep is None else prep
    x2d = x.reshape(n_seq * T, D_MODEL)
    x2d = _ffn(_mixer0(x2d, p, w0, n_seq, T), p, ffn_w, 0)
    x2d = _ffn(_mixer1(x2d, p, s5, n_seq, T), p, ffn_w, 1)
    return x2d.reshape(n_seq, T, D_MODEL)


def kernel(x_prompt, x_sample, mix_norm, ffn_norm, ffn_up, ffn_down, hyb_w_in, hyb_shift_mu, rwkv_w0_f, rwkv_w_up_f, rwkv_w0_b, rwkv_w_up_b, rwkv_a0, rwkv_a_up, rwkv_g_up, rwkv_k_k, rwkv_k_a, rwkv_r_k, rwkv_lnx_g, rwkv_lnx_b, att_q_norm, att_k_norm, hyb_w_out, s5_lam_re_f, s5_lam_im_f, s5_log_dt_f, s5_lam_re_b, s5_lam_im_b, s5_log_dt_b, s5_b_re, s5_b_im, s5_c_re_f, s5_c_im_f, s5_c_re_b, s5_c_im_b, s5_d, s5_glu_w, s5_glu_b):
    p = dict(mix_norm=mix_norm, ffn_norm=ffn_norm, ffn_up=ffn_up, ffn_down=ffn_down,
             hyb_w_in=hyb_w_in, hyb_shift_mu=hyb_shift_mu,
             rwkv_w0_f=rwkv_w0_f, rwkv_w_up_f=rwkv_w_up_f, rwkv_w0_b=rwkv_w0_b, rwkv_w_up_b=rwkv_w_up_b,
             rwkv_a0=rwkv_a0, rwkv_a_up=rwkv_a_up, rwkv_g_up=rwkv_g_up,
             rwkv_k_k=rwkv_k_k, rwkv_k_a=rwkv_k_a, rwkv_r_k=rwkv_r_k,
             rwkv_lnx_g=rwkv_lnx_g, rwkv_lnx_b=rwkv_lnx_b,
             att_q_norm=att_q_norm, att_k_norm=att_k_norm, hyb_w_out=hyb_w_out,
             s5_lam_re_f=s5_lam_re_f, s5_lam_im_f=s5_lam_im_f, s5_log_dt_f=s5_log_dt_f,
             s5_lam_re_b=s5_lam_re_b, s5_lam_im_b=s5_lam_im_b, s5_log_dt_b=s5_log_dt_b,
             s5_b_re=s5_b_re, s5_b_im=s5_b_im,
             s5_c_re_f=s5_c_re_f, s5_c_im_f=s5_c_im_f, s5_c_re_b=s5_c_re_b, s5_c_im_b=s5_c_im_b,
             s5_d=s5_d, s5_glu_w=s5_glu_w, s5_glu_b=s5_glu_b)
    assert x_prompt.shape[1] == x_sample.shape[1]
    prep = _prepare(p, x_prompt.shape[1])
    return (_trunk(x_prompt, p, prep), _trunk(x_sample, p, prep))
```

```python
import functools
import math

import jax
import jax.numpy as jnp
from jax import lax
from jax.experimental import pallas as pl
from jax.experimental.pallas import tpu as pltpu

F32 = jnp.float32
BF16 = jnp.bfloat16

D_MODEL = 1024
D_FF = 4 * D_MODEL
RMS_EPS = 1e-6
GRID_W = 64
RWKV_DIM = 512
HEAD = 64
GN_EPS = 64e-5
N_KV = 2
ROPE_THETA = 10000.0
S5_GROUP = 16
S5_GROUPS = D_MODEL // S5_GROUP
S5_STATE = 64
S5_SLAB = 128 // S5_GROUP

WKV_CHUNK = 64
WKV_STEP = 2
S5_CHUNK = 16
RW_COLS = 1920
ALL_COLS = 2688
EXP_M05 = math.exp(-0.5)
LOG2_E = math.log2(math.e)
V_ROWS = HEAD + 16
VMEM_LIMIT = 56 * 1024 * 1024


def _dot(a, b):
    return jnp.dot(a, b, preferred_element_type=F32)


def _dot_nt(a, b):
    return lax.dot_general(a, b, (((1,), (1,)), ((), ())), preferred_element_type=F32)


def _dot_tn(a, b):
    return lax.dot_general(a, b, (((0,), (0,)), ((), ())), preferred_element_type=F32)


def _split2(x):
    hi = x.astype(BF16)
    lo = (x - hi.astype(F32)).astype(BF16)
    return hi, lo


def _seg_sum(x, ones_bd):
    hi, lo = _split2(x)
    return _dot(hi, ones_bd) + _dot(lo, ones_bd)


def _rms(x, gain):
    return x * lax.rsqrt(jnp.mean(x * x, axis=-1, keepdims=True) + RMS_EPS) * gain


def _sigmoid(x):
    return 1.0 / (1.0 + jnp.exp(-x))


def _params(sem):
    return pltpu.CompilerParams(dimension_semantics=sem, vmem_limit_bytes=VMEM_LIMIT)


def _pre0_kernel(x_ref, xp_ref, xn_ref, gain_ref, w_ref, mu_ref, wup_ref, par_ref, qg_ref, kg_ref,
                 cos_ref, sin_ref, ones_ref,
                 r_o, k_o, v_o, a_o, b_o, lf_o, lb_o, g_o, bon_o, q_o, ka_o, va_o,
                 *, tiles_per_seq, tm):
    pos = pl.program_id(0) % tiles_per_seq
    gain = gain_ref[...]
    xp = jnp.where(pos == 0, 0.0, xp_ref[...])
    xn = jnp.where(pos == tiles_per_seq - 1, 0.0, xn_ref[...])
    x_all = jnp.concatenate([xp, x_ref[...], xn], axis=0)
    H_all = _dot(_rms(x_all, gain).astype(BF16), w_ref[...])
    H = H_all[8:8 + tm]
    Hr = H[:, :RW_COLS]
    Hr_all = H_all[:, :RW_COLS]
    prev = pltpu.roll(Hr_all, 1, 0)[8:8 + tm]
    nxt = pltpu.roll(Hr_all, tm + 15, 0)[8:8 + tm]
    Hs = Hr + mu_ref[...] * (0.5 * (prev + nxt) - Hr)

    ones = ones_ref[...]
    par = par_ref[...]
    r = Hs[:, 0:512]
    k = Hs[:, 512:1024]
    v = Hs[:, 1024:1536]
    act = jnp.concatenate(
        [jnp.tanh(Hs[:, 1536:1664]), Hs[:, 1664:1792], _sigmoid(Hs[:, 1792:1920])], axis=1).astype(BF16)
    up = _dot(act, wup_ref[...])
    lf_o[...] = -EXP_M05 * _sigmoid(par[0:1] + up[:, 0:512])
    lb_o[...] = -EXP_M05 * _sigmoid(par[1:2] + up[:, 512:1024])
    a_sig = _sigmoid(par[2:3] + up[:, 1024:1536])
    g_o[...] = up[:, 1536:2048]
    kk = k * par[3:4]
    kk = kk / jnp.maximum(jnp.sqrt(_seg_sum(kk * kk, ones)), 1e-12)
    k2 = k * (1.0 + (a_sig - 1.0) * par[4:5])
    r_o[...] = r
    k_o[...] = k2
    v_o[...] = v
    a_o[...] = -kk
    b_o[...] = kk * a_sig
    bon_o[...] = _seg_sum(r * k2 * par[5:6], ones) * v

    qa = H[:, RW_COLS:RW_COLS + 512]
    ka = H[:, RW_COLS + 512:RW_COLS + 640]
    va = H[:, RW_COLS + 640:RW_COLS + 768]
    cos = cos_ref[...]
    sin = sin_ref[...]

    def rope(x, c, s):
        n = x.shape[1]
        lane = lax.broadcasted_iota(jnp.int32, x.shape, 1)
        swapped = jnp.where(lane % 2 == 0, pltpu.roll(x, n - 1, 1), pltpu.roll(x, 1, 1))
        return x * c + swapped * s

    qn = qa * lax.rsqrt(_seg_sum(qa * qa, ones) * (1.0 / HEAD) + RMS_EPS) * qg_ref[...]
    qr = rope(qn, jnp.concatenate([cos] * 4, axis=1), jnp.concatenate([sin] * 4, axis=1))
    q_o[...] = (qr * (HEAD ** -0.5 * LOG2_E)).astype(BF16)
    kn = ka * lax.rsqrt(_seg_sum(ka * ka, ones[:128, :128]) * (1.0 / HEAD) + RMS_EPS) * kg_ref[...]
    kr = rope(kn, cos, sin)
    lt64 = lax.broadcasted_iota(jnp.int32, kr.shape, 1) < HEAD

    def rep(x):
        sw = pltpu.roll(x, HEAD, 1)
        return jnp.concatenate([jnp.where(lt64, x, sw), jnp.where(lt64, sw, x)], axis=1).astype(BF16)

    ka_o[...] = rep(kr)
    vt = va.T.astype(BF16)
    one = jnp.ones((V_ROWS - HEAD, tm), BF16)
    va_o[0] = jnp.concatenate([vt[0:HEAD], one, vt[HEAD:2 * HEAD], one], axis=0)


def _pre0(x2d, T, gain, w_all, mu_all, wup, par, qg, kg, cos_t, sin_t, ones_bd):
    N = x2d.shape[0]
    tm = min(256, T)
    tps = T // tm
    nt = N // tm
    t8 = tm // 8
    nb8 = N // 8
    const = lambda i: (0, 0)
    tile = lambda i: (i, 0)
    f512 = jax.ShapeDtypeStruct((N, 512), F32)
    out_shape = [f512] * 9 + [jax.ShapeDtypeStruct((N, 512), BF16), jax.ShapeDtypeStruct((N, 256), BF16),
                              jax.ShapeDtypeStruct((N // T, 2 * V_ROWS, T), BF16)]
    out_specs = [pl.BlockSpec((tm, 512), tile)] * 10 + [
        pl.BlockSpec((tm, 256), tile), pl.BlockSpec((1, 2 * V_ROWS, tm), lambda i: (i // tps, 0, i % tps))]
    return pl.pallas_call(
        functools.partial(_pre0_kernel, tiles_per_seq=tps, tm=tm),
        grid=(nt,),
        in_specs=[
            pl.BlockSpec((tm, D_MODEL), tile),
            pl.BlockSpec((8, D_MODEL), lambda i: (jnp.maximum(i * t8 - 1, 0), 0)),
            pl.BlockSpec((8, D_MODEL), lambda i: (jnp.minimum((i + 1) * t8, nb8 - 1), 0)),
            pl.BlockSpec((1, D_MODEL), const),
            pl.BlockSpec((D_MODEL, ALL_COLS), const),
            pl.BlockSpec((1, RW_COLS), const),
            pl.BlockSpec((384, 2048), const),
            pl.BlockSpec((8, 512), const),
            pl.BlockSpec((1, 512), const),
            pl.BlockSpec((1, 128), const),
            pl.BlockSpec((tm, 128), lambda i: (i % tps, 0)),
            pl.BlockSpec((tm, 128), lambda i: (i % tps, 0)),
            pl.BlockSpec((512, 512), const),
        ],
        out_specs=out_specs,
        out_shape=out_shape,
        compiler_params=_params(("parallel",)),
        name="pre0",
    )(x2d, x2d, x2d, gain, w_all, mu_all, wup, par, qg, kg, cos_t, sin_t, ones_bd)


def _wkv_direction(r, k, v, a, b, L, h_ref, fwd):
    C = WKV_CHUNK
    Q = 4 * HEAD
    ti = lax.broadcasted_iota(jnp.int32, (C, C), 0)
    si = lax.broadcasted_iota(jnp.int32, (C, C), 1)
    tri = jnp.where(si <= ti, 1.0, 0.0).astype(BF16)
    l1 = L.astype(BF16)
    rem = L - l1.astype(F32)
    l2 = rem.astype(BF16)
    l3 = (rem - l2.astype(F32)).astype(BF16)
    cs = _dot(tri, l1) + _dot(tri, l2) + _dot(tri, l3)
    total = cs[C - 1:C, :]
    if fwd:
        cs_incl = cs
        cs_excl = cs - L
    else:
        cs_incl = total - (cs - L)
        cs_excl = total - cs
    e_incl = jnp.exp(cs_incl)
    e_inv = jnp.exp(-cs_incl)
    e_rem = jnp.exp(total - cs_incl)
    a_t = a * jnp.exp(cs_excl)
    r_t = r * e_incl
    b_t = b * e_inv
    k_t = k * e_inv
    b_h = b * e_rem
    k_h = k * e_rem
    gam = jnp.exp(total)

    streams = []
    for q in range(RWKV_DIM // Q):
        sl = slice(q * Q, (q + 1) * Q)
        streams.append(dict(
            q=q, h_ref=h_ref, fwd=fwd, gam=gam[:, sl], a_t=a_t[:, sl], r_t=r_t[:, sl], b_t=b_t[:, sl],
            k_t=k_t[:, sl], v=v[:, sl], b_h=b_h[:, sl], k_h=k_h[:, sl]))
    return streams


def _wkv_solve(streams):
    C = WKV_CHUNK
    Q = 4 * HEAD
    lane_q = lax.broadcasted_iota(jnp.int32, (C, Q), 1) // HEAD
    ri = lax.broadcasted_iota(jnp.int32, (Q, Q), 0)
    ci = lax.broadcasted_iota(jnp.int32, (Q, Q), 1)
    same = (ri // C) == (ci // C)
    tw = lax.broadcasted_iota(jnp.int32, (C, Q), 0)
    sw = lax.broadcasted_iota(jnp.int32, (C, Q), 1) % C
    masks = {True: (sw < tw, sw <= tw), False: (sw > tw, sw >= tw)}
    eye_w = sw == tw

    def stack(xq):
        return jnp.concatenate([jnp.where(lane_q == h, xq, 0.0) for h in range(4)], axis=0).astype(BF16)

    def blockdiag(w):
        return jnp.where(same, jnp.concatenate([w] * 4, axis=0), 0.0).astype(BF16)

    for s in streams:
        strict, incl = masks[s['fwd']]
        P = _dot_nt(jnp.concatenate([s['a_t'], s['r_t']], axis=0).astype(BF16),
                    jnp.concatenate([stack(s['b_t']), stack(s['k_t'])], axis=0))
        A_ab = jnp.where(strict, P[:C, :Q], 0.0)
        s['A_k'] = jnp.concatenate([jnp.where(strict, P[:C, Q:], 0.0), jnp.where(incl, P[C:, Q:], 0.0)],
                                   axis=0).astype(BF16)
        s['A_rb'] = jnp.where(incl, P[C:, :Q], 0.0).astype(BF16)
        s['T'] = jnp.where(eye_w, 1.0, A_ab)
        s['Ap'] = A_ab
        s['Ap_bd'] = blockdiag(A_ab)
        s['Vs'] = stack(s['v'])
    for _ in range(int(math.log2(C)) - 1):
        for s in streams:
            s['Ap'] = _dot(s['Ap'].astype(BF16), s['Ap_bd'])
            s['Ap_bd'] = blockdiag(s['Ap'])
        for s in streams:
            s['T'] = s['T'] + _dot(s['T'].astype(BF16), s['Ap_bd'])
    for s in streams:
        zy = _dot(s['A_k'], s['Vs'])
        s['Z'] = zy[:C]
        s['y_k'] = zy[C:]
        s['n_k'] = _dot_tn(s['k_h'].astype(BF16), s['v'].astype(BF16))
    for s in streams:
        s['X'] = _dot(s['T'].astype(BF16), jnp.concatenate([stack(s['a_t']), stack(s['Z'])], axis=1))
    for s in streams:
        X = s['X']
        W1 = _dot(s['A_rb'], jnp.concatenate([stack(X[:, :Q]), stack(X[:, Q:])], axis=1))
        MN = _dot_tn(s['b_h'].astype(BF16), X.astype(BF16))
        s['r_p'] = s['r_t'] + W1[:, :Q]
        s['y_p'] = W1[:, Q:] + s['y_k']
        s['M'] = jnp.where(same, MN[:, :Q], 0.0)
        s['N'] = jnp.where(same, MN[:, Q:] + s['n_k'], 0.0)
    ys = []
    for s in streams:
        h0 = s['h_ref'][s['q']]
        gam_col = jnp.sum(jnp.where(ri == ci, s['gam'], 0.0), axis=1, keepdims=True)
        seq = _dot(jnp.concatenate([s['r_p'], s['M']], axis=0).astype(BF16), h0.astype(BF16))
        ys.append(seq[:C] + s['y_p'])
        s['h_ref'][s['q']] = gam_col * h0 + seq[C:] + s['N']
    return ys


def _wkv_kernel(rf, kf, vf, af, bf, lf, rb, kb, vb, ab, bb, lb, yf_o, yb_o, hf_ref, hb_ref):
    @pl.when(pl.program_id(1) == 0)
    def _():
        hf_ref[...] = jnp.zeros_like(hf_ref)
        hb_ref[...] = jnp.zeros_like(hb_ref)

    C = WKV_CHUNK
    streams = []
    for refs, h_ref, fwd in (((rf, kf, vf, af, bf, lf), hf_ref, True), ((rb, kb, vb, ab, bb, lb), hb_ref, False)):
        for n in (range(WKV_STEP) if fwd else range(WKV_STEP - 1, -1, -1)):
            streams += _wkv_direction(*(x[n * C:(n + 1) * C, :] for x in refs), h_ref, fwd)
    ys = _wkv_solve(streams)
    for n in range(WKV_STEP):
        f0 = 2 * n
        b0 = 2 * WKV_STEP + 2 * (WKV_STEP - 1 - n)
        yf_o[n * C:(n + 1) * C, :] = jnp.concatenate(ys[f0:f0 + 2], axis=1)
        yb_o[n * C:(n + 1) * C, :] = jnp.concatenate(ys[b0:b0 + 2], axis=1)


def _wkv(r, k, v, a, b, lf, lb, n_seq, T):
    C = WKV_CHUNK * WKV_STEP
    nc = T // C
    fw = lambda s, i: (s * nc + i, 0)
    bw = lambda s, i: (s * nc + nc - 1 - i, 0)
    spec_f = pl.BlockSpec((C, 512), fw)
    spec_b = pl.BlockSpec((C, 512), bw)
    shp = jax.ShapeDtypeStruct(r.shape, F32)
    return pl.pallas_call(
        _wkv_kernel,
        grid=(n_seq, nc),
        in_specs=[spec_f] * 6 + [spec_b] * 6,
        out_specs=[spec_f, spec_b],
        out_shape=[shp, shp],
        scratch_shapes=[pltpu.VMEM((2, 256, 256), F32), pltpu.VMEM((2, 256, 256), F32)],
        compiler_params=_params(("parallel", "arbitrary")),
        name="wkv",
    )(r, k, v, a, b, lf, r, k, v, a, b, lb)


def _attn_kernel(q_ref, k_ref, vt_ref, o_ref, s0_ref, s1_ref, p0_ref, p1_ref, acc_ref, *, tq, tk, T):
    q = q_ref[...].astype(F32)
    lo = lax.broadcasted_iota(jnp.int32, (tq, 128), 1) < HEAD
    qs = jnp.concatenate([
        jnp.where(lo, q[:, 0:128], 0.0), jnp.where(lo, 0.0, q[:, 0:128]),
        jnp.where(lo, q[:, 128:256], 0.0), jnp.where(lo, 0.0, q[:, 128:256])], axis=0)
    qst = qs.T.astype(BF16)
    R = 4 * tq
    n = T // tk

    def chunk(j):
        return pl.ds(j * tk if isinstance(j, int) else pl.multiple_of(j * tk, tk), tk)

    def scores(j):
        return _dot(k_ref[chunk(j), :], qst)

    def values(j, p):
        return _dot(vt_ref[0, :, chunk(j)], p)

    s_buf = (s0_ref, s1_ref)
    p_buf = (p0_ref, p1_ref)
    s_buf[0][...] = scores(0)
    p_buf[1][...] = jnp.zeros((tk, R), BF16)
    acc_ref[...] = jnp.zeros((V_ROWS, R), F32)

    def step(j, b, carry, ahead=True):
        m, alpha_prev = carry
        if ahead:
            s_buf[1 - b][...] = scores(j + 1)
        prev = max(j - 1, 0) if isinstance(j, int) else jnp.maximum(j - 1, 0)
        acc_ref[...] = alpha_prev * acc_ref[...] + values(prev, p_buf[1 - b][...])
        s = s_buf[b][...]
        m_new = jnp.maximum(m, jnp.max(s, axis=0, keepdims=True))
        p_buf[b][...] = jnp.exp2(s - m_new).astype(BF16)
        return m_new, jnp.exp2(m - m_new)

    def body(i, carry):
        return step(2 * i + 1, 1, step(2 * i, 0, carry))

    carry = lax.fori_loop(0, n // 2 - 1, body, (jnp.full((1, R), -1e30, F32), jnp.ones((1, R), F32)))
    carry = step(n - 2, 0, carry)
    _, alpha = step(n - 1, 1, carry, ahead=False)
    acc = alpha * acc_ref[...] + values(n - 1, p_buf[1][...])
    ot = acc[0:HEAD] / acc[HEAD:HEAD + 1]
    o01 = jnp.concatenate([ot[:, 0:tq], ot[:, tq:2 * tq]], axis=0).T
    o23 = jnp.concatenate([ot[:, 2 * tq:3 * tq], ot[:, 3 * tq:4 * tq]], axis=0).T
    o_ref[...] = jnp.concatenate([o01, o23], axis=1)


def _attention(q, k_rep, v_t, n_seq, T):
    tq = min(1024, T)
    tk = min(512, T // 2)
    nq = T // tq
    assert (T // tk) % 2 == 0
    return pl.pallas_call(
        functools.partial(_attn_kernel, tq=tq, tk=tk, T=T),
        grid=(n_seq, N_KV, nq),
        in_specs=[
            pl.BlockSpec((tq, 256), lambda s, h, i: (s * nq + i, h)),
            pl.BlockSpec((T, 128), lambda s, h, i: (s, h)),
            pl.BlockSpec((1, V_ROWS, T), lambda s, h, i: (s, h, 0)),
        ],
        out_specs=pl.BlockSpec((tq, 256), lambda s, h, i: (s * nq + i, h)),
        out_shape=jax.ShapeDtypeStruct(q.shape, F32),
        scratch_shapes=[pltpu.VMEM((tk, 4 * tq), F32)] * 2 + [pltpu.VMEM((tk, 4 * tq), BF16)] * 2 + [
            pltpu.VMEM((V_ROWS, 4 * tq), F32)],
        compiler_params=_params(("parallel", "parallel", "arbitrary")),
        name="attention",
    )(q, k_rep, v_t)


def _post0_kernel(x_ref, yf_ref, yb_ref, g_ref, bon_ref, att_ref, ln_ref, wo_ref, ones_ref, o_ref):
    ones = ones_ref[...]
    y = yf_ref[...] + yb_ref[...]
    mean = _seg_sum(y, ones) * (1.0 / HEAD)
    d = y - mean
    var = _seg_sum(d * d, ones) * (1.0 / HEAD)
    yn = d * lax.rsqrt(var + GN_EPS) * ln_ref[0:1] + ln_ref[1:2]
    ya = ((yn + bon_ref[...]) * g_ref[...]).astype(BF16)
    mix = _dot(ya, wo_ref[0:512, :]) + _dot(att_ref[...].astype(BF16), wo_ref[512:1024, :])
    o_ref[...] = x_ref[...] + mix


def _post0(x2d, yf, yb, g, bon, att, ln, wo, ones_bd):
    N = x2d.shape[0]
    tm = min(512, N)
    tile = lambda i: (i, 0)
    const = lambda i: (0, 0)
    return pl.pallas_call(
        _post0_kernel,
        grid=(N // tm,),
        in_specs=[pl.BlockSpec((tm, D_MODEL), tile)] + [pl.BlockSpec((tm, 512), tile)] * 5 + [
            pl.BlockSpec((8, 512), const), pl.BlockSpec((D_MODEL, D_MODEL), const), pl.BlockSpec((512, 512), const)],
        out_specs=pl.BlockSpec((tm, D_MODEL), tile),
        out_shape=jax.ShapeDtypeStruct(x2d.shape, F32),
        compiler_params=_params(("parallel",)),
        name="post0",
    )(x2d, yf, yb, g, bon, att, ln, wo, ones_bd)


def _mlp_kernel(x_ref, gain_ref, wu_ref, wd_ref, o_ref, hn_ref):
    @pl.when(pl.program_id(1) == 0)
    def _():
        x = x_ref[...]
        hn_ref[...] = _rms(x, gain_ref[...]).astype(BF16)
        o_ref[...] = x

    u = _dot(hn_ref[...], wu_ref[...])
    u = jnp.maximum(u, 0.0)
    o_ref[...] += _dot((u * u).astype(BF16), wd_ref[...])


def _mlp(x2d, gain, w_up, w_down):
    N = x2d.shape[0]
    tm = min(512, N)
    fc = 1024
    return pl.pallas_call(
        _mlp_kernel,
        grid=(N // tm, D_FF // fc),
        in_specs=[
            pl.BlockSpec((tm, D_MODEL), lambda i, j: (i, 0)),
            pl.BlockSpec((1, D_MODEL), lambda i, j: (0, 0)),
            pl.BlockSpec((D_MODEL, fc), lambda i, j: (0, j)),
            pl.BlockSpec((fc, D_MODEL), lambda i, j: (j, 0)),
        ],
        out_specs=pl.BlockSpec((tm, D_MODEL), lambda i, j: (i, 0)),
        out_shape=jax.ShapeDtypeStruct(x2d.shape, F32),
        scratch_shapes=[pltpu.VMEM((tm, D_MODEL), BF16)],
        compiler_params=_params(("parallel", "arbitrary")),
        name="mlp",
    )(x2d, gain, w_up, w_down)


def _norm_kernel(x_ref, gain_ref, o_ref):
    o_ref[...] = _rms(x_ref[...], gain_ref[...]).astype(o_ref.dtype)


def _norm_f32(x2d, gain):
    N = x2d.shape[0]
    tm = min(1024, N)
    return pl.pallas_call(
        _norm_kernel,
        grid=(N // tm,),
        in_specs=[pl.BlockSpec((tm, D_MODEL), lambda i: (i, 0)), pl.BlockSpec((1, D_MODEL), lambda i: (0, 0))],
        out_specs=pl.BlockSpec((tm, D_MODEL), lambda i: (i, 0)),
        out_shape=jax.ShapeDtypeStruct(x2d.shape, F32),
        compiler_params=_params(("parallel",)),
        name="s5_norm",
    )(x2d, gain)


def _cmul_add(ar, ai, br, bi, cr, ci):
    return ar * br - ai * bi + cr, ar * bi + ai * br + ci


S5_SEGMENTS = 8


def _s5_scan(zs_ref, ps_ref, lam, pw_ref, fwd, nc):
    nl = zs_ref.shape[1]
    W = nl * 128
    ls = nc // S5_SEGMENTS
    ns = S5_SEGMENTS

    def gather(c, k):
        return jnp.concatenate([zs_ref[c, l, pl.ds(k, ns, stride=ls), :] for l in range(nl)], axis=1)

    def load(c, k):
        return jnp.concatenate([ps_ref[c, l, k * ns:(k + 1) * ns, :] for l in range(nl)], axis=1)

    def store(c, k, v):
        for l in range(nl):
            ps_ref[c, l, k * ns:(k + 1) * ns, :] = v[:, l * 128:(l + 1) * 128]

    xr = xi = jnp.zeros((ns, W), F32)
    for k in (range(ls) if fwd else range(ls - 1, -1, -1)):
        store(0, k, xr)
        store(1, k, xi)
        xr, xi = _cmul_add(lam[0:1], lam[1:2], xr, xi, gather(0, k), gather(1, k))
    cr = ci = jnp.zeros((1, W), F32)
    rows_r = [None] * S5_SEGMENTS
    rows_i = [None] * S5_SEGMENTS
    for s in (range(S5_SEGMENTS) if fwd else range(S5_SEGMENTS - 1, -1, -1)):
        rows_r[s] = cr
        rows_i[s] = ci
        cr, ci = _cmul_add(lam[2:3], lam[3:4], cr, ci, xr[s:s + 1], xi[s:s + 1])
    car_r = jnp.concatenate(rows_r, axis=0)
    car_i = jnp.concatenate(rows_i, axis=0)
    for k in range(ls):
        pr, pi = _cmul_add(pw_ref[0, k:k + 1, :], pw_ref[1, k:k + 1, :], car_r, car_i, load(0, k), load(1, k))
        store(0, k, pr)
        store(1, k, pi)


def _s5_kernel(x_ref, g2_ref, wz_ref, w2_ref, lam_ref, pw_ref, y_ref, lhs_ref, zs_ref, ps_ref, *, nc):
    C = S5_CHUNK
    W = S5_SLAB * S5_STATE
    nl = W // 128
    for j in range(C):
        lhs_ref[:, j * 128:(j + 1) * 128] = x_ref[pl.ds(j, nc, stride=C), :].astype(BF16)
    lhs = lhs_ref[...]
    for d in range(2):
        z = _dot(lhs, wz_ref[0, :, d * 2 * W:(d + 1) * 2 * W])
        for c in range(2):
            for l in range(nl):
                zs_ref[d, c, l] = z[:, c * W + l * 128:c * W + (l + 1) * 128]
        _s5_scan(zs_ref.at[d], ps_ref.at[d], lam_ref[0, d], pw_ref.at[0, d], d == 0, nc)
    ls = nc // S5_SEGMENTS

    def chunk_order(ref):
        return jnp.concatenate([ref[pl.ds(s, ls, stride=S5_SEGMENTS), :] for s in range(S5_SEGMENTS)], axis=0)

    pv = jnp.concatenate([chunk_order(ps_ref.at[d, c, l]) for d in range(2) for c in range(2) for l in range(nl)],
                         axis=1).astype(BF16)
    for i in range(0, C, 2):
        w_loc = g2_ref[0, (C - 1 - i) * 128:(2 * C - 1 - i) * 128, :]
        y2 = _dot(lhs, w_loc) + _dot(pv, w2_ref[0, :, i * 128:(i + 2) * 128])
        y_ref[pl.ds(i, nc, stride=C), :] = y2[:, :128]
        y_ref[pl.ds(i + 1, nc, stride=C), :] = y2[:, 128:]


def _s5_core(hn, g2, wz, w2, lam, pw, n_seq, T):
    nc = T // S5_CHUNK
    ls = nc // S5_SEGMENTS
    W = S5_SLAB * S5_STATE
    n_slab = D_MODEL // 128
    once = pl.Buffered(1)
    slab = lambda c, s: (c, 0, 0)
    return pl.pallas_call(
        functools.partial(_s5_kernel, nc=nc),
        grid=(n_slab, n_seq),
        in_specs=[
            pl.BlockSpec((T, 128), lambda c, s: (s, c)),
            pl.BlockSpec((1, 2 * S5_CHUNK * 128, 256), slab, pipeline_mode=once),
            pl.BlockSpec((1, S5_CHUNK * 128, 4 * W), slab, pipeline_mode=once),
            pl.BlockSpec((1, 4 * W, S5_CHUNK * 128), slab, pipeline_mode=once),
            pl.BlockSpec((1, 2, 8, W), lambda c, s: (c, 0, 0, 0)),
            pl.BlockSpec((1, 2, 2, ls, W), lambda c, s: (c, 0, 0, 0, 0)),
        ],
        out_specs=pl.BlockSpec((T, 128), lambda c, s: (s, c)),
        out_shape=jax.ShapeDtypeStruct(hn.shape, F32),
        scratch_shapes=[
            pltpu.VMEM((nc, S5_CHUNK * 128), BF16),
            pltpu.VMEM((2, 2, W // 128, nc, 128), F32), pltpu.VMEM((2, 2, W // 128, nc, 128), F32)],
        compiler_params=_params(("parallel", "arbitrary")),
        name="s5_core",
    )(hn, g2, wz, w2, lam, pw)


def _post1_kernel(x_ref, ys_ref, gain_ref, d_ref, wg_ref, bg_ref, o_ref):
    x = x_ref[...]
    y = _rms(x, gain_ref[...]) * d_ref[...] + ys_ref[...]
    z = 0.5 * y * (1.0 + jnp.tanh(math.sqrt(2.0 / math.pi) * (y + 0.044715 * (y * y * y))))
    gate = _sigmoid(_dot(z.astype(BF16), wg_ref[...]) + bg_ref[...])
    o_ref[...] = x + z * gate


def _post1(x2d, ys, gain, d, wg, bg):
    N = x2d.shape[0]
    tm = min(512, N)
    tile = lambda i: (i, 0)
    const = lambda i: (0, 0)
    return pl.pallas_call(
        _post1_kernel,
        grid=(N // tm,),
        in_specs=[pl.BlockSpec((tm, D_MODEL), tile), pl.BlockSpec((tm, D_MODEL), tile),
                  pl.BlockSpec((1, D_MODEL), const), pl.BlockSpec((1, D_MODEL), const),
                  pl.BlockSpec((D_MODEL, D_MODEL), const), pl.BlockSpec((1, D_MODEL), const)],
        out_specs=pl.BlockSpec((tm, D_MODEL), tile),
        out_shape=jax.ShapeDtypeStruct(x2d.shape, F32),
        compiler_params=_params(("parallel",)),
        name="post1",
    )(x2d, ys, gain, d, wg, bg)


def _s5_tables(p, T):
    C, G, P, SL = S5_CHUNK, S5_GROUPS, S5_STATE, S5_SLAB
    ns = G // SL
    ls = T // C // S5_SEGMENTS
    eye = jnp.eye(SL, dtype=F32)
    b_re = p['s5_b_re'][0].astype(F32)
    b_im = p['s5_b_im'][0].astype(F32)
    steps = jnp.arange(C, dtype=F32)

    def direction(sfx):
        lr = p['s5_lam_re_' + sfx][0].astype(F32)
        li = p['s5_lam_im_' + sfx][0].astype(F32)
        dt = jnp.exp(p['s5_log_dt_' + sfx][0].astype(F32))[:, None]

        def power(k):
            k = k[:, None, None]
            mag = jnp.exp(lr * dt * k)
            return mag * jnp.cos(li * dt * k), mag * jnp.sin(li * dt * k)

        l1r, l1i = power(jnp.ones((1,), F32))
        nr, ni = l1r[0] - 1.0, l1i[0]
        den = lr * lr + li * li
        cr = (nr * lr + ni * li) / den
        ci = (ni * lr - nr * li) / den
        cb_r = cr[:, :, None] * b_re - ci[:, :, None] * b_im
        cb_i = cr[:, :, None] * b_im + ci[:, :, None] * b_re
        c_r = p['s5_c_re_' + sfx][0].astype(F32)
        c_i = p['s5_c_im_' + sfx][0].astype(F32)
        return power, cb_r, cb_i, c_r, c_i

    def kernels(power, cb_r, cb_i, c_r, c_i):
        pr, pi = power(steps)
        d_r = pr[..., None] * cb_r - pi[..., None] * cb_i
        d_i = pr[..., None] * cb_i + pi[..., None] * cb_r
        k = jnp.einsum('gop,lgpi->lgio', c_r, d_r) - jnp.einsum('gop,lgpi->lgio', c_i, d_i)
        k = k.reshape(C, ns, SL, S5_GROUP, 1, S5_GROUP)
        return jnp.where(eye[:, None, :, None] > 0, k, 0.0).reshape(C, ns, 128, 128)

    def state_in(power, cb_r, cb_i, ks):
        pr, pi = power(ks)
        w_r = (pr[..., None] * cb_r - pi[..., None] * cb_i).reshape(C, ns, SL, P, S5_GROUP)
        w_i = (pr[..., None] * cb_i + pi[..., None] * cb_r).reshape(C, ns, SL, P, S5_GROUP)
        return [w_r, w_i]

    def state_out(power, c_r, c_i, ks):
        pr, pi = power(ks)
        e_r = (c_r[None] * pr[:, :, None, :] - c_i[None] * pi[:, :, None, :]).reshape(C, ns, SL, S5_GROUP, P)
        e_i = (c_r[None] * pi[:, :, None, :] + c_i[None] * pr[:, :, None, :]).reshape(C, ns, SL, S5_GROUP, P)
        return [e_r, -e_i]

    def scan_tables(power, ks_rows):
        lr_, li_ = power(C * jnp.array([1.0, ls], F32))
        lam = jnp.stack([lr_, li_], axis=1).reshape(4, ns, SL * P)
        lam = jnp.concatenate([lam, jnp.zeros_like(lam)], axis=0).transpose(1, 0, 2)
        pr, pi = power(C * ks_rows)
        pw = jnp.stack([pr.reshape(ls, ns, SL * P), pi.reshape(ls, ns, SL * P)], axis=0).transpose(2, 0, 1, 3)
        return lam, pw

    def expand(x, row_group, col_group):
        src = jnp.arange(2048) // (col_group * SL) * col_group + jnp.arange(2048) % col_group
        spread = (jnp.arange(256)[:, None] == src[None, :]).astype(BF16)
        full = jnp.einsum('srk,kc->src', x.astype(BF16), spread, preferred_element_type=F32)
        same = ((jnp.arange(2048) // row_group) % SL)[:, None] == ((jnp.arange(2048) // col_group) % SL)[None, :]
        return jnp.where(same, full, 0.0).astype(BF16)

    pf = direction('f')
    pb = direction('b')
    kf = kernels(*pf)
    kb = kernels(*pb)
    zero = jnp.zeros((1, ns, 128, 128), F32)
    gen = jnp.concatenate([kf[:0:-1], (kf[0] + kb[0])[None], kb[1:], zero], axis=0)
    gen_prev = jnp.concatenate([zero, gen[:-1]], axis=0)
    g2 = jnp.concatenate([gen, gen_prev], axis=-1).transpose(1, 0, 2, 3).reshape(ns, 2 * C * 128, 256)

    w4 = jnp.stack(state_in(pf[0], pf[1], pf[2], (C - 1) - steps) + state_in(pb[0], pb[1], pb[2], steps))
    wz = expand(w4.transpose(2, 1, 3, 5, 0, 4).reshape(ns, C * 128, 4 * P), S5_GROUP, P)
    e4 = jnp.stack(state_out(pf[0], pf[3], pf[4], steps + 1.0) + state_out(pb[0], pb[3], pb[4], C - steps))
    w2 = expand(e4.transpose(2, 0, 3, 5, 1, 4).reshape(ns, 4 * SL * P, C * S5_GROUP), P, S5_GROUP)
    pos = jnp.arange(ls, dtype=F32)
    lam_f, pw_f = scan_tables(pf[0], pos)
    lam_b, pw_b = scan_tables(pb[0], (ls - 1.0) - pos)
    lam = jnp.stack([lam_f, lam_b], axis=1)
    pw = jnp.stack([pw_f, pw_b], axis=1)
    return g2.astype(BF16), wz, w2, lam, pw


def _rope_tables(T):
    rows = T // GRID_W
    row_ids = jnp.repeat(jnp.arange(rows, dtype=F32), GRID_W)
    col_ids = jnp.tile(jnp.arange(GRID_W, dtype=F32), rows)
    pairs = HEAD // 4
    inv_freq = ROPE_THETA ** (-jnp.arange(pairs, dtype=F32) / pairs)
    ang = jnp.concatenate([row_ids[:, None] * inv_freq, col_ids[:, None] * inv_freq], axis=-1)
    cos = jnp.repeat(jnp.cos(ang), 2, axis=-1)
    sin = jnp.repeat(jnp.sin(ang), 2, axis=-1)
    sign = jnp.tile(jnp.array([-1.0, 1.0], F32), HEAD // 2)
    return jnp.tile(cos, (1, 2)), jnp.tile(sin * sign, (1, 2))


def _layer0_weights(p, T):
    w_in = p['hyb_w_in'][0]
    zc = jnp.zeros((D_MODEL, 64), F32)
    w_all = jnp.concatenate([w_in[:, 0:1728], zc, w_in[:, 1728:2624]], axis=1).astype(BF16)
    mu = p['hyb_shift_mu'][0]
    mu_all = jnp.concatenate([mu[0:1728], jnp.zeros((64,), F32), mu[1728:1856]]).reshape(1, RW_COLS)
    wup = jnp.zeros((384, 2048), F32)
    wup = wup.at[0:64, 0:512].set(p['rwkv_w_up_f'][0])
    wup = wup.at[64:128, 512:1024].set(p['rwkv_w_up_b'][0])
    wup = wup.at[128:192, 1024:1536].set(p['rwkv_a_up'][0])
    wup = wup.at[256:384, 1536:2048].set(p['rwkv_g_up'][0])
    wup = wup.astype(BF16)
    zr = jnp.zeros((512,), F32)
    par = jnp.stack([p['rwkv_w0_f'][0], p['rwkv_w0_b'][0], p['rwkv_a0'][0], p['rwkv_k_k'][0],
                     p['rwkv_k_a'][0], p['rwkv_r_k'][0].reshape(-1), zr, zr]).astype(F32)
    qg = jnp.tile(p['att_q_norm'][0], 8).reshape(1, 512).astype(F32)
    kg = jnp.tile(p['att_k_norm'][0], 2).reshape(1, 128).astype(F32)
    cos_t, sin_t = _rope_tables(T)
    seg = jnp.arange(512) // HEAD
    ones_bd = (seg[:, None] == seg[None, :]).astype(BF16)
    ln = jnp.stack([p['rwkv_lnx_g'][0], p['rwkv_lnx_b'][0]] + [zr] * 6).astype(F32)
    return dict(w_all=w_all, mu_all=mu_all, wup=wup, par=par, qg=qg, kg=kg, cos_t=cos_t, sin_t=sin_t,
                ones_bd=ones_bd, ln=ln, wo=p['hyb_w_out'][0].astype(BF16))


def _row(v):
    return v.reshape(1, -1).astype(F32)


def _mixer0(x2d, p, w, n_seq, T):
    r, k, v, a, b, lf, lb, g, bon, q, k_rep, v_t = _pre0(
        x2d, T, _row(p['mix_norm'][0]), w['w_all'], w['mu_all'], w['wup'], w['par'], w['qg'], w['kg'],
        w['cos_t'], w['sin_t'], w['ones_bd'])
    yf, yb = _wkv(r, k, v, a, b, lf, lb, n_seq, T)
    att = _attention(q, k_rep, v_t, n_seq, T)
    return _post0(x2d, yf, yb, g, bon, att, w['ln'], w['wo'], w['ones_bd'])


def _mixer1(x2d, p, s5, n_seq, T):
    g2, wz, w2, lam, pw = s5
    hn = _norm_f32(x2d, _row(p['mix_norm'][1]))
    ys = _s5_core(hn, g2, wz, w2, lam, pw, n_seq, T)
    return _post1(x2d, ys, _row(p['mix_norm'][1]), _row(p['s5_d'][0]),
                  p['s5_glu_w'][0].astype(BF16), _row(p['s5_glu_b'][0]))


def _ffn(x2d, p, ffn_w, layer):
    return _mlp(x2d, _row(p['ffn_norm'][layer]), ffn_w[layer][0], ffn_w[layer][1])


def _prepare(p, T):
    ffn_w = [(p['ffn_up'][l].astype(BF16), p['ffn_down'][l].astype(BF16)) for l in range(2)]
    return _layer0_weights(p, T), _s5_tables(p, T), ffn_w


def _trunk(x, p, prep=None):
    n_seq, T, _ = x.shape
    w0, s5, ffn_w = _prepare(p, T) if prep is None else prep
    x2d = x.reshape(n_seq * T, D_MODEL)
    x2d = _ffn(_mixer0(x2d, p, w0, n_seq, T), p, ffn_w, 0)
    x2d = _ffn(_mixer1(x2d, p, s5, n_seq, T), p, ffn_w, 1)
    return x2d.reshape(n_seq, T, D_MODEL)


def kernel(x_prompt, x_sample, mix_norm, ffn_norm, ffn_up, ffn_down, hyb_w_in, hyb_shift_mu, rwkv_w0_f, rwkv_w_up_f, rwkv_w0_b, rwkv_w_up_b, rwkv_a0, rwkv_a_up, rwkv_g_up, rwkv_k_k, rwkv_k_a, rwkv_r_k, rwkv_lnx_g, rwkv_lnx_b, att_q_norm, att_k_norm, hyb_w_out, s5_lam_re_f, s5_lam_im_f, s5_log_dt_f, s5_lam_re_b, s5_lam_im_b, s5_log_dt_b, s5_b_re, s5_b_im, s5_c_re_f, s5_c_im_f, s5_c_re_b, s5_c_im_b, s5_d, s5_glu_w, s5_glu_b):
    p = dict(mix_norm=mix_norm, ffn_norm=ffn_norm, ffn_up=ffn_up, ffn_down=ffn_down,
             hyb_w_in=hyb_w_in, hyb_shift_mu=hyb_shift_mu,
             rwkv_w0_f=rwkv_w0_f, rwkv_w_up_f=rwkv_w_up_f, rwkv_w0_b=rwkv_w0_b, rwkv_w_up_b=rwkv_w_up_b,
             rwkv_a0=rwkv_a0, rwkv_a_up=rwkv_a_up, rwkv_g_up=rwkv_g_up,
             rwkv_k_k=rwkv_k_k, rwkv_k_a=rwkv_k_a, rwkv_r_k=rwkv_r_k,
             rwkv_lnx_g=rwkv_lnx_g, rwkv_lnx_b=rwkv_lnx_b,
             att_q_norm=att_q_norm, att_k_norm=att_k_norm, hyb_w_out=hyb_w_out,
             s5_lam_re_f=s5_lam_re_f, s5_lam_im_f=s5_lam_im_f, s5_log_dt_f=s5_log_dt_f,
             s5_lam_re_b=s5_lam_re_b, s5_lam_im_b=s5_lam_im_b, s5_log_dt_b=s5_log_dt_b,
             s5_b_re=s5_b_re, s5_b_im=s5_b_im,
             s5_c_re_f=s5_c_re_f, s5_c_im_f=s5_c_im_f, s5_c_re_b=s5_c_re_b, s5_c_im_b=s5_c_im_b,
             s5_d=s5_d, s5_glu_w=s5_glu_w, s5_glu_b=s5_glu_b)
    assert x_prompt.shape[1] == x_sample.shape[1]
    prep = _prepare(p, x_prompt.shape[1])
    return (_trunk(x_prompt, p, prep), _trunk(x_sample, p, prep))
```

```python
import functools
import math

import jax
import jax.numpy as jnp
from jax import lax
from jax.experimental import pallas as pl
from jax.experimental.pallas import tpu as pltpu

F32 = jnp.float32
BF16 = jnp.bfloat16

D_MODEL = 1024
D_FF = 4 * D_MODEL
RMS_EPS = 1e-6
GRID_W = 64
RWKV_DIM = 512
HEAD = 64
GN_EPS = 64e-5
N_KV = 2
ROPE_THETA = 10000.0
S5_GROUP = 16
S5_GROUPS = D_MODEL // S5_GROUP
S5_STATE = 64
S5_SLAB = 128 // S5_GROUP

WKV_CHUNK = 64
WKV_STEP = 2
S5_CHUNK = 16
RW_COLS = 1920
ALL_COLS = 2688
EXP_M05 = math.exp(-0.5)
LOG2_E = math.log2(math.e)
V_ROWS = HEAD + 16
VMEM_LIMIT = 56 * 1024 * 1024


def _dot(a, b):
    return jnp.dot(a, b, preferred_element_type=F32)


def _dot_nt(a, b):
    return lax.dot_general(a, b, (((1,), (1,)), ((), ())), preferred_element_type=F32)


def _dot_tn(a, b):
    return lax.dot_general(a, b, (((0,), (0,)), ((), ())), preferred_element_type=F32)


def _split2(x):
    hi = x.astype(BF16)
    lo = (x - hi.astype(F32)).astype(BF16)
    return hi, lo


def _seg_sum(x, ones_bd, split=True):
    if not split:
        return _dot(x.astype(BF16), ones_bd)
    hi, lo = _split2(x)
    return _dot(hi, ones_bd) + _dot(lo, ones_bd)


def _rms(x, gain):
    return x * lax.rsqrt(jnp.mean(x * x, axis=-1, keepdims=True) + RMS_EPS) * gain


def _sigmoid(x):
    return 1.0 / (1.0 + jnp.exp(-x))


def _params(sem):
    return pltpu.CompilerParams(dimension_semantics=sem, vmem_limit_bytes=VMEM_LIMIT)


def _pre0_kernel(x_ref, xp_ref, xn_ref, gain_ref, w_ref, mu_ref, wup_ref, par_ref, qg_ref, kg_ref,
                 cos_ref, sin_ref, ones_ref,
                 r_o, k_o, v_o, a_o, b_o, lf_o, lb_o, g_o, bon_o, q_o, ka_o, va_o,
                 *, tiles_per_seq, tm):
    pos = pl.program_id(0) % tiles_per_seq
    gain = gain_ref[...]
    xp = jnp.where(pos == 0, 0.0, xp_ref[...])
    xn = jnp.where(pos == tiles_per_seq - 1, 0.0, xn_ref[...])
    x_all = jnp.concatenate([xp, x_ref[...], xn], axis=0)
    H_all = _dot(_rms(x_all, gain).astype(BF16), w_ref[...])
    H = H_all[8:8 + tm]
    Hr = H[:, :RW_COLS]
    Hr_all = H_all[:, :RW_COLS]
    prev = pltpu.roll(Hr_all, 1, 0)[8:8 + tm]
    nxt = pltpu.roll(Hr_all, tm + 15, 0)[8:8 + tm]
    Hs = Hr + mu_ref[...] * (0.5 * (prev + nxt) - Hr)

    ones = ones_ref[...]
    par = par_ref[...]
    r = Hs[:, 0:512]
    k = Hs[:, 512:1024]
    v = Hs[:, 1024:1536]
    up_w = _dot(jnp.tanh(Hs[:, 1536:1664]).astype(BF16), wup_ref[0:128, 0:1024])
    up_a = _dot(Hs[:, 1664:1792].astype(BF16), wup_ref[128:256, 1024:1536])
    lf_o[...] = -EXP_M05 * _sigmoid(par[0:1] + up_w[:, 0:512])
    lb_o[...] = -EXP_M05 * _sigmoid(par[1:2] + up_w[:, 512:1024])
    a_sig = _sigmoid(par[2:3] + up_a)
    g_o[...] = _dot(_sigmoid(Hs[:, 1792:1920]).astype(BF16), wup_ref[256:384, 1536:2048])
    kk = k * par[3:4]
    kk = kk / jnp.maximum(jnp.sqrt(_seg_sum(kk * kk, ones, split=False)), 1e-12)
    k2 = k * (1.0 + (a_sig - 1.0) * par[4:5])
    r_o[...] = r
    k_o[...] = k2
    v_o[...] = v
    a_o[...] = -kk
    b_o[...] = kk * a_sig
    bon_o[...] = _seg_sum(r * k2 * par[5:6], ones) * v

    qa = H[:, RW_COLS:RW_COLS + 512]
    ka = H[:, RW_COLS + 512:RW_COLS + 640]
    va = H[:, RW_COLS + 640:RW_COLS + 768]
    cos = cos_ref[...]
    sin = sin_ref[...]

    def rope(x, c, s):
        n = x.shape[1]
        lane = lax.broadcasted_iota(jnp.int32, x.shape, 1)
        swapped = jnp.where(lane % 2 == 0, pltpu.roll(x, n - 1, 1), pltpu.roll(x, 1, 1))
        return x * c + swapped * s

    qn = qa * lax.rsqrt(_seg_sum(qa * qa, ones, split=False) * (1.0 / HEAD) + RMS_EPS) * qg_ref[...]
    qr = rope(qn, jnp.concatenate([cos] * 4, axis=1), jnp.concatenate([sin] * 4, axis=1))
    q_o[...] = (qr * (HEAD ** -0.5 * LOG2_E)).astype(BF16)
    kn = ka * lax.rsqrt(_seg_sum(ka * ka, ones[:128, :128], split=False) * (1.0 / HEAD) + RMS_EPS) * kg_ref[...]
    kr = rope(kn, cos, sin)
    lt64 = lax.broadcasted_iota(jnp.int32, kr.shape, 1) < HEAD

    def rep(x):
        sw = pltpu.roll(x, HEAD, 1)
        return jnp.concatenate([jnp.where(lt64, x, sw), jnp.where(lt64, sw, x)], axis=1).astype(BF16)

    ka_o[...] = rep(kr)
    vt = va.T.astype(BF16)
    one = jnp.ones((V_ROWS - HEAD, tm), BF16)
    va_o[0] = jnp.concatenate([vt[0:HEAD], one, vt[HEAD:2 * HEAD], one], axis=0)


def _pre0(x2d, T, gain, w_all, mu_all, wup, par, qg, kg, cos_t, sin_t, ones_bd):
    N = x2d.shape[0]
    tm = min(256, T)
    tps = T // tm
    nt = N // tm
    t8 = tm // 8
    nb8 = N // 8
    const = lambda i: (0, 0)
    tile = lambda i: (i, 0)
    f512 = jax.ShapeDtypeStruct((N, 512), F32)
    out_shape = [f512] * 9 + [jax.ShapeDtypeStruct((N, 512), BF16), jax.ShapeDtypeStruct((N, 256), BF16),
                              jax.ShapeDtypeStruct((N // T, 2 * V_ROWS, T), BF16)]
    out_specs = [pl.BlockSpec((tm, 512), tile)] * 10 + [
        pl.BlockSpec((tm, 256), tile), pl.BlockSpec((1, 2 * V_ROWS, tm), lambda i: (i // tps, 0, i % tps))]
    return pl.pallas_call(
        functools.partial(_pre0_kernel, tiles_per_seq=tps, tm=tm),
        grid=(nt,),
        in_specs=[
            pl.BlockSpec((tm, D_MODEL), tile),
            pl.BlockSpec((8, D_MODEL), lambda i: (jnp.maximum(i * t8 - 1, 0), 0)),
            pl.BlockSpec((8, D_MODEL), lambda i: (jnp.minimum((i + 1) * t8, nb8 - 1), 0)),
            pl.BlockSpec((1, D_MODEL), const),
            pl.BlockSpec((D_MODEL, ALL_COLS), const),
            pl.BlockSpec((1, RW_COLS), const),
            pl.BlockSpec((384, 2048), const),
            pl.BlockSpec((8, 512), const),
            pl.BlockSpec((1, 512), const),
            pl.BlockSpec((1, 128), const),
            pl.BlockSpec((tm, 128), lambda i: (i % tps, 0)),
            pl.BlockSpec((tm, 128), lambda i: (i % tps, 0)),
            pl.BlockSpec((512, 512), const),
        ],
        out_specs=out_specs,
        out_shape=out_shape,
        compiler_params=_params(("parallel",)),
        name="pre0",
    )(x2d, x2d, x2d, gain, w_all, mu_all, wup, par, qg, kg, cos_t, sin_t, ones_bd)


def _wkv_direction(r, k, v, a, b, L, h_ref, fwd):
    C = WKV_CHUNK
    Q = 4 * HEAD
    ti = lax.broadcasted_iota(jnp.int32, (C, C), 0)
    si = lax.broadcasted_iota(jnp.int32, (C, C), 1)
    tri = jnp.where(si <= ti, 1.0, 0.0).astype(BF16)
    l1 = L.astype(BF16)
    rem = L - l1.astype(F32)
    l2 = rem.astype(BF16)
    l3 = (rem - l2.astype(F32)).astype(BF16)
    cs = _dot(tri, l1) + _dot(tri, l2) + _dot(tri, l3)
    total = cs[C - 1:C, :]
    if fwd:
        cs_incl = cs
        cs_excl = cs - L
    else:
        cs_incl = total - (cs - L)
        cs_excl = total - cs
    e_incl = jnp.exp(cs_incl)
    e_inv = jnp.exp(-cs_incl)
    e_rem = jnp.exp(total - cs_incl)
    a_t = a * jnp.exp(cs_excl)
    r_t = r * e_incl
    b_t = b * e_inv
    k_t = k * e_inv
    b_h = b * e_rem
    k_h = k * e_rem
    gam = jnp.exp(total)

    streams = []
    for q in range(RWKV_DIM // Q):
        sl = slice(q * Q, (q + 1) * Q)
        streams.append(dict(
            q=q, h_ref=h_ref, fwd=fwd, gam=gam[:, sl], a_t=a_t[:, sl], r_t=r_t[:, sl], b_t=b_t[:, sl],
            k_t=k_t[:, sl], v=v[:, sl], b_h=b_h[:, sl], k_h=k_h[:, sl]))
    return streams


def _wkv_solve(streams):
    C = WKV_CHUNK
    Q = 4 * HEAD
    lane_q = lax.broadcasted_iota(jnp.int32, (C, Q), 1) // HEAD
    ri = lax.broadcasted_iota(jnp.int32, (Q, Q), 0)
    ci = lax.broadcasted_iota(jnp.int32, (Q, Q), 1)
    same = (ri // C) == (ci // C)
    tw = lax.broadcasted_iota(jnp.int32, (C, Q), 0)
    sw = lax.broadcasted_iota(jnp.int32, (C, Q), 1) % C
    masks = {True: (sw < tw, sw <= tw), False: (sw > tw, sw >= tw)}
    eye_w = sw == tw

    def stack(xq):
        return jnp.concatenate([jnp.where(lane_q == h, xq, 0.0) for h in range(4)], axis=0).astype(BF16)

    def blockdiag(w):
        return jnp.where(same, jnp.concatenate([w] * 4, axis=0), 0.0).astype(BF16)

    for s in streams:
        strict, incl = masks[s['fwd']]
        P = _dot_nt(jnp.concatenate([s['a_t'], s['r_t']], axis=0).astype(BF16),
                    jnp.concatenate([stack(s['b_t']), stack(s['k_t'])], axis=0))
        A_ab = jnp.where(strict, P[:C, :Q], 0.0)
        s['A_k'] = jnp.concatenate([jnp.where(strict, P[:C, Q:], 0.0), jnp.where(incl, P[C:, Q:], 0.0)],
                                   axis=0).astype(BF16)
        s['A_rb'] = jnp.where(incl, P[C:, :Q], 0.0).astype(BF16)
        s['T'] = jnp.where(eye_w, 1.0, A_ab)
        s['Ap'] = A_ab
        s['Ap_bd'] = blockdiag(A_ab)
        s['Vs'] = stack(s['v'])
    for _ in range(int(math.log2(C)) - 1):
        for s in streams:
            s['Ap'] = _dot(s['Ap'].astype(BF16), s['Ap_bd'])
            s['Ap_bd'] = blockdiag(s['Ap'])
        for s in streams:
            s['T'] = s['T'] + _dot(s['T'].astype(BF16), s['Ap_bd'])
    for s in streams:
        zy = _dot(s['A_k'], s['Vs'])
        s['Z'] = zy[:C]
        s['y_k'] = zy[C:]
        s['n_k'] = _dot_tn(s['k_h'].astype(BF16), s['v'].astype(BF16))
    for s in streams:
        s['X'] = _dot(s['T'].astype(BF16), jnp.concatenate([stack(s['a_t']), stack(s['Z'])], axis=1))
    for s in streams:
        X = s['X']
        W1 = _dot(s['A_rb'], jnp.concatenate([stack(X[:, :Q]), stack(X[:, Q:])], axis=1))
        MN = _dot_tn(s['b_h'].astype(BF16), X.astype(BF16))
        s['r_p'] = s['r_t'] + W1[:, :Q]
        s['y_p'] = W1[:, Q:] + s['y_k']
        s['M'] = jnp.where(same, MN[:, :Q], 0.0)
        s['N'] = jnp.where(same, MN[:, Q:] + s['n_k'], 0.0)
    ys = []
    for s in streams:
        h0 = s['h_ref'][s['q']]
        gam_col = jnp.sum(jnp.where(ri == ci, s['gam'], 0.0), axis=1, keepdims=True)
        seq = _dot(jnp.concatenate([s['r_p'], s['M']], axis=0).astype(BF16), h0.astype(BF16))
        ys.append(seq[:C] + s['y_p'])
        s['h_ref'][s['q']] = gam_col * h0 + seq[C:] + s['N']
    return ys


def _wkv_kernel(rf, kf, vf, af, bf, lf, rb, kb, vb, ab, bb, lb, yf_o, yb_o, hf_ref, hb_ref):
    @pl.when(pl.program_id(1) == 0)
    def _():
        hf_ref[...] = jnp.zeros_like(hf_ref)
        hb_ref[...] = jnp.zeros_like(hb_ref)

    C = WKV_CHUNK
    streams = []
    for refs, h_ref, fwd in (((rf, kf, vf, af, bf, lf), hf_ref, True), ((rb, kb, vb, ab, bb, lb), hb_ref, False)):
        for n in (range(WKV_STEP) if fwd else range(WKV_STEP - 1, -1, -1)):
            streams += _wkv_direction(*(x[n * C:(n + 1) * C, :] for x in refs), h_ref, fwd)
    ys = _wkv_solve(streams)
    for n in range(WKV_STEP):
        f0 = 2 * n
        b0 = 2 * WKV_STEP + 2 * (WKV_STEP - 1 - n)
        yf_o[n * C:(n + 1) * C, :] = jnp.concatenate(ys[f0:f0 + 2], axis=1)
        yb_o[n * C:(n + 1) * C, :] = jnp.concatenate(ys[b0:b0 + 2], axis=1)


def _wkv(r, k, v, a, b, lf, lb, n_seq, T):
    C = WKV_CHUNK * WKV_STEP
    nc = T // C
    fw = lambda s, i: (s * nc + i, 0)
    bw = lambda s, i: (s * nc + nc - 1 - i, 0)
    spec_f = pl.BlockSpec((C, 512), fw)
    spec_b = pl.BlockSpec((C, 512), bw)
    shp = jax.ShapeDtypeStruct(r.shape, F32)
    return pl.pallas_call(
        _wkv_kernel,
        grid=(n_seq, nc),
        in_specs=[spec_f] * 6 + [spec_b] * 6,
        out_specs=[spec_f, spec_b],
        out_shape=[shp, shp],
        scratch_shapes=[pltpu.VMEM((2, 256, 256), F32), pltpu.VMEM((2, 256, 256), F32)],
        compiler_params=_params(("parallel", "arbitrary")),
        name="wkv",
    )(r, k, v, a, b, lf, r, k, v, a, b, lb)


def _attn_kernel(q_ref, k_ref, vt_ref, o_ref, s0_ref, s1_ref, p0_ref, p1_ref, acc_ref, *, tq, tk, T):
    q = q_ref[...].astype(F32)
    lo = lax.broadcasted_iota(jnp.int32, (tq, 128), 1) < HEAD
    qs = jnp.concatenate([
        jnp.where(lo, q[:, 0:128], 0.0), jnp.where(lo, 0.0, q[:, 0:128]),
        jnp.where(lo, q[:, 128:256], 0.0), jnp.where(lo, 0.0, q[:, 128:256])], axis=0)
    qst = qs.T.astype(BF16)
    R = 4 * tq
    n = T // tk

    def chunk(j):
        return pl.ds(j * tk if isinstance(j, int) else pl.multiple_of(j * tk, tk), tk)

    def scores(j):
        return _dot(k_ref[chunk(j), :], qst)

    def values(j, p):
        return _dot(vt_ref[0, :, chunk(j)], p)

    s_buf = (s0_ref, s1_ref)
    p_buf = (p0_ref, p1_ref)
    s_buf[0][...] = scores(0)
    p_buf[1][...] = jnp.zeros((tk, R), BF16)
    acc_ref[...] = jnp.zeros((V_ROWS, R), F32)

    def step(j, b, carry, ahead=True):
        m, alpha_prev = carry
        if ahead:
            s_buf[1 - b][...] = scores(j + 1)
        prev = max(j - 1, 0) if isinstance(j, int) else jnp.maximum(j - 1, 0)
        acc_ref[...] = alpha_prev * acc_ref[...] + values(prev, p_buf[1 - b][...])
        s = s_buf[b][...]
        m_new = jnp.maximum(m, jnp.max(s, axis=0, keepdims=True))
        p_buf[b][...] = jnp.exp2(s - m_new).astype(BF16)
        return m_new, jnp.exp2(m - m_new)

    def body(i, carry):
        return step(2 * i + 1, 1, step(2 * i, 0, carry))

    carry = lax.fori_loop(0, n // 2 - 1, body, (jnp.full((1, R), -1e30, F32), jnp.ones((1, R), F32)))
    carry = step(n - 2, 0, carry)
    _, alpha = step(n - 1, 1, carry, ahead=False)
    acc = alpha * acc_ref[...] + values(n - 1, p_buf[1][...])
    ot = acc[0:HEAD] / acc[HEAD:HEAD + 1]
    o01 = jnp.concatenate([ot[:, 0:tq], ot[:, tq:2 * tq]], axis=0).T
    o23 = jnp.concatenate([ot[:, 2 * tq:3 * tq], ot[:, 3 * tq:4 * tq]], axis=0).T
    o_ref[...] = jnp.concatenate([o01, o23], axis=1)


def _attention(q, k_rep, v_t, n_seq, T):
    tq = min(1024, T)
    tk = min(512, T // 2)
    nq = T // tq
    assert (T // tk) % 2 == 0
    return pl.pallas_call(
        functools.partial(_attn_kernel, tq=tq, tk=tk, T=T),
        grid=(n_seq, N_KV, nq),
        in_specs=[
            pl.BlockSpec((tq, 256), lambda s, h, i: (s * nq + i, h)),
            pl.BlockSpec((T, 128), lambda s, h, i: (s, h)),
            pl.BlockSpec((1, V_ROWS, T), lambda s, h, i: (s, h, 0)),
        ],
        out_specs=pl.BlockSpec((tq, 256), lambda s, h, i: (s * nq + i, h)),
        out_shape=jax.ShapeDtypeStruct(q.shape, F32),
        scratch_shapes=[pltpu.VMEM((tk, 4 * tq), F32)] * 2 + [pltpu.VMEM((tk, 4 * tq), BF16)] * 2 + [
            pltpu.VMEM((V_ROWS, 4 * tq), F32)],
        compiler_params=_params(("parallel", "parallel", "arbitrary")),
        name="attention",
    )(q, k_rep, v_t)


MLP_TILE = 512
MLP_FF_CHUNK = 1024


def _mlp_step(hn_ref, wu_ref, wd_ref, o_ref):
    u = jnp.maximum(_dot(hn_ref[...], wu_ref[...]), 0.0)
    o_ref[...] += _dot((u * u).astype(BF16), wd_ref[...])


def _post0_mlp_kernel(x_ref, yf_ref, yb_ref, g_ref, bon_ref, att_ref, ln_ref, wo_ref, ones_ref, gain_ref,
                      gain_next_ref, wu_ref, wd_ref, o_ref, hn_next_ref, hn_ref):
    @pl.when(pl.program_id(1) == 0)
    def _():
        ones = ones_ref[...]
        y = yf_ref[...] + yb_ref[...]
        mean = _seg_sum(y, ones) * (1.0 / HEAD)
        d = y - mean
        var = _seg_sum(d * d, ones) * (1.0 / HEAD)
        yn = d * lax.rsqrt(var + GN_EPS) * ln_ref[0:1] + ln_ref[1:2]
        ya = ((yn + bon_ref[...]) * g_ref[...]).astype(BF16)
        mix = _dot(ya, wo_ref[0:512, :]) + _dot(att_ref[...].astype(BF16), wo_ref[512:1024, :])
        x1 = x_ref[...] + mix
        o_ref[...] = x1
        hn_ref[...] = _rms(x1, gain_ref[...]).astype(BF16)

    _mlp_step(hn_ref, wu_ref, wd_ref, o_ref)

    @pl.when(pl.program_id(1) == pl.num_programs(1) - 1)
    def _():
        hn_next_ref[...] = _rms(o_ref[...], gain_next_ref[...])


def _mlp_specs(tm):
    row = lambda i, j: (i, 0)
    const = lambda i, j: (0, 0)
    weights = [pl.BlockSpec((D_MODEL, MLP_FF_CHUNK), lambda i, j: (0, j)),
               pl.BlockSpec((MLP_FF_CHUNK, D_MODEL), lambda i, j: (j, 0))]
    return row, const, weights


def _post0_mlp(x2d, yf, yb, g, bon, att, ln, wo, ones_bd, gain, gain_next, w_up, w_down):
    N = x2d.shape[0]
    tm = min(MLP_TILE, N)
    row, const, weights = _mlp_specs(tm)
    wide = pl.BlockSpec((tm, D_MODEL), row)
    vec = pl.BlockSpec((1, D_MODEL), const)
    shp = jax.ShapeDtypeStruct(x2d.shape, F32)
    return pl.pallas_call(
        _post0_mlp_kernel,
        grid=(N // tm, D_FF // MLP_FF_CHUNK),
        in_specs=[wide] + [pl.BlockSpec((tm, 512), row)] * 5 + [
            pl.BlockSpec((8, 512), const), pl.BlockSpec((D_MODEL, D_MODEL), const), pl.BlockSpec((512, 512), const),
            vec, vec] + weights,
        out_specs=[wide, wide],
        out_shape=[shp, shp],
        scratch_shapes=[pltpu.VMEM((tm, D_MODEL), BF16)],
        compiler_params=_params(("parallel", "arbitrary")),
        name="post0_mlp",
    )(x2d, yf, yb, g, bon, att, ln, wo, ones_bd, gain, gain_next, w_up, w_down)


def _cmul_add(ar, ai, br, bi, cr, ci):
    return ar * br - ai * bi + cr, ar * bi + ai * br + ci


S5_SEGMENTS = 8


def _s5_scan(zs_ref, ps_ref, lam, pw_ref, fwd, nc):
    nl = zs_ref.shape[1]
    W = nl * 128
    ls = nc // S5_SEGMENTS
    ns = S5_SEGMENTS

    def gather(c, k):
        return jnp.concatenate([zs_ref[c, l, pl.ds(k, ns, stride=ls), :] for l in range(nl)], axis=1)

    def load(c, k):
        return jnp.concatenate([ps_ref[c, l, k * ns:(k + 1) * ns, :] for l in range(nl)], axis=1)

    def store(c, k, v):
        for l in range(nl):
            ps_ref[c, l, k * ns:(k + 1) * ns, :] = v[:, l * 128:(l + 1) * 128]

    xr = xi = jnp.zeros((ns, W), F32)
    for k in (range(ls) if fwd else range(ls - 1, -1, -1)):
        store(0, k, xr)
        store(1, k, xi)
        xr, xi = _cmul_add(lam[0:1], lam[1:2], xr, xi, gather(0, k), gather(1, k))
    cr = ci = jnp.zeros((1, W), F32)
    rows_r = [None] * S5_SEGMENTS
    rows_i = [None] * S5_SEGMENTS
    for s in (range(S5_SEGMENTS) if fwd else range(S5_SEGMENTS - 1, -1, -1)):
        rows_r[s] = cr
        rows_i[s] = ci
        cr, ci = _cmul_add(lam[2:3], lam[3:4], cr, ci, xr[s:s + 1], xi[s:s + 1])
    car_r = jnp.concatenate(rows_r, axis=0)
    car_i = jnp.concatenate(rows_i, axis=0)
    for k in range(ls):
        pr, pi = _cmul_add(pw_ref[0, k:k + 1, :], pw_ref[1, k:k + 1, :], car_r, car_i, load(0, k), load(1, k))
        store(0, k, pr)
        store(1, k, pi)


def _s5_kernel(x_ref, g2_ref, wz_ref, w2_ref, lam_ref, pw_ref, y_ref, lhs_ref, zs_ref, ps_ref, *, nc):
    C = S5_CHUNK
    W = S5_SLAB * S5_STATE
    nl = W // 128
    for j in range(C):
        lhs_ref[:, j * 128:(j + 1) * 128] = x_ref[pl.ds(j, nc, stride=C), :].astype(BF16)
    lhs = lhs_ref[...]
    for d in range(2):
        z = _dot(lhs, wz_ref[0, :, d * 2 * W:(d + 1) * 2 * W])
        for c in range(2):
            for l in range(nl):
                zs_ref[d, c, l] = z[:, c * W + l * 128:c * W + (l + 1) * 128]
        _s5_scan(zs_ref.at[d], ps_ref.at[d], lam_ref[0, d], pw_ref.at[0, d], d == 0, nc)
    ls = nc // S5_SEGMENTS

    def chunk_order(ref):
        return jnp.concatenate([ref[pl.ds(s, ls, stride=S5_SEGMENTS), :] for s in range(S5_SEGMENTS)], axis=0)

    pv = jnp.concatenate([chunk_order(ps_ref.at[d, c, l]) for d in range(2) for c in range(2) for l in range(nl)],
                         axis=1).astype(BF16)
    for i in range(0, C, 2):
        w_loc = g2_ref[0, (C - 1 - i) * 128:(2 * C - 1 - i) * 128, :]
        y2 = _dot(lhs, w_loc) + _dot(pv, w2_ref[0, :, i * 128:(i + 2) * 128])
        y_ref[pl.ds(i, nc, stride=C), :] = y2[:, :128]
        y_ref[pl.ds(i + 1, nc, stride=C), :] = y2[:, 128:]


def _s5_core(hn, g2, wz, w2, lam, pw, n_seq, T):
    nc = T // S5_CHUNK
    ls = nc // S5_SEGMENTS
    W = S5_SLAB * S5_STATE
    n_slab = D_MODEL // 128
    once = pl.Buffered(1)
    slab = lambda c, s: (c, 0, 0)
    return pl.pallas_call(
        functools.partial(_s5_kernel, nc=nc),
        grid=(n_slab, n_seq),
        in_specs=[
            pl.BlockSpec((T, 128), lambda c, s: (s, c)),
            pl.BlockSpec((1, 2 * S5_CHUNK * 128, 256), slab, pipeline_mode=once),
            pl.BlockSpec((1, S5_CHUNK * 128, 4 * W), slab, pipeline_mode=once),
            pl.BlockSpec((1, 4 * W, S5_CHUNK * 128), slab, pipeline_mode=once),
            pl.BlockSpec((1, 2, 8, W), lambda c, s: (c, 0, 0, 0)),
            pl.BlockSpec((1, 2, 2, ls, W), lambda c, s: (c, 0, 0, 0, 0)),
        ],
        out_specs=pl.BlockSpec((T, 128), lambda c, s: (s, c)),
        out_shape=jax.ShapeDtypeStruct(hn.shape, F32),
        scratch_shapes=[
            pltpu.VMEM((nc, S5_CHUNK * 128), BF16),
            pltpu.VMEM((2, 2, W // 128, nc, 128), F32), pltpu.VMEM((2, 2, W // 128, nc, 128), F32)],
        compiler_params=_params(("parallel", "arbitrary")),
        name="s5_core",
    )(hn, g2, wz, w2, lam, pw)


def _post1_mlp_kernel(x_ref, ys_ref, gain_ref, d_ref, wg_ref, bg_ref, gain2_ref, wu_ref, wd_ref, o_ref, hn_ref):
    @pl.when(pl.program_id(1) == 0)
    def _():
        x = x_ref[...]
        y = _rms(x, gain_ref[...]) * d_ref[...] + ys_ref[...]
        z = 0.5 * y * (1.0 + jnp.tanh(math.sqrt(2.0 / math.pi) * (y + 0.044715 * (y * y * y))))
        gate = _sigmoid(_dot(z.astype(BF16), wg_ref[...]) + bg_ref[...])
        x1 = x + z * gate
        o_ref[...] = x1
        hn_ref[...] = _rms(x1, gain2_ref[...]).astype(BF16)

    _mlp_step(hn_ref, wu_ref, wd_ref, o_ref)


def _post1_mlp(x2d, ys, gain, d, wg, bg, gain2, w_up, w_down):
    N = x2d.shape[0]
    tm = min(MLP_TILE, N)
    row, const, weights = _mlp_specs(tm)
    wide = pl.BlockSpec((tm, D_MODEL), row)
    vec = pl.BlockSpec((1, D_MODEL), const)
    return pl.pallas_call(
        _post1_mlp_kernel,
        grid=(N // tm, D_FF // MLP_FF_CHUNK),
        in_specs=[wide, wide, vec, vec, pl.BlockSpec((D_MODEL, D_MODEL), const), vec, vec] + weights,
        out_specs=wide,
        out_shape=jax.ShapeDtypeStruct(x2d.shape, F32),
        scratch_shapes=[pltpu.VMEM((tm, D_MODEL), BF16)],
        compiler_params=_params(("parallel", "arbitrary")),
        name="post1_mlp",
    )(x2d, ys, gain, d, wg, bg, gain2, w_up, w_down)


def _s5_tables(p, T):
    C, G, P, SL = S5_CHUNK, S5_GROUPS, S5_STATE, S5_SLAB
    ns = G // SL
    ls = T // C // S5_SEGMENTS
    eye = jnp.eye(SL, dtype=F32)
    b_re = p['s5_b_re'][0].astype(F32)
    b_im = p['s5_b_im'][0].astype(F32)
    steps = jnp.arange(C, dtype=F32)

    def direction(sfx):
        lr = p['s5_lam_re_' + sfx][0].astype(F32)
        li = p['s5_lam_im_' + sfx][0].astype(F32)
        dt = jnp.exp(p['s5_log_dt_' + sfx][0].astype(F32))[:, None]

        def power(k):
            k = k[:, None, None]
            mag = jnp.exp(lr * dt * k)
            return mag * jnp.cos(li * dt * k), mag * jnp.sin(li * dt * k)

        l1r, l1i = power(jnp.ones((1,), F32))
        nr, ni = l1r[0] - 1.0, l1i[0]
        den = lr * lr + li * li
        cr = (nr * lr + ni * li) / den
        ci = (ni * lr - nr * li) / den
        cb_r = cr[:, :, None] * b_re - ci[:, :, None] * b_im
        cb_i = cr[:, :, None] * b_im + ci[:, :, None] * b_re
        c_r = p['s5_c_re_' + sfx][0].astype(F32)
        c_i = p['s5_c_im_' + sfx][0].astype(F32)
        return power, cb_r, cb_i, c_r, c_i

    def kernels(power, cb_r, cb_i, c_r, c_i):
        pr, pi = power(steps)
        d_r = pr[..., None] * cb_r - pi[..., None] * cb_i
        d_i = pr[..., None] * cb_i + pi[..., None] * cb_r
        k = jnp.einsum('gop,lgpi->lgio', c_r, d_r) - jnp.einsum('gop,lgpi->lgio', c_i, d_i)
        k = k.reshape(C, ns, SL, S5_GROUP, 1, S5_GROUP)
        return jnp.where(eye[:, None, :, None] > 0, k, 0.0).reshape(C, ns, 128, 128)

    def state_in(power, cb_r, cb_i, ks):
        pr, pi = power(ks)
        w_r = (pr[..., None] * cb_r - pi[..., None] * cb_i).reshape(C, ns, SL, P, S5_GROUP)
        w_i = (pr[..., None] * cb_i + pi[..., None] * cb_r).reshape(C, ns, SL, P, S5_GROUP)
        return [w_r, w_i]

    def state_out(power, c_r, c_i, ks):
        pr, pi = power(ks)
        e_r = (c_r[None] * pr[:, :, None, :] - c_i[None] * pi[:, :, None, :]).reshape(C, ns, SL, S5_GROUP, P)
        e_i = (c_r[None] * pi[:, :, None, :] + c_i[None] * pr[:, :, None, :]).reshape(C, ns, SL, S5_GROUP, P)
        return [e_r, -e_i]

    def scan_tables(power, ks_rows):
        lr_, li_ = power(C * jnp.array([1.0, ls], F32))
        lam = jnp.stack([lr_, li_], axis=1).reshape(4, ns, SL * P)
        lam = jnp.concatenate([lam, jnp.zeros_like(lam)], axis=0).transpose(1, 0, 2)
        pr, pi = power(C * ks_rows)
        pw = jnp.stack([pr.reshape(ls, ns, SL * P), pi.reshape(ls, ns, SL * P)], axis=0).transpose(2, 0, 1, 3)
        return lam, pw

    def expand(x, row_group, col_group):
        src = jnp.arange(2048) // (col_group * SL) * col_group + jnp.arange(2048) % col_group
        spread = (jnp.arange(256)[:, None] == src[None, :]).astype(BF16)
        full = jnp.einsum('srk,kc->src', x.astype(BF16), spread, preferred_element_type=F32)
        same = ((jnp.arange(2048) // row_group) % SL)[:, None] == ((jnp.arange(2048) // col_group) % SL)[None, :]
        return jnp.where(same, full, 0.0).astype(BF16)

    pf = direction('f')
    pb = direction('b')
    kf = kernels(*pf)
    kb = kernels(*pb)
    zero = jnp.zeros((1, ns, 128, 128), F32)
    gen = jnp.concatenate([kf[:0:-1], (kf[0] + kb[0])[None], kb[1:], zero], axis=0)
    gen_prev = jnp.concatenate([zero, gen[:-1]], axis=0)
    g2 = jnp.concatenate([gen, gen_prev], axis=-1).transpose(1, 0, 2, 3).reshape(ns, 2 * C * 128, 256)

    w4 = jnp.stack(state_in(pf[0], pf[1], pf[2], (C - 1) - steps) + state_in(pb[0], pb[1], pb[2], steps))
    wz = expand(w4.transpose(2, 1, 3, 5, 0, 4).reshape(ns, C * 128, 4 * P), S5_GROUP, P)
    e4 = jnp.stack(state_out(pf[0], pf[3], pf[4], steps + 1.0) + state_out(pb[0], pb[3], pb[4], C - steps))
    w2 = expand(e4.transpose(2, 0, 3, 5, 1, 4).reshape(ns, 4 * SL * P, C * S5_GROUP), P, S5_GROUP)
    pos = jnp.arange(ls, dtype=F32)
    lam_f, pw_f = scan_tables(pf[0], pos)
    lam_b, pw_b = scan_tables(pb[0], (ls - 1.0) - pos)
    lam = jnp.stack([lam_f, lam_b], axis=1)
    pw = jnp.stack([pw_f, pw_b], axis=1)
    return g2.astype(BF16), wz, w2, lam, pw


def _rope_tables(T):
    rows = T // GRID_W
    row_ids = jnp.repeat(jnp.arange(rows, dtype=F32), GRID_W)
    col_ids = jnp.tile(jnp.arange(GRID_W, dtype=F32), rows)
    pairs = HEAD // 4
    inv_freq = ROPE_THETA ** (-jnp.arange(pairs, dtype=F32) / pairs)
    ang = jnp.concatenate([row_ids[:, None] * inv_freq, col_ids[:, None] * inv_freq], axis=-1)
    cos = jnp.repeat(jnp.cos(ang), 2, axis=-1)
    sin = jnp.repeat(jnp.sin(ang), 2, axis=-1)
    sign = jnp.tile(jnp.array([-1.0, 1.0], F32), HEAD // 2)
    return jnp.tile(cos, (1, 2)), jnp.tile(sin * sign, (1, 2))


def _layer0_weights(p, T):
    w_in = p['hyb_w_in'][0]
    zc = jnp.zeros((D_MODEL, 64), F32)
    w_all = jnp.concatenate([w_in[:, 0:1728], zc, w_in[:, 1728:2624]], axis=1).astype(BF16)
    mu = p['hyb_shift_mu'][0]
    mu_all = jnp.concatenate([mu[0:1728], jnp.zeros((64,), F32), mu[1728:1856]]).reshape(1, RW_COLS)
    wup = jnp.zeros((384, 2048), F32)
    wup = wup.at[0:64, 0:512].set(p['rwkv_w_up_f'][0])
    wup = wup.at[64:128, 512:1024].set(p['rwkv_w_up_b'][0])
    wup = wup.at[128:192, 1024:1536].set(p['rwkv_a_up'][0])
    wup = wup.at[256:384, 1536:2048].set(p['rwkv_g_up'][0])
    wup = wup.astype(BF16)
    zr = jnp.zeros((512,), F32)
    par = jnp.stack([p['rwkv_w0_f'][0], p['rwkv_w0_b'][0], p['rwkv_a0'][0], p['rwkv_k_k'][0],
                     p['rwkv_k_a'][0], p['rwkv_r_k'][0].reshape(-1), zr, zr]).astype(F32)
    qg = jnp.tile(p['att_q_norm'][0], 8).reshape(1, 512).astype(F32)
    kg = jnp.tile(p['att_k_norm'][0], 2).reshape(1, 128).astype(F32)
    cos_t, sin_t = _rope_tables(T)
    seg = jnp.arange(512) // HEAD
    ones_bd = (seg[:, None] == seg[None, :]).astype(BF16)
    ln = jnp.stack([p['rwkv_lnx_g'][0], p['rwkv_lnx_b'][0]] + [zr] * 6).astype(F32)
    return dict(w_all=w_all, mu_all=mu_all, wup=wup, par=par, qg=qg, kg=kg, cos_t=cos_t, sin_t=sin_t,
                ones_bd=ones_bd, ln=ln, wo=p['hyb_w_out'][0].astype(BF16))


def _row(v):
    return v.reshape(1, -1).astype(F32)


def _layer0(x2d, p, w, ffn_w, n_seq, T):
    r, k, v, a, b, lf, lb, g, bon, q, k_rep, v_t = _pre0(
        x2d, T, _row(p['mix_norm'][0]), w['w_all'], w['mu_all'], w['wup'], w['par'], w['qg'], w['kg'],
        w['cos_t'], w['sin_t'], w['ones_bd'])
    yf, yb = _wkv(r, k, v, a, b, lf, lb, n_seq, T)
    att = _attention(q, k_rep, v_t, n_seq, T)
    return _post0_mlp(x2d, yf, yb, g, bon, att, w['ln'], w['wo'], w['ones_bd'], _row(p['ffn_norm'][0]),
                      _row(p['mix_norm'][1]), ffn_w[0][0], ffn_w[0][1])


def _layer1(x2d, hn, p, s5, ffn_w, n_seq, T):
    g2, wz, w2, lam, pw = s5
    ys = _s5_core(hn, g2, wz, w2, lam, pw, n_seq, T)
    return _post1_mlp(x2d, ys, _row(p['mix_norm'][1]), _row(p['s5_d'][0]), p['s5_glu_w'][0].astype(BF16),
                      _row(p['s5_glu_b'][0]), _row(p['ffn_norm'][1]), ffn_w[1][0], ffn_w[1][1])


def _prepare(p, T):
    ffn_w = [(p['ffn_up'][l].astype(BF16), p['ffn_down'][l].astype(BF16)) for l in range(2)]
    return _layer0_weights(p, T), _s5_tables(p, T), ffn_w


def _trunk(x, p, prep=None):
    n_seq, T, _ = x.shape
    w0, s5, ffn_w = _prepare(p, T) if prep is None else prep
    x2d = x.reshape(n_seq * T, D_MODEL)
    x2d, hn = _layer0(x2d, p, w0, ffn_w, n_seq, T)
    x2d = _layer1(x2d, hn, p, s5, ffn_w, n_seq, T)
    return x2d.reshape(n_seq, T, D_MODEL)


def kernel(x_prompt, x_sample, mix_norm, ffn_norm, ffn_up, ffn_down, hyb_w_in, hyb_shift_mu, rwkv_w0_f, rwkv_w_up_f, rwkv_w0_b, rwkv_w_up_b, rwkv_a0, rwkv_a_up, rwkv_g_up, rwkv_k_k, rwkv_k_a, rwkv_r_k, rwkv_lnx_g, rwkv_lnx_b, att_q_norm, att_k_norm, hyb_w_out, s5_lam_re_f, s5_lam_im_f, s5_log_dt_f, s5_lam_re_b, s5_lam_im_b, s5_log_dt_b, s5_b_re, s5_b_im, s5_c_re_f, s5_c_im_f, s5_c_re_b, s5_c_im_b, s5_d, s5_glu_w, s5_glu_b):
    p = dict(mix_norm=mix_norm, ffn_norm=ffn_norm, ffn_up=ffn_up, ffn_down=ffn_down,
             hyb_w_in=hyb_w_in, hyb_shift_mu=hyb_shift_mu,
             rwkv_w0_f=rwkv_w0_f, rwkv_w_up_f=rwkv_w_up_f, rwkv_w0_b=rwkv_w0_b, rwkv_w_up_b=rwkv_w_up_b,
             rwkv_a0=rwkv_a0, rwkv_a_up=rwkv_a_up, rwkv_g_up=rwkv_g_up,
             rwkv_k_k=rwkv_k_k, rwkv_k_a=rwkv_k_a, rwkv_r_k=rwkv_r_k,
             rwkv_lnx_g=rwkv_lnx_g, rwkv_lnx_b=rwkv_lnx_b,
             att_q_norm=att_q_norm, att_k_norm=att_k_norm, hyb_w_out=hyb_w_out,
             s5_lam_re_f=s5_lam_re_f, s5_lam_im_f=s5_lam_im_f, s5_log_dt_f=s5_log_dt_f,
             s5_lam_re_b=s5_lam_re_b, s5_lam_im_b=s5_lam_im_b, s5_log_dt_b=s5_log_dt_b,
             s5_b_re=s5_b_re, s5_b_im=s5_b_im,
             s5_c_re_f=s5_c_re_f, s5_c_im_f=s5_c_im_f, s5_c_re_b=s5_c_re_b, s5_c_im_b=s5_c_im_b,
             s5_d=s5_d, s5_glu_w=s5_glu_w, s5_glu_b=s5_glu_b)
    assert x_prompt.shape[1] == x_sample.shape[1]
    prep = _prepare(p, x_prompt.shape[1])
    return (_trunk(x_prompt, p, prep), _trunk(x_sample, p, prep))
```

```python
import functools
import math

import jax
import jax.numpy as jnp
from jax import lax
from jax.experimental import pallas as pl
from jax.experimental.pallas import tpu as pltpu

F32 = jnp.float32
BF16 = jnp.bfloat16

D_MODEL = 1024
D_FF = 4 * D_MODEL
RMS_EPS = 1e-6
GRID_W = 64
RWKV_DIM = 512
HEAD = 64
GN_EPS = 64e-5
N_KV = 2
ROPE_THETA = 10000.0
S5_GROUP = 16
S5_GROUPS = D_MODEL // S5_GROUP
S5_STATE = 64
S5_SLAB = 128 // S5_GROUP

WKV_CHUNK = 64
WKV_STEP = 2
S5_CHUNK = 16
RW_COLS = 1920
ALL_COLS = 2688
EXP_M05 = math.exp(-0.5)
LOG2_E = math.log2(math.e)
V_ROWS = HEAD + 16
VMEM_LIMIT = 56 * 1024 * 1024


def _dot(a, b):
    return jnp.dot(a, b, preferred_element_type=F32)


def _dot_nt(a, b):
    return lax.dot_general(a, b, (((1,), (1,)), ((), ())), preferred_element_type=F32)


def _dot_tn(a, b):
    return lax.dot_general(a, b, (((0,), (0,)), ((), ())), preferred_element_type=F32)


def _split2(x):
    hi = x.astype(BF16)
    lo = (x - hi.astype(F32)).astype(BF16)
    return hi, lo


def _seg_sum(x, ones_bd, split=True):
    if not split:
        return _dot(x.astype(BF16), ones_bd)
    hi, lo = _split2(x)
    return _dot(hi, ones_bd) + _dot(lo, ones_bd)


def _rms(x, gain):
    return x * lax.rsqrt(jnp.mean(x * x, axis=-1, keepdims=True) + RMS_EPS) * gain


def _sigmoid(x):
    return 1.0 / (1.0 + jnp.exp(-x))


def _params(sem):
    return pltpu.CompilerParams(dimension_semantics=sem, vmem_limit_bytes=VMEM_LIMIT)


def _pre0_kernel(x_ref, xp_ref, xn_ref, gain_ref, w_ref, mu_ref, wup_ref, par_ref, qg_ref, kg_ref,
                 cos_ref, sin_ref, ones_ref,
                 r_o, k_o, v_o, a_o, b_o, lf_o, lb_o, g_o, bon_o, q_o, ka_o, va_o,
                 *, tiles_per_seq, tm):
    pos = pl.program_id(0) % tiles_per_seq
    gain = gain_ref[...]
    xp = jnp.where(pos == 0, 0.0, xp_ref[...])
    xn = jnp.where(pos == tiles_per_seq - 1, 0.0, xn_ref[...])
    x_all = jnp.concatenate([xp, x_ref[...], xn], axis=0)
    H_all = _dot(_rms(x_all, gain).astype(BF16), w_ref[...])
    H = H_all[8:8 + tm]
    Hr = H[:, :RW_COLS]
    Hr_all = H_all[:, :RW_COLS]
    prev = pltpu.roll(Hr_all, 1, 0)[8:8 + tm]
    nxt = pltpu.roll(Hr_all, tm + 15, 0)[8:8 + tm]
    Hs = Hr + mu_ref[...] * (0.5 * (prev + nxt) - Hr)

    ones = ones_ref[...]
    par = par_ref[...]
    r = Hs[:, 0:512]
    k = Hs[:, 512:1024]
    v = Hs[:, 1024:1536]
    up_w = _dot(jnp.tanh(Hs[:, 1536:1664]).astype(BF16), wup_ref[0:128, 0:1024])
    up_a = _dot(Hs[:, 1664:1792].astype(BF16), wup_ref[128:256, 1024:1536])
    lf_o[...] = -EXP_M05 * _sigmoid(par[0:1] + up_w[:, 0:512])
    lb_o[...] = -EXP_M05 * _sigmoid(par[1:2] + up_w[:, 512:1024])
    a_sig = _sigmoid(par[2:3] + up_a)
    g_o[...] = _dot(_sigmoid(Hs[:, 1792:1920]).astype(BF16), wup_ref[256:384, 1536:2048])
    kk = k * par[3:4]
    kk = kk / jnp.maximum(jnp.sqrt(_seg_sum(kk * kk, ones, split=False)), 1e-12)
    k2 = k * (1.0 + (a_sig - 1.0) * par[4:5])
    r_o[...] = r
    k_o[...] = k2
    v_o[...] = v
    a_o[...] = -kk
    b_o[...] = kk * a_sig
    bon_o[...] = _seg_sum(r * k2 * par[5:6], ones) * v

    qa = H[:, RW_COLS:RW_COLS + 512]
    ka = H[:, RW_COLS + 512:RW_COLS + 640]
    va = H[:, RW_COLS + 640:RW_COLS + 768]
    cos = cos_ref[...]
    sin = sin_ref[...]

    def rope(x, c, s):
        n = x.shape[1]
        lane = lax.broadcasted_iota(jnp.int32, x.shape, 1)
        swapped = jnp.where(lane % 2 == 0, pltpu.roll(x, n - 1, 1), pltpu.roll(x, 1, 1))
        return x * c + swapped * s

    qn = qa * lax.rsqrt(_seg_sum(qa * qa, ones, split=False) * (1.0 / HEAD) + RMS_EPS) * qg_ref[...]
    qr = rope(qn, jnp.concatenate([cos] * 4, axis=1), jnp.concatenate([sin] * 4, axis=1))
    q_o[...] = (qr * (HEAD ** -0.5 * LOG2_E)).astype(BF16)
    kn = ka * lax.rsqrt(_seg_sum(ka * ka, ones[:128, :128], split=False) * (1.0 / HEAD) + RMS_EPS) * kg_ref[...]
    kr = rope(kn, cos, sin)
    lt64 = lax.broadcasted_iota(jnp.int32, kr.shape, 1) < HEAD

    def rep(x):
        sw = pltpu.roll(x, HEAD, 1)
        return jnp.concatenate([jnp.where(lt64, x, sw), jnp.where(lt64, sw, x)], axis=1).astype(BF16)

    ka_o[...] = rep(kr)
    vt = va.T.astype(BF16)
    one = jnp.ones((V_ROWS - HEAD, tm), BF16)
    va_o[0] = jnp.concatenate([vt[0:HEAD], one, vt[HEAD:2 * HEAD], one], axis=0)


def _pre0(x2d, T, gain, w_all, mu_all, wup, par, qg, kg, cos_t, sin_t, ones_bd):
    N = x2d.shape[0]
    tm = min(256, T)
    tps = T // tm
    nt = N // tm
    t8 = tm // 8
    nb8 = N // 8
    const = lambda i: (0, 0)
    tile = lambda i: (i, 0)
    f512 = jax.ShapeDtypeStruct((N, 512), F32)
    out_shape = [f512] * 9 + [jax.ShapeDtypeStruct((N, 512), BF16), jax.ShapeDtypeStruct((N, 256), BF16),
                              jax.ShapeDtypeStruct((N // T, 2 * V_ROWS, T), BF16)]
    out_specs = [pl.BlockSpec((tm, 512), tile)] * 10 + [
        pl.BlockSpec((tm, 256), tile), pl.BlockSpec((1, 2 * V_ROWS, tm), lambda i: (i // tps, 0, i % tps))]
    return pl.pallas_call(
        functools.partial(_pre0_kernel, tiles_per_seq=tps, tm=tm),
        grid=(nt,),
        in_specs=[
            pl.BlockSpec((tm, D_MODEL), tile),
            pl.BlockSpec((8, D_MODEL), lambda i: (jnp.maximum(i * t8 - 1, 0), 0)),
            pl.BlockSpec((8, D_MODEL), lambda i: (jnp.minimum((i + 1) * t8, nb8 - 1), 0)),
            pl.BlockSpec((1, D_MODEL), const),
            pl.BlockSpec((D_MODEL, ALL_COLS), const),
            pl.BlockSpec((1, RW_COLS), const),
            pl.BlockSpec((384, 2048), const),
            pl.BlockSpec((8, 512), const),
            pl.BlockSpec((1, 512), const),
            pl.BlockSpec((1, 128), const),
            pl.BlockSpec((tm, 128), lambda i: (i % tps, 0)),
            pl.BlockSpec((tm, 128), lambda i: (i % tps, 0)),
            pl.BlockSpec((512, 512), const),
        ],
        out_specs=out_specs,
        out_shape=out_shape,
        compiler_params=_params(("parallel",)),
        name="pre0",
    )(x2d, x2d, x2d, gain, w_all, mu_all, wup, par, qg, kg, cos_t, sin_t, ones_bd)


def _wkv_direction(r, k, v, a, b, L, h_ref, fwd):
    C = WKV_CHUNK
    Q = 4 * HEAD
    row = lax.broadcasted_iota(jnp.int32, L.shape, 0)
    cs = L
    for lvl in range(int(math.log2(C))):
        cs = cs + jnp.where(row >= 2 ** lvl, pltpu.roll(cs, 2 ** lvl, 0), 0.0)
    total = cs[C - 1:C, :]
    if fwd:
        cs_incl = cs
        cs_excl = cs - L
    else:
        cs_incl = total - (cs - L)
        cs_excl = total - cs
    e_incl = jnp.exp(cs_incl)
    e_inv = jnp.exp(-cs_incl)
    e_rem = jnp.exp(total - cs_incl)
    a_t = a * jnp.exp(cs_excl)
    r_t = r * e_incl
    b_t = b * e_inv
    k_t = k * e_inv
    b_h = b * e_rem
    k_h = k * e_rem
    gam = jnp.exp(total)

    streams = []
    for q in range(RWKV_DIM // Q):
        sl = slice(q * Q, (q + 1) * Q)
        streams.append(dict(
            q=q, h_ref=h_ref, fwd=fwd, gam=gam[:, sl], a_t=a_t[:, sl], r_t=r_t[:, sl], b_t=b_t[:, sl],
            k_t=k_t[:, sl], v=v[:, sl], b_h=b_h[:, sl], k_h=k_h[:, sl]))
    return streams


def _wkv_solve(streams):
    C = WKV_CHUNK
    Q = 4 * HEAD
    lane_q = lax.broadcasted_iota(jnp.int32, (C, Q), 1) // HEAD
    ri = lax.broadcasted_iota(jnp.int32, (Q, Q), 0)
    ci = lax.broadcasted_iota(jnp.int32, (Q, Q), 1)
    same = (ri // C) == (ci // C)
    tw = lax.broadcasted_iota(jnp.int32, (C, Q), 0)
    sw = lax.broadcasted_iota(jnp.int32, (C, Q), 1) % C
    masks = {True: (sw < tw, sw <= tw), False: (sw > tw, sw >= tw)}
    eye_w = sw == tw

    def stack(xq):
        return jnp.concatenate([jnp.where(lane_q == h, xq, 0.0) for h in range(4)], axis=0).astype(BF16)

    def blockdiag(w):
        return jnp.where(same, jnp.concatenate([w] * 4, axis=0), 0.0).astype(BF16)

    for s in streams:
        strict, incl = masks[s['fwd']]
        P = _dot_nt(jnp.concatenate([s['a_t'], s['r_t']], axis=0).astype(BF16),
                    jnp.concatenate([stack(s['b_t']), stack(s['k_t'])], axis=0))
        A_ab = jnp.where(strict, P[:C, :Q], 0.0)
        s['A_k'] = jnp.concatenate([jnp.where(strict, P[:C, Q:], 0.0), jnp.where(incl, P[C:, Q:], 0.0)],
                                   axis=0).astype(BF16)
        s['A_rb'] = jnp.where(incl, P[C:, :Q], 0.0).astype(BF16)
        s['T'] = jnp.where(eye_w, 1.0, A_ab)
        s['Ap'] = _dot(A_ab.astype(BF16), blockdiag(A_ab))
        s['Vs'] = stack(s['v'])
    levels = int(math.log2(C))
    for lvl in range(1, levels):
        for s in streams:
            rhs = blockdiag(s['Ap'])
            if lvl < levels - 1:
                both = _dot(jnp.concatenate([s['T'], s['Ap']], axis=0).astype(BF16), rhs)
                s['T'] = s['T'] + both[:C]
                s['Ap'] = both[C:]
            else:
                s['T'] = s['T'] + _dot(s['T'].astype(BF16), rhs)
    for s in streams:
        zy = _dot(s['A_k'], s['Vs'])
        s['Z'] = zy[:C]
        s['y_k'] = zy[C:]
        s['n_k'] = _dot_tn(s['k_h'].astype(BF16), s['v'].astype(BF16))
    for s in streams:
        s['X'] = _dot(s['T'].astype(BF16), jnp.concatenate([stack(s['a_t']), stack(s['Z'])], axis=1))
    for s in streams:
        X = s['X']
        W1 = _dot(s['A_rb'], jnp.concatenate([stack(X[:, :Q]), stack(X[:, Q:])], axis=1))
        MN = _dot_tn(s['b_h'].astype(BF16), X.astype(BF16))
        s['r_p'] = s['r_t'] + W1[:, :Q]
        s['y_p'] = W1[:, Q:] + s['y_k']
        s['M'] = jnp.where(same, MN[:, :Q], 0.0)
        s['N'] = jnp.where(same, MN[:, Q:] + s['n_k'], 0.0)
    ys = []
    for s in streams:
        h0 = s['h_ref'][s['q']]
        gam_col = jnp.sum(jnp.where(ri == ci, s['gam'], 0.0), axis=1, keepdims=True)
        seq = _dot(jnp.concatenate([s['r_p'], s['M']], axis=0).astype(BF16), h0.astype(BF16))
        ys.append(seq[:C] + s['y_p'])
        s['h_ref'][s['q']] = gam_col * h0 + seq[C:] + s['N']
    return ys


def _wkv_kernel(rf, kf, vf, af, bf, lf, rb, kb, vb, ab, bb, lb, yf_o, yb_o, hf_ref, hb_ref):
    @pl.when(pl.program_id(1) == 0)
    def _():
        hf_ref[...] = jnp.zeros_like(hf_ref)
        hb_ref[...] = jnp.zeros_like(hb_ref)

    C = WKV_CHUNK
    streams = []
    for refs, h_ref, fwd in (((rf, kf, vf, af, bf, lf), hf_ref, True), ((rb, kb, vb, ab, bb, lb), hb_ref, False)):
        for n in (range(WKV_STEP) if fwd else range(WKV_STEP - 1, -1, -1)):
            streams += _wkv_direction(*(x[n * C:(n + 1) * C, :] for x in refs), h_ref, fwd)
    ys = _wkv_solve(streams)
    for n in range(WKV_STEP):
        f0 = 2 * n
        b0 = 2 * WKV_STEP + 2 * (WKV_STEP - 1 - n)
        yf_o[n * C:(n + 1) * C, :] = jnp.concatenate(ys[f0:f0 + 2], axis=1)
        yb_o[n * C:(n + 1) * C, :] = jnp.concatenate(ys[b0:b0 + 2], axis=1)


def _wkv(r, k, v, a, b, lf, lb, n_seq, T):
    C = WKV_CHUNK * WKV_STEP
    nc = T // C
    fw = lambda s, i: (s * nc + i, 0)
    bw = lambda s, i: (s * nc + nc - 1 - i, 0)
    spec_f = pl.BlockSpec((C, 512), fw)
    spec_b = pl.BlockSpec((C, 512), bw)
    shp = jax.ShapeDtypeStruct(r.shape, F32)
    return pl.pallas_call(
        _wkv_kernel,
        grid=(n_seq, nc),
        in_specs=[spec_f] * 6 + [spec_b] * 6,
        out_specs=[spec_f, spec_b],
        out_shape=[shp, shp],
        scratch_shapes=[pltpu.VMEM((2, 256, 256), F32), pltpu.VMEM((2, 256, 256), F32)],
        compiler_params=_params(("parallel", "arbitrary")),
        name="wkv",
    )(r, k, v, a, b, lf, r, k, v, a, b, lb)


def _attn_kernel(q_ref, k_ref, vt_ref, o_ref, s0_ref, s1_ref, p0_ref, p1_ref, acc_ref, *, tq, tk, T):
    q = q_ref[...].astype(F32)
    lo = lax.broadcasted_iota(jnp.int32, (tq, 128), 1) < HEAD
    qs = jnp.concatenate([
        jnp.where(lo, q[:, 0:128], 0.0), jnp.where(lo, 0.0, q[:, 0:128]),
        jnp.where(lo, q[:, 128:256], 0.0), jnp.where(lo, 0.0, q[:, 128:256])], axis=0)
    qst = qs.T.astype(BF16)
    R = 4 * tq
    n = T // tk

    def chunk(j):
        return pl.ds(j * tk if isinstance(j, int) else pl.multiple_of(j * tk, tk), tk)

    def scores(j):
        return _dot(k_ref[chunk(j), :], qst)

    def values(j, p):
        return _dot(vt_ref[0, :, chunk(j)], p)

    s_buf = (s0_ref, s1_ref)
    p_buf = (p0_ref, p1_ref)
    s_buf[0][...] = scores(0)
    p_buf[1][...] = jnp.zeros((tk, R), BF16)
    acc_ref[...] = jnp.zeros((V_ROWS, R), F32)

    def step(j, b, carry, ahead=True):
        m, alpha_prev = carry
        if ahead:
            s_buf[1 - b][...] = scores(j + 1)
        prev = max(j - 1, 0) if isinstance(j, int) else jnp.maximum(j - 1, 0)
        acc_ref[...] = alpha_prev * acc_ref[...] + values(prev, p_buf[1 - b][...])
        s = s_buf[b][...]
        m_new = jnp.maximum(m, jnp.max(s, axis=0, keepdims=True))
        p_buf[b][...] = jnp.exp2(s - m_new).astype(BF16)
        return m_new, jnp.exp2(m - m_new)

    def body(i, carry):
        return step(2 * i + 1, 1, step(2 * i, 0, carry))

    carry = lax.fori_loop(0, n // 2 - 1, body, (jnp.full((1, R), -1e30, F32), jnp.ones((1, R), F32)))
    carry = step(n - 2, 0, carry)
    _, alpha = step(n - 1, 1, carry, ahead=False)
    acc = alpha * acc_ref[...] + values(n - 1, p_buf[1][...])
    ot = acc[0:HEAD] / acc[HEAD:HEAD + 1]
    o01 = jnp.concatenate([ot[:, 0:tq], ot[:, tq:2 * tq]], axis=0).T
    o23 = jnp.concatenate([ot[:, 2 * tq:3 * tq], ot[:, 3 * tq:4 * tq]], axis=0).T
    o_ref[...] = jnp.concatenate([o01, o23], axis=1)


def _attention(q, k_rep, v_t, n_seq, T):
    tq = min(1024, T)
    tk = min(512, T // 2)
    nq = T // tq
    assert (T // tk) % 2 == 0
    return pl.pallas_call(
        functools.partial(_attn_kernel, tq=tq, tk=tk, T=T),
        grid=(n_seq, N_KV, nq),
        in_specs=[
            pl.BlockSpec((tq, 256), lambda s, h, i: (s * nq + i, h)),
            pl.BlockSpec((T, 128), lambda s, h, i: (s, h)),
            pl.BlockSpec((1, V_ROWS, T), lambda s, h, i: (s, h, 0)),
        ],
        out_specs=pl.BlockSpec((tq, 256), lambda s, h, i: (s * nq + i, h)),
        out_shape=jax.ShapeDtypeStruct(q.shape, F32),
        scratch_shapes=[pltpu.VMEM((tk, 4 * tq), F32)] * 2 + [pltpu.VMEM((tk, 4 * tq), BF16)] * 2 + [
            pltpu.VMEM((V_ROWS, 4 * tq), F32)],
        compiler_params=_params(("parallel", "parallel", "arbitrary")),
        name="attention",
    )(q, k_rep, v_t)


MLP_TILE = 512
MLP_FF_CHUNK = 1024


def _mlp_step(hn_ref, wu_ref, wd_ref, o_ref):
    u = jnp.maximum(_dot(hn_ref[...], wu_ref[...]), 0.0)
    o_ref[...] += _dot((u * u).astype(BF16), wd_ref[...])


def _post0_mlp_kernel(x_ref, yf_ref, yb_ref, g_ref, bon_ref, att_ref, ln_ref, wo_ref, ones_ref, gain_ref,
                      gain_next_ref, wu_ref, wd_ref, o_ref, hn_next_ref, hn_ref):
    @pl.when(pl.program_id(1) == 0)
    def _():
        ones = ones_ref[...]
        y = yf_ref[...] + yb_ref[...]
        mean = _seg_sum(y, ones) * (1.0 / HEAD)
        d = y - mean
        var = _seg_sum(d * d, ones) * (1.0 / HEAD)
        yn = d * lax.rsqrt(var + GN_EPS) * ln_ref[0:1] + ln_ref[1:2]
        ya = ((yn + bon_ref[...]) * g_ref[...]).astype(BF16)
        mix = _dot(ya, wo_ref[0:512, :]) + _dot(att_ref[...].astype(BF16), wo_ref[512:1024, :])
        x1 = x_ref[...] + mix
        o_ref[...] = x1
        hn_ref[...] = _rms(x1, gain_ref[...]).astype(BF16)

    _mlp_step(hn_ref, wu_ref, wd_ref, o_ref)

    @pl.when(pl.program_id(1) == pl.num_programs(1) - 1)
    def _():
        hn_next_ref[...] = _rms(o_ref[...], gain_next_ref[...])


def _mlp_specs(tm):
    row = lambda i, j: (i, 0)
    const = lambda i, j: (0, 0)
    weights = [pl.BlockSpec((D_MODEL, MLP_FF_CHUNK), lambda i, j: (0, j)),
               pl.BlockSpec((MLP_FF_CHUNK, D_MODEL), lambda i, j: (j, 0))]
    return row, const, weights


def _post0_mlp(x2d, yf, yb, g, bon, att, ln, wo, ones_bd, gain, gain_next, w_up, w_down):
    N = x2d.shape[0]
    tm = min(MLP_TILE, N)
    row, const, weights = _mlp_specs(tm)
    wide = pl.BlockSpec((tm, D_MODEL), row)
    vec = pl.BlockSpec((1, D_MODEL), const)
    shp = jax.ShapeDtypeStruct(x2d.shape, F32)
    return pl.pallas_call(
        _post0_mlp_kernel,
        grid=(N // tm, D_FF // MLP_FF_CHUNK),
        in_specs=[wide] + [pl.BlockSpec((tm, 512), row)] * 5 + [
            pl.BlockSpec((8, 512), const), pl.BlockSpec((D_MODEL, D_MODEL), const), pl.BlockSpec((512, 512), const),
            vec, vec] + weights,
        out_specs=[wide, wide],
        out_shape=[shp, shp],
        scratch_shapes=[pltpu.VMEM((tm, D_MODEL), BF16)],
        compiler_params=_params(("parallel", "arbitrary")),
        name="post0_mlp",
    )(x2d, yf, yb, g, bon, att, ln, wo, ones_bd, gain, gain_next, w_up, w_down)


def _cmul_add(ar, ai, br, bi, cr, ci):
    return ar * br - ai * bi + cr, ar * bi + ai * br + ci


S5_SEGMENTS = 8


def _s5_scan(zs_ref, ps_ref, lam, pw_ref, fwd, nc):
    nl = zs_ref.shape[1]
    W = nl * 128
    ls = nc // S5_SEGMENTS
    ns = S5_SEGMENTS

    def gather(c, k):
        return jnp.concatenate([zs_ref[c, l, pl.ds(k, ns, stride=ls), :] for l in range(nl)], axis=1)

    def load(c, k):
        return jnp.concatenate([ps_ref[c, l, k * ns:(k + 1) * ns, :] for l in range(nl)], axis=1)

    def store(c, k, v):
        for l in range(nl):
            ps_ref[c, l, k * ns:(k + 1) * ns, :] = v[:, l * 128:(l + 1) * 128]

    xr = xi = jnp.zeros((ns, W), F32)
    for k in (range(ls) if fwd else range(ls - 1, -1, -1)):
        store(0, k, xr)
        store(1, k, xi)
        xr, xi = _cmul_add(lam[0:1], lam[1:2], xr, xi, gather(0, k), gather(1, k))
    cr = ci = jnp.zeros((1, W), F32)
    rows_r = [None] * S5_SEGMENTS
    rows_i = [None] * S5_SEGMENTS
    for s in (range(S5_SEGMENTS) if fwd else range(S5_SEGMENTS - 1, -1, -1)):
        rows_r[s] = cr
        rows_i[s] = ci
        cr, ci = _cmul_add(lam[2:3], lam[3:4], cr, ci, xr[s:s + 1], xi[s:s + 1])
    car_r = jnp.concatenate(rows_r, axis=0)
    car_i = jnp.concatenate(rows_i, axis=0)
    for k in range(ls):
        pr, pi = _cmul_add(pw_ref[0, k:k + 1, :], pw_ref[1, k:k + 1, :], car_r, car_i, load(0, k), load(1, k))
        store(0, k, pr)
        store(1, k, pi)


def _s5_kernel(x_ref, g2_ref, wz_ref, w2_ref, lam_ref, pw_ref, y_ref, lhs_ref, zs_ref, ps_ref, *, nc):
    C = S5_CHUNK
    W = S5_SLAB * S5_STATE
    nl = W // 128
    for j in range(C):
        lhs_ref[:, j * 128:(j + 1) * 128] = x_ref[pl.ds(j, nc, stride=C), :].astype(BF16)
    lhs = lhs_ref[...]
    for d in range(2):
        z = _dot(lhs, wz_ref[0, :, d * 2 * W:(d + 1) * 2 * W])
        for c in range(2):
            for l in range(nl):
                zs_ref[d, c, l] = z[:, c * W + l * 128:c * W + (l + 1) * 128]
        _s5_scan(zs_ref.at[d], ps_ref.at[d], lam_ref[0, d], pw_ref.at[0, d], d == 0, nc)
    ls = nc // S5_SEGMENTS

    def chunk_order(ref):
        return jnp.concatenate([ref[pl.ds(s, ls, stride=S5_SEGMENTS), :] for s in range(S5_SEGMENTS)], axis=0)

    pv = jnp.concatenate([chunk_order(ps_ref.at[d, c, l]) for d in range(2) for c in range(2) for l in range(nl)],
                         axis=1).astype(BF16)
    for i in range(0, C, 2):
        w_loc = g2_ref[0, (C - 1 - i) * 128:(2 * C - 1 - i) * 128, :]
        y2 = _dot(lhs, w_loc) + _dot(pv, w2_ref[0, :, i * 128:(i + 2) * 128])
        y_ref[pl.ds(i, nc, stride=C), :] = y2[:, :128]
        y_ref[pl.ds(i + 1, nc, stride=C), :] = y2[:, 128:]


def _s5_core(hn, g2, wz, w2, lam, pw, n_seq, T):
    nc = T // S5_CHUNK
    ls = nc // S5_SEGMENTS
    W = S5_SLAB * S5_STATE
    n_slab = D_MODEL // 128
    once = pl.Buffered(1)
    slab = lambda c, s: (c, 0, 0)
    return pl.pallas_call(
        functools.partial(_s5_kernel, nc=nc),
        grid=(n_slab, n_seq),
        in_specs=[
            pl.BlockSpec((T, 128), lambda c, s: (s, c)),
            pl.BlockSpec((1, 2 * S5_CHUNK * 128, 256), slab, pipeline_mode=once),
            pl.BlockSpec((1, S5_CHUNK * 128, 4 * W), slab, pipeline_mode=once),
            pl.BlockSpec((1, 4 * W, S5_CHUNK * 128), slab, pipeline_mode=once),
            pl.BlockSpec((1, 2, 8, W), lambda c, s: (c, 0, 0, 0)),
            pl.BlockSpec((1, 2, 2, ls, W), lambda c, s: (c, 0, 0, 0, 0)),
        ],
        out_specs=pl.BlockSpec((T, 128), lambda c, s: (s, c)),
        out_shape=jax.ShapeDtypeStruct(hn.shape, F32),
        scratch_shapes=[
            pltpu.VMEM((nc, S5_CHUNK * 128), BF16),
            pltpu.VMEM((2, 2, W // 128, nc, 128), F32), pltpu.VMEM((2, 2, W // 128, nc, 128), F32)],
        compiler_params=_params(("parallel", "arbitrary")),
        name="s5_core",
    )(hn, g2, wz, w2, lam, pw)


def _post1_mlp_kernel(x_ref, ys_ref, gain_ref, d_ref, wg_ref, bg_ref, gain2_ref, wu_ref, wd_ref, o_ref, hn_ref):
    @pl.when(pl.program_id(1) == 0)
    def _():
        x = x_ref[...]
        y = _rms(x, gain_ref[...]) * d_ref[...] + ys_ref[...]
        z = 0.5 * y * (1.0 + jnp.tanh(math.sqrt(2.0 / math.pi) * (y + 0.044715 * (y * y * y))))
        gate = _sigmoid(_dot(z.astype(BF16), wg_ref[...]) + bg_ref[...])
        x1 = x + z * gate
        o_ref[...] = x1
        hn_ref[...] = _rms(x1, gain2_ref[...]).astype(BF16)

    _mlp_step(hn_ref, wu_ref, wd_ref, o_ref)


def _post1_mlp(x2d, ys, gain, d, wg, bg, gain2, w_up, w_down):
    N = x2d.shape[0]
    tm = min(MLP_TILE, N)
    row, const, weights = _mlp_specs(tm)
    wide = pl.BlockSpec((tm, D_MODEL), row)
    vec = pl.BlockSpec((1, D_MODEL), const)
    return pl.pallas_call(
        _post1_mlp_kernel,
        grid=(N // tm, D_FF // MLP_FF_CHUNK),
        in_specs=[wide, wide, vec, vec, pl.BlockSpec((D_MODEL, D_MODEL), const), vec, vec] + weights,
        out_specs=wide,
        out_shape=jax.ShapeDtypeStruct(x2d.shape, F32),
        scratch_shapes=[pltpu.VMEM((tm, D_MODEL), BF16)],
        compiler_params=_params(("parallel", "arbitrary")),
        name="post1_mlp",
    )(x2d, ys, gain, d, wg, bg, gain2, w_up, w_down)


def _s5_tables(p, T):
    C, G, P, SL = S5_CHUNK, S5_GROUPS, S5_STATE, S5_SLAB
    ns = G // SL
    ls = T // C // S5_SEGMENTS
    eye = jnp.eye(SL, dtype=F32)
    b_re = p['s5_b_re'][0].astype(F32)
    b_im = p['s5_b_im'][0].astype(F32)
    steps = jnp.arange(C, dtype=F32)

    def direction(sfx):
        lr = p['s5_lam_re_' + sfx][0].astype(F32)
        li = p['s5_lam_im_' + sfx][0].astype(F32)
        dt = jnp.exp(p['s5_log_dt_' + sfx][0].astype(F32))[:, None]

        def power(k):
            k = k[:, None, None]
            mag = jnp.exp(lr * dt * k)
            return mag * jnp.cos(li * dt * k), mag * jnp.sin(li * dt * k)

        l1r, l1i = power(jnp.ones((1,), F32))
        nr, ni = l1r[0] - 1.0, l1i[0]
        den = lr * lr + li * li
        cr = (nr * lr + ni * li) / den
        ci = (ni * lr - nr * li) / den
        cb_r = cr[:, :, None] * b_re - ci[:, :, None] * b_im
        cb_i = cr[:, :, None] * b_im + ci[:, :, None] * b_re
        c_r = p['s5_c_re_' + sfx][0].astype(F32)
        c_i = p['s5_c_im_' + sfx][0].astype(F32)
        return power, cb_r, cb_i, c_r, c_i

    def kernels(power, cb_r, cb_i, c_r, c_i):
        pr, pi = power(steps)
        d_r = pr[..., None] * cb_r - pi[..., None] * cb_i
        d_i = pr[..., None] * cb_i + pi[..., None] * cb_r
        k = jnp.einsum('gop,lgpi->lgio', c_r, d_r) - jnp.einsum('gop,lgpi->lgio', c_i, d_i)
        k = k.reshape(C, ns, SL, S5_GROUP, 1, S5_GROUP)
        return jnp.where(eye[:, None, :, None] > 0, k, 0.0).reshape(C, ns, 128, 128)

    def state_in(power, cb_r, cb_i, ks):
        pr, pi = power(ks)
        w_r = (pr[..., None] * cb_r - pi[..., None] * cb_i).reshape(C, ns, SL, P, S5_GROUP)
        w_i = (pr[..., None] * cb_i + pi[..., None] * cb_r).reshape(C, ns, SL, P, S5_GROUP)
        return [w_r, w_i]

    def state_out(power, c_r, c_i, ks):
        pr, pi = power(ks)
        e_r = (c_r[None] * pr[:, :, None, :] - c_i[None] * pi[:, :, None, :]).reshape(C, ns, SL, S5_GROUP, P)
        e_i = (c_r[None] * pi[:, :, None, :] + c_i[None] * pr[:, :, None, :]).reshape(C, ns, SL, S5_GROUP, P)
        return [e_r, -e_i]

    def scan_tables(power, ks_rows):
        lr_, li_ = power(C * jnp.array([1.0, ls], F32))
        lam = jnp.stack([lr_, li_], axis=1).reshape(4, ns, SL * P)
        lam = jnp.concatenate([lam, jnp.zeros_like(lam)], axis=0).transpose(1, 0, 2)
        pr, pi = power(C * ks_rows)
        pw = jnp.stack([pr.reshape(ls, ns, SL * P), pi.reshape(ls, ns, SL * P)], axis=0).transpose(2, 0, 1, 3)
        return lam, pw

    def expand(x, row_group, col_group):
        src = jnp.arange(2048) // (col_group * SL) * col_group + jnp.arange(2048) % col_group
        spread = (jnp.arange(256)[:, None] == src[None, :]).astype(BF16)
        full = jnp.einsum('srk,kc->src', x.astype(BF16), spread, preferred_element_type=F32)
        same = ((jnp.arange(2048) // row_group) % SL)[:, None] == ((jnp.arange(2048) // col_group) % SL)[None, :]
        return jnp.where(same, full, 0.0).astype(BF16)

    pf = direction('f')
    pb = direction('b')
    kf = kernels(*pf)
    kb = kernels(*pb)
    zero = jnp.zeros((1, ns, 128, 128), F32)
    gen = jnp.concatenate([kf[:0:-1], (kf[0] + kb[0])[None], kb[1:], zero], axis=0)
    gen_prev = jnp.concatenate([zero, gen[:-1]], axis=0)
    g2 = jnp.concatenate([gen, gen_prev], axis=-1).transpose(1, 0, 2, 3).reshape(ns, 2 * C * 128, 256)

    w4 = jnp.stack(state_in(pf[0], pf[1], pf[2], (C - 1) - steps) + state_in(pb[0], pb[1], pb[2], steps))
    wz = expand(w4.transpose(2, 1, 3, 5, 0, 4).reshape(ns, C * 128, 4 * P), S5_GROUP, P)
    e4 = jnp.stack(state_out(pf[0], pf[3], pf[4], steps + 1.0) + state_out(pb[0], pb[3], pb[4], C - steps))
    w2 = expand(e4.transpose(2, 0, 3, 5, 1, 4).reshape(ns, 4 * SL * P, C * S5_GROUP), P, S5_GROUP)
    pos = jnp.arange(ls, dtype=F32)
    lam_f, pw_f = scan_tables(pf[0], pos)
    lam_b, pw_b = scan_tables(pb[0], (ls - 1.0) - pos)
    lam = jnp.stack([lam_f, lam_b], axis=1)
    pw = jnp.stack([pw_f, pw_b], axis=1)
    return g2.astype(BF16), wz, w2, lam, pw


def _rope_tables(T):
    rows = T // GRID_W
    row_ids = jnp.repeat(jnp.arange(rows, dtype=F32), GRID_W)
    col_ids = jnp.tile(jnp.arange(GRID_W, dtype=F32), rows)
    pairs = HEAD // 4
    inv_freq = ROPE_THETA ** (-jnp.arange(pairs, dtype=F32) / pairs)
    ang = jnp.concatenate([row_ids[:, None] * inv_freq, col_ids[:, None] * inv_freq], axis=-1)
    cos = jnp.repeat(jnp.cos(ang), 2, axis=-1)
    sin = jnp.repeat(jnp.sin(ang), 2, axis=-1)
    sign = jnp.tile(jnp.array([-1.0, 1.0], F32), HEAD // 2)
    return jnp.tile(cos, (1, 2)), jnp.tile(sin * sign, (1, 2))


def _layer0_weights(p, T):
    w_in = p['hyb_w_in'][0]
    zc = jnp.zeros((D_MODEL, 64), F32)
    w_all = jnp.concatenate([w_in[:, 0:1728], zc, w_in[:, 1728:2624]], axis=1).astype(BF16)
    mu = p['hyb_shift_mu'][0]
    mu_all = jnp.concatenate([mu[0:1728], jnp.zeros((64,), F32), mu[1728:1856]]).reshape(1, RW_COLS)
    wup = jnp.zeros((384, 2048), F32)
    wup = wup.at[0:64, 0:512].set(p['rwkv_w_up_f'][0])
    wup = wup.at[64:128, 512:1024].set(p['rwkv_w_up_b'][0])
    wup = wup.at[128:192, 1024:1536].set(p['rwkv_a_up'][0])
    wup = wup.at[256:384, 1536:2048].set(p['rwkv_g_up'][0])
    wup = wup.astype(BF16)
    zr = jnp.zeros((512,), F32)
    par = jnp.stack([p['rwkv_w0_f'][0], p['rwkv_w0_b'][0], p['rwkv_a0'][0], p['rwkv_k_k'][0],
                     p['rwkv_k_a'][0], p['rwkv_r_k'][0].reshape(-1), zr, zr]).astype(F32)
    qg = jnp.tile(p['att_q_norm'][0], 8).reshape(1, 512).astype(F32)
    kg = jnp.tile(p['att_k_norm'][0], 2).reshape(1, 128).astype(F32)
    cos_t, sin_t = _rope_tables(T)
    seg = jnp.arange(512) // HEAD
    ones_bd = (seg[:, None] == seg[None, :]).astype(BF16)
    ln = jnp.stack([p['rwkv_lnx_g'][0], p['rwkv_lnx_b'][0]] + [zr] * 6).astype(F32)
    return dict(w_all=w_all, mu_all=mu_all, wup=wup, par=par, qg=qg, kg=kg, cos_t=cos_t, sin_t=sin_t,
                ones_bd=ones_bd, ln=ln, wo=p['hyb_w_out'][0].astype(BF16))


def _row(v):
    return v.reshape(1, -1).astype(F32)


def _layer0(x2d, p, w, ffn_w, n_seq, T):
    r, k, v, a, b, lf, lb, g, bon, q, k_rep, v_t = _pre0(
        x2d, T, _row(p['mix_norm'][0]), w['w_all'], w['mu_all'], w['wup'], w['par'], w['qg'], w['kg'],
        w['cos_t'], w['sin_t'], w['ones_bd'])
    yf, yb = _wkv(r, k, v, a, b, lf, lb, n_seq, T)
    att = _attention(q, k_rep, v_t, n_seq, T)
    return _post0_mlp(x2d, yf, yb, g, bon, att, w['ln'], w['wo'], w['ones_bd'], _row(p['ffn_norm'][0]),
                      _row(p['mix_norm'][1]), ffn_w[0][0], ffn_w[0][1])


def _layer1(x2d, hn, p, s5, ffn_w, n_seq, T):
    g2, wz, w2, lam, pw = s5
    ys = _s5_core(hn, g2, wz, w2, lam, pw, n_seq, T)
    return _post1_mlp(x2d, ys, _row(p['mix_norm'][1]), _row(p['s5_d'][0]), p['s5_glu_w'][0].astype(BF16),
                      _row(p['s5_glu_b'][0]), _row(p['ffn_norm'][1]), ffn_w[1][0], ffn_w[1][1])


def _prepare(p, T):
    ffn_w = [(p['ffn_up'][l].astype(BF16), p['ffn_down'][l].astype(BF16)) for l in range(2)]
    return _layer0_weights(p, T), _s5_tables(p, T), ffn_w


def _trunk(x, p, prep=None):
    n_seq, T, _ = x.shape
    w0, s5, ffn_w = _prepare(p, T) if prep is None else prep
    x2d = x.reshape(n_seq * T, D_MODEL)
    x2d, hn = _layer0(x2d, p, w0, ffn_w, n_seq, T)
    x2d = _layer1(x2d, hn, p, s5, ffn_w, n_seq, T)
    return x2d.reshape(n_seq, T, D_MODEL)


def kernel(x_prompt, x_sample, mix_norm, ffn_norm, ffn_up, ffn_down, hyb_w_in, hyb_shift_mu, rwkv_w0_f, rwkv_w_up_f, rwkv_w0_b, rwkv_w_up_b, rwkv_a0, rwkv_a_up, rwkv_g_up, rwkv_k_k, rwkv_k_a, rwkv_r_k, rwkv_lnx_g, rwkv_lnx_b, att_q_norm, att_k_norm, hyb_w_out, s5_lam_re_f, s5_lam_im_f, s5_log_dt_f, s5_lam_re_b, s5_lam_im_b, s5_log_dt_b, s5_b_re, s5_b_im, s5_c_re_f, s5_c_im_f, s5_c_re_b, s5_c_im_b, s5_d, s5_glu_w, s5_glu_b):
    p = dict(mix_norm=mix_norm, ffn_norm=ffn_norm, ffn_up=ffn_up, ffn_down=ffn_down,
             hyb_w_in=hyb_w_in, hyb_shift_mu=hyb_shift_mu,
             rwkv_w0_f=rwkv_w0_f, rwkv_w_up_f=rwkv_w_up_f, rwkv_w0_b=rwkv_w0_b, rwkv_w_up_b=rwkv_w_up_b,
             rwkv_a0=rwkv_a0, rwkv_a_up=rwkv_a_up, rwkv_g_up=rwkv_g_up,
             rwkv_k_k=rwkv_k_k, rwkv_k_a=rwkv_k_a, rwkv_r_k=rwkv_r_k,
             rwkv_lnx_g=rwkv_lnx_g, rwkv_lnx_b=rwkv_lnx_b,
             att_q_norm=att_q_norm, att_k_norm=att_k_norm, hyb_w_out=hyb_w_out,
             s5_lam_re_f=s5_lam_re_f, s5_lam_im_f=s5_lam_im_f, s5_log_dt_f=s5_log_dt_f,
             s5_lam_re_b=s5_lam_re_b, s5_lam_im_b=s5_lam_im_b, s5_log_dt_b=s5_log_dt_b,
             s5_b_re=s5_b_re, s5_b_im=s5_b_im,
             s5_c_re_f=s5_c_re_f, s5_c_im_f=s5_c_im_f, s5_c_re_b=s5_c_re_b, s5_c_im_b=s5_c_im_b,
             s5_d=s5_d, s5_glu_w=s5_glu_w, s5_glu_b=s5_glu_b)
    assert x_prompt.shape[1] == x_sample.shape[1]
    prep = _prepare(p, x_prompt.shape[1])
    return (_trunk(x_prompt, p, prep), _trunk(x_sample, p, prep))
```

```python
import functools
import math

import jax
import jax.numpy as jnp
import numpy as np
from jax import lax
from jax.experimental import pallas as pl
from jax.experimental.pallas import tpu as pltpu

F32 = jnp.float32
BF16 = jnp.bfloat16

D_MODEL = 1024
D_FF = 4 * D_MODEL
RMS_EPS = 1e-6
GRID_W = 64
RWKV_DIM = 512
HEAD = 64
GN_EPS = 64e-5
N_KV = 2
ROPE_THETA = 10000.0
S5_GROUP = 16
S5_GROUPS = D_MODEL // S5_GROUP
S5_STATE = 64
S5_SLAB = 128 // S5_GROUP

WKV_CHUNK = 64
WKV_STEP = 2
S5_CHUNK = 16
RW_COLS = 1920
ALL_COLS = 2688
EXP_M05 = math.exp(-0.5)
LOG2_E = math.log2(math.e)
V_ROWS = HEAD + 16
VMEM_LIMIT = 56 * 1024 * 1024


def _dot(a, b):
    return jnp.dot(a, b, preferred_element_type=F32)


def _dot_nt(a, b):
    return lax.dot_general(a, b, (((1,), (1,)), ((), ())), preferred_element_type=F32)


def _dot_tn(a, b):
    return lax.dot_general(a, b, (((0,), (0,)), ((), ())), preferred_element_type=F32)


def _split2(x):
    hi = x.astype(BF16)
    lo = (x - hi.astype(F32)).astype(BF16)
    return hi, lo


def _seg_sum(x, ones_bd, split=True):
    if not split:
        return _dot(x.astype(BF16), ones_bd)
    hi, lo = _split2(x)
    return _dot(hi, ones_bd) + _dot(lo, ones_bd)


def _rms(x, gain):
    return x * lax.rsqrt(jnp.mean(x * x, axis=-1, keepdims=True) + RMS_EPS) * gain


def _sigmoid(x):
    return 1.0 / (1.0 + jnp.exp(-x))


def _params(sem):
    return pltpu.CompilerParams(dimension_semantics=sem, vmem_limit_bytes=VMEM_LIMIT)


def _pre0_kernel(x_ref, xp_ref, xn_ref, gain_ref, w_ref, mu_ref, wup_ref, par_ref, qg_ref, kg_ref,
                 cos_ref, sin_ref, ones_ref,
                 r_o, k_o, v_o, a_o, b_o, lf_o, lb_o, g_o, bon_o, q_o, ka_o, va_o,
                 *, tiles_per_seq, tm):
    pos = pl.program_id(0) % tiles_per_seq
    gain = gain_ref[...]
    xp = jnp.where(pos == 0, 0.0, xp_ref[...])
    xn = jnp.where(pos == tiles_per_seq - 1, 0.0, xn_ref[...])
    x_all = jnp.concatenate([xp, x_ref[...], xn], axis=0)
    H_all = _dot(_rms(x_all, gain).astype(BF16), w_ref[...])
    H = H_all[8:8 + tm]
    Hr = H[:, :RW_COLS]
    Hr_all = H_all[:, :RW_COLS]
    prev = pltpu.roll(Hr_all, 1, 0)[8:8 + tm]
    nxt = pltpu.roll(Hr_all, tm + 15, 0)[8:8 + tm]
    Hs = Hr + mu_ref[...] * (0.5 * (prev + nxt) - Hr)

    ones = ones_ref[...]
    par = par_ref[...]
    r = Hs[:, 0:512]
    k = Hs[:, 512:1024]
    v = Hs[:, 1024:1536]
    up_w = _dot(jnp.tanh(Hs[:, 1536:1664]).astype(BF16), wup_ref[0:128, 0:1024])
    up_a = _dot(Hs[:, 1664:1792].astype(BF16), wup_ref[128:256, 1024:1536])
    lf_o[...] = -EXP_M05 * _sigmoid(par[0:1] + up_w[:, 0:512])
    lb_o[...] = -EXP_M05 * _sigmoid(par[1:2] + up_w[:, 512:1024])
    a_sig = _sigmoid(par[2:3] + up_a)
    g_o[...] = _dot(_sigmoid(Hs[:, 1792:1920]).astype(BF16), wup_ref[256:384, 1536:2048])
    kk = k * par[3:4]
    kk = kk / jnp.maximum(jnp.sqrt(_seg_sum(kk * kk, ones, split=False)), 1e-12)
    k2 = k * (1.0 + (a_sig - 1.0) * par[4:5])
    r_o[...] = r
    k_o[...] = k2
    v_o[...] = v
    a_o[...] = -kk
    b_o[...] = kk * a_sig
    bon_o[...] = _seg_sum(r * k2 * par[5:6], ones) * v

    qa = H[:, RW_COLS:RW_COLS + 512]
    ka = H[:, RW_COLS + 512:RW_COLS + 640]
    va = H[:, RW_COLS + 640:RW_COLS + 768]
    cos = cos_ref[...]
    sin = sin_ref[...]

    def rope(x, c, s):
        n = x.shape[1]
        lane = lax.broadcasted_iota(jnp.int32, x.shape, 1)
        swapped = jnp.where(lane % 2 == 0, pltpu.roll(x, n - 1, 1), pltpu.roll(x, 1, 1))
        return x * c + swapped * s

    qn = qa * lax.rsqrt(_seg_sum(qa * qa, ones, split=False) * (1.0 / HEAD) + RMS_EPS) * qg_ref[...]
    qr = rope(qn, jnp.concatenate([cos] * 4, axis=1), jnp.concatenate([sin] * 4, axis=1))
    q_o[...] = (qr * (HEAD ** -0.5 * LOG2_E)).astype(BF16)
    kn = ka * lax.rsqrt(_seg_sum(ka * ka, ones[:128, :128], split=False) * (1.0 / HEAD) + RMS_EPS) * kg_ref[...]
    kr = rope(kn, cos, sin)
    lt64 = lax.broadcasted_iota(jnp.int32, kr.shape, 1) < HEAD

    def rep(x):
        sw = pltpu.roll(x, HEAD, 1)
        return jnp.concatenate([jnp.where(lt64, x, sw), jnp.where(lt64, sw, x)], axis=1).astype(BF16)

    ka_o[...] = rep(kr)
    vt = va.T.astype(BF16)
    one = jnp.ones((V_ROWS - HEAD, tm), BF16)
    va_o[0] = jnp.concatenate([vt[0:HEAD], one, vt[HEAD:2 * HEAD], one], axis=0)


def _pre0(x2d, T, gain, w_all, mu_all, wup, par, qg, kg, cos_t, sin_t, ones_bd):
    N = x2d.shape[0]
    tm = min(512, T)
    tps = T // tm
    nt = N // tm
    t8 = tm // 8
    nb8 = N // 8
    const = lambda i: (0, 0)
    tile = lambda i: (i, 0)
    f512 = jax.ShapeDtypeStruct((N, 512), F32)
    out_shape = [f512] * 9 + [jax.ShapeDtypeStruct((N, 512), BF16), jax.ShapeDtypeStruct((N, 256), BF16),
                              jax.ShapeDtypeStruct((N // T, 2 * V_ROWS, T), BF16)]
    out_specs = [pl.BlockSpec((tm, 512), tile)] * 10 + [
        pl.BlockSpec((tm, 256), tile), pl.BlockSpec((1, 2 * V_ROWS, tm), lambda i: (i // tps, 0, i % tps))]
    return pl.pallas_call(
        functools.partial(_pre0_kernel, tiles_per_seq=tps, tm=tm),
        grid=(nt,),
        in_specs=[
            pl.BlockSpec((tm, D_MODEL), tile),
            pl.BlockSpec((8, D_MODEL), lambda i: (jnp.maximum(i * t8 - 1, 0), 0)),
            pl.BlockSpec((8, D_MODEL), lambda i: (jnp.minimum((i + 1) * t8, nb8 - 1), 0)),
            pl.BlockSpec((1, D_MODEL), const),
            pl.BlockSpec((D_MODEL, ALL_COLS), const),
            pl.BlockSpec((1, RW_COLS), const),
            pl.BlockSpec((384, 2048), const),
            pl.BlockSpec((8, 512), const),
            pl.BlockSpec((1, 512), const),
            pl.BlockSpec((1, 128), const),
            pl.BlockSpec((tm, 128), lambda i: (i % tps, 0)),
            pl.BlockSpec((tm, 128), lambda i: (i % tps, 0)),
            pl.BlockSpec((512, 512), const),
        ],
        out_specs=out_specs,
        out_shape=out_shape,
        compiler_params=_params(("parallel",)),
        name="pre0",
    )(x2d, x2d, x2d, gain, w_all, mu_all, wup, par, qg, kg, cos_t, sin_t, ones_bd)


def _wkv_direction(r, k, v, a, b, L, h_ref, fwd):
    C = WKV_CHUNK
    Q = 4 * HEAD
    row = lax.broadcasted_iota(jnp.int32, L.shape, 0)
    cs = L
    for lvl in range(int(math.log2(C))):
        cs = cs + jnp.where(row >= 2 ** lvl, pltpu.roll(cs, 2 ** lvl, 0), 0.0)
    total = cs[C - 1:C, :]
    if fwd:
        cs_incl = cs
        cs_excl = cs - L
    else:
        cs_incl = total - (cs - L)
        cs_excl = total - cs
    e_incl = jnp.exp(cs_incl)
    e_inv = jnp.exp(-cs_incl)
    e_rem = jnp.exp(total - cs_incl)
    a_t = a * jnp.exp(cs_excl)
    r_t = r * e_incl
    b_t = b * e_inv
    k_t = k * e_inv
    b_h = b * e_rem
    k_h = k * e_rem
    gam = jnp.exp(total)

    streams = []
    for q in range(RWKV_DIM // Q):
        sl = slice(q * Q, (q + 1) * Q)
        streams.append(dict(
            q=q, h_ref=h_ref, fwd=fwd, gam=gam[:, sl], a_t=a_t[:, sl], r_t=r_t[:, sl], b_t=b_t[:, sl],
            k_t=k_t[:, sl], v=v[:, sl], b_h=b_h[:, sl], k_h=k_h[:, sl]))
    return streams


def _wkv_solve(streams):
    C = WKV_CHUNK
    Q = 4 * HEAD
    lane_q = lax.broadcasted_iota(jnp.int32, (C, Q), 1) // HEAD
    ri = lax.broadcasted_iota(jnp.int32, (Q, Q), 0)
    ci = lax.broadcasted_iota(jnp.int32, (Q, Q), 1)
    same = (ri // C) == (ci // C)
    tw = lax.broadcasted_iota(jnp.int32, (C, Q), 0)
    sw = lax.broadcasted_iota(jnp.int32, (C, Q), 1) % C
    masks = {True: (sw < tw, sw <= tw), False: (sw > tw, sw >= tw)}
    eye_w = sw == tw

    def stack(xq):
        return jnp.concatenate([jnp.where(lane_q == h, xq, 0.0) for h in range(4)], axis=0).astype(BF16)

    def blockdiag(w):
        return jnp.where(same, jnp.concatenate([w] * 4, axis=0), 0.0).astype(BF16)

    for s in streams:
        strict, incl = masks[s['fwd']]
        P = _dot_nt(jnp.concatenate([s['a_t'], s['r_t']], axis=0).astype(BF16),
                    jnp.concatenate([stack(s['b_t']), stack(s['k_t'])], axis=0))
        A_ab = jnp.where(strict, P[:C, :Q], 0.0)
        s['A_k'] = jnp.concatenate([jnp.where(strict, P[:C, Q:], 0.0), jnp.where(incl, P[C:, Q:], 0.0)],
                                   axis=0).astype(BF16)
        s['A_rb'] = jnp.where(incl, P[C:, :Q], 0.0).astype(BF16)
        s['T'] = jnp.where(eye_w, 1.0, A_ab)
        s['Ap'] = _dot(A_ab.astype(BF16), blockdiag(A_ab))
        s['Vs'] = stack(s['v'])
    levels = int(math.log2(C))
    for lvl in range(1, levels):
        for s in streams:
            rhs = blockdiag(s['Ap'])
            if lvl < levels - 1:
                both = _dot(jnp.concatenate([s['T'], s['Ap']], axis=0).astype(BF16), rhs)
                s['T'] = s['T'] + both[:C]
                s['Ap'] = both[C:]
            else:
                s['T'] = s['T'] + _dot(s['T'].astype(BF16), rhs)
    for s in streams:
        zy = _dot(s['A_k'], s['Vs'])
        s['Z'] = zy[:C]
        s['y_k'] = zy[C:]
        s['n_k'] = _dot_tn(s['k_h'].astype(BF16), s['v'].astype(BF16))
    for s in streams:
        s['X'] = _dot(s['T'].astype(BF16), jnp.concatenate([stack(s['a_t']), stack(s['Z'])], axis=1))
    for s in streams:
        X = s['X']
        W1 = _dot(s['A_rb'], jnp.concatenate([stack(X[:, :Q]), stack(X[:, Q:])], axis=1))
        MN = _dot_tn(s['b_h'].astype(BF16), X.astype(BF16))
        s['r_p'] = s['r_t'] + W1[:, :Q]
        s['y_p'] = W1[:, Q:] + s['y_k']
        s['M'] = jnp.where(same, MN[:, :Q], 0.0)
        s['N'] = jnp.where(same, MN[:, Q:] + s['n_k'], 0.0)
    ys = []
    for s in streams:
        h0 = s['h_ref'][s['q']]
        gam_col = jnp.sum(jnp.where(ri == ci, s['gam'], 0.0), axis=1, keepdims=True)
        seq = _dot(jnp.concatenate([s['r_p'], s['M']], axis=0).astype(BF16), h0.astype(BF16))
        ys.append(seq[:C] + s['y_p'])
        s['h_ref'][s['q']] = gam_col * h0 + seq[C:] + s['N']
    return ys


def _wkv_kernel(rf, kf, vf, af, bf, lf, rb, kb, vb, ab, bb, lb, yf_o, yb_o, hf_ref, hb_ref):
    @pl.when(pl.program_id(1) == 0)
    def _():
        hf_ref[...] = jnp.zeros_like(hf_ref)
        hb_ref[...] = jnp.zeros_like(hb_ref)

    C = WKV_CHUNK
    streams = []
    for refs, h_ref, fwd in (((rf, kf, vf, af, bf, lf), hf_ref, True), ((rb, kb, vb, ab, bb, lb), hb_ref, False)):
        for n in (range(WKV_STEP) if fwd else range(WKV_STEP - 1, -1, -1)):
            streams += _wkv_direction(*(x[n * C:(n + 1) * C, :] for x in refs), h_ref, fwd)
    ys = _wkv_solve(streams)
    for n in range(WKV_STEP):
        f0 = 2 * n
        b0 = 2 * WKV_STEP + 2 * (WKV_STEP - 1 - n)
        yf_o[n * C:(n + 1) * C, :] = jnp.concatenate(ys[f0:f0 + 2], axis=1)
        yb_o[n * C:(n + 1) * C, :] = jnp.concatenate(ys[b0:b0 + 2], axis=1)


def _wkv(r, k, v, a, b, lf, lb, n_seq, T):
    C = WKV_CHUNK * WKV_STEP
    nc = T // C
    fw = lambda s, i: (s * nc + i, 0)
    bw = lambda s, i: (s * nc + nc - 1 - i, 0)
    spec_f = pl.BlockSpec((C, 512), fw)
    spec_b = pl.BlockSpec((C, 512), bw)
    shp = jax.ShapeDtypeStruct(r.shape, F32)
    return pl.pallas_call(
        _wkv_kernel,
        grid=(n_seq, nc),
        in_specs=[spec_f] * 6 + [spec_b] * 6,
        out_specs=[spec_f, spec_b],
        out_shape=[shp, shp],
        scratch_shapes=[pltpu.VMEM((2, 256, 256), F32), pltpu.VMEM((2, 256, 256), F32)],
        compiler_params=_params(("parallel", "arbitrary")),
        name="wkv",
    )(r, k, v, a, b, lf, r, k, v, a, b, lb)


def _attn_kernel(q_ref, k_ref, vt_ref, o_ref, s0_ref, s1_ref, p0_ref, p1_ref, acc_ref, *, tq, tk, T):
    q = q_ref[...].astype(F32)
    lo = lax.broadcasted_iota(jnp.int32, (tq, 128), 1) < HEAD
    qs = jnp.concatenate([
        jnp.where(lo, q[:, 0:128], 0.0), jnp.where(lo, 0.0, q[:, 0:128]),
        jnp.where(lo, q[:, 128:256], 0.0), jnp.where(lo, 0.0, q[:, 128:256])], axis=0)
    qst = qs.T.astype(BF16)
    R = 4 * tq
    n = T // tk

    def chunk(j):
        return pl.ds(j * tk if isinstance(j, int) else pl.multiple_of(j * tk, tk), tk)

    def scores(j):
        return _dot(k_ref[chunk(j), :], qst)

    def values(j, p):
        return _dot(vt_ref[0, :, chunk(j)], p)

    s_buf = (s0_ref, s1_ref)
    p_buf = (p0_ref, p1_ref)
    s_buf[0][...] = scores(0)
    p_buf[1][...] = jnp.zeros((tk, R), BF16)
    acc_ref[...] = jnp.zeros((V_ROWS, R), F32)

    def step(j, b, carry, ahead=True):
        m, alpha_prev = carry
        if ahead:
            s_buf[1 - b][...] = scores(j + 1)
        prev = max(j - 1, 0) if isinstance(j, int) else jnp.maximum(j - 1, 0)
        acc_ref[...] = alpha_prev * acc_ref[...] + values(prev, p_buf[1 - b][...])
        s = s_buf[b][...]
        m_new = jnp.maximum(m, jnp.max(s, axis=0, keepdims=True))
        p_buf[b][...] = jnp.exp2(s - m_new).astype(BF16)
        return m_new, jnp.exp2(m - m_new)

    def body(i, carry):
        return step(2 * i + 1, 1, step(2 * i, 0, carry))

    carry = lax.fori_loop(0, n // 2 - 1, body, (jnp.full((1, R), -1e30, F32), jnp.ones((1, R), F32)))
    carry = step(n - 2, 0, carry)
    _, alpha = step(n - 1, 1, carry, ahead=False)
    acc = alpha * acc_ref[...] + values(n - 1, p_buf[1][...])
    ot = acc[0:HEAD] / acc[HEAD:HEAD + 1]
    o01 = jnp.concatenate([ot[:, 0:tq], ot[:, tq:2 * tq]], axis=0).T
    o23 = jnp.concatenate([ot[:, 2 * tq:3 * tq], ot[:, 3 * tq:4 * tq]], axis=0).T
    o_ref[...] = jnp.concatenate([o01, o23], axis=1)


def _attention(q, k_rep, v_t, n_seq, T):
    tq = min(1024, T)
    tk = min(512, T // 2)
    nq = T // tq
    assert (T // tk) % 2 == 0
    return pl.pallas_call(
        functools.partial(_attn_kernel, tq=tq, tk=tk, T=T),
        grid=(n_seq, N_KV, nq),
        in_specs=[
            pl.BlockSpec((tq, 256), lambda s, h, i: (s * nq + i, h)),
            pl.BlockSpec((T, 128), lambda s, h, i: (s, h)),
            pl.BlockSpec((1, V_ROWS, T), lambda s, h, i: (s, h, 0)),
        ],
        out_specs=pl.BlockSpec((tq, 256), lambda s, h, i: (s * nq + i, h)),
        out_shape=jax.ShapeDtypeStruct(q.shape, F32),
        scratch_shapes=[pltpu.VMEM((tk, 4 * tq), F32)] * 2 + [pltpu.VMEM((tk, 4 * tq), BF16)] * 2 + [
            pltpu.VMEM((V_ROWS, 4 * tq), F32)],
        compiler_params=_params(("parallel", "parallel", "arbitrary")),
        name="attention",
    )(q, k_rep, v_t)


MLP_TILE = 512
MLP_FF_CHUNK = 1024


def _mlp_step(hn_ref, wu_ref, wd_ref, o_ref):
    u = jnp.maximum(_dot(hn_ref[...], wu_ref[...]), 0.0)
    o_ref[...] += _dot((u * u).astype(BF16), wd_ref[...])


def _post0_mlp_kernel(x_ref, yf_ref, yb_ref, g_ref, bon_ref, att_ref, ln_ref, wo_ref, ones_ref, gain_ref,
                      gain_next_ref, wu_ref, wd_ref, o_ref, hn_next_ref, hn_ref):
    @pl.when(pl.program_id(1) == 0)
    def _():
        ones = ones_ref[...]
        y = yf_ref[...] + yb_ref[...]
        mean = _seg_sum(y, ones) * (1.0 / HEAD)
        d = y - mean
        var = _seg_sum(d * d, ones) * (1.0 / HEAD)
        yn = d * lax.rsqrt(var + GN_EPS) * ln_ref[0:1] + ln_ref[1:2]
        ya = ((yn + bon_ref[...]) * g_ref[...]).astype(BF16)
        mix = _dot(ya, wo_ref[0:512, :]) + _dot(att_ref[...].astype(BF16), wo_ref[512:1024, :])
        x1 = x_ref[...] + mix
        o_ref[...] = x1
        hn_ref[...] = _rms(x1, gain_ref[...]).astype(BF16)

    _mlp_step(hn_ref, wu_ref, wd_ref, o_ref)

    @pl.when(pl.program_id(1) == pl.num_programs(1) - 1)
    def _():
        hn_next_ref[...] = _rms(o_ref[...], gain_next_ref[...])


def _mlp_specs(tm):
    row = lambda i, j: (i, 0)
    const = lambda i, j: (0, 0)
    weights = [pl.BlockSpec((D_MODEL, MLP_FF_CHUNK), lambda i, j: (0, j)),
               pl.BlockSpec((MLP_FF_CHUNK, D_MODEL), lambda i, j: (j, 0))]
    return row, const, weights


def _post0_mlp(x2d, yf, yb, g, bon, att, ln, wo, ones_bd, gain, gain_next, w_up, w_down):
    N = x2d.shape[0]
    tm = min(MLP_TILE, N)
    row, const, weights = _mlp_specs(tm)
    wide = pl.BlockSpec((tm, D_MODEL), row)
    vec = pl.BlockSpec((1, D_MODEL), const)
    shp = jax.ShapeDtypeStruct(x2d.shape, F32)
    return pl.pallas_call(
        _post0_mlp_kernel,
        grid=(N // tm, D_FF // MLP_FF_CHUNK),
        in_specs=[wide] + [pl.BlockSpec((tm, 512), row)] * 5 + [
            pl.BlockSpec((8, 512), const), pl.BlockSpec((D_MODEL, D_MODEL), const), pl.BlockSpec((512, 512), const),
            vec, vec] + weights,
        out_specs=[wide, wide],
        out_shape=[shp, shp],
        scratch_shapes=[pltpu.VMEM((tm, D_MODEL), BF16)],
        compiler_params=_params(("parallel", "arbitrary")),
        name="post0_mlp",
    )(x2d, yf, yb, g, bon, att, ln, wo, ones_bd, gain, gain_next, w_up, w_down)


def _cmul_add(ar, ai, br, bi, cr, ci):
    return ar * br - ai * bi + cr, ar * bi + ai * br + ci


S5_SEGMENTS = 8


def _s5_scan(zs_ref, ps_ref, lam, pw_ref, fwd, nc):
    nl = zs_ref.shape[1]
    W = nl * 128
    ls = nc // S5_SEGMENTS
    ns = S5_SEGMENTS

    def gather(c, k):
        return jnp.concatenate([zs_ref[c, l, pl.ds(k, ns, stride=ls), :] for l in range(nl)], axis=1)

    def load(c, k):
        return jnp.concatenate([ps_ref[c, l, k * ns:(k + 1) * ns, :] for l in range(nl)], axis=1)

    def store(c, k, v):
        for l in range(nl):
            ps_ref[c, l, k * ns:(k + 1) * ns, :] = v[:, l * 128:(l + 1) * 128]

    xr = xi = jnp.zeros((ns, W), F32)
    for k in (range(ls) if fwd else range(ls - 1, -1, -1)):
        store(0, k, xr)
        store(1, k, xi)
        xr, xi = _cmul_add(lam[0:1], lam[1:2], xr, xi, gather(0, k), gather(1, k))
    cr = ci = jnp.zeros((1, W), F32)
    rows_r = [None] * S5_SEGMENTS
    rows_i = [None] * S5_SEGMENTS
    for s in (range(S5_SEGMENTS) if fwd else range(S5_SEGMENTS - 1, -1, -1)):
        rows_r[s] = cr
        rows_i[s] = ci
        cr, ci = _cmul_add(lam[2:3], lam[3:4], cr, ci, xr[s:s + 1], xi[s:s + 1])
    car_r = jnp.concatenate(rows_r, axis=0)
    car_i = jnp.concatenate(rows_i, axis=0)
    for k in range(ls):
        pr, pi = _cmul_add(pw_ref[0, k:k + 1, :], pw_ref[1, k:k + 1, :], car_r, car_i, load(0, k), load(1, k))
        store(0, k, pr)
        store(1, k, pi)


def _s5_kernel(x_ref, g2_ref, wz_ref, w2_ref, lam_ref, pw_ref, y_ref, lhs_ref, zs_ref, ps_ref, *, nc):
    C = S5_CHUNK
    W = S5_SLAB * S5_STATE
    nl = W // 128
    for j in range(C):
        lhs_ref[:, j * 128:(j + 1) * 128] = x_ref[pl.ds(j, nc, stride=C), :].astype(BF16)
    lhs = lhs_ref[...]
    for d in range(2):
        z = _dot(lhs, wz_ref[0, :, d * 2 * W:(d + 1) * 2 * W])
        for c in range(2):
            for l in range(nl):
                zs_ref[d, c, l] = z[:, c * W + l * 128:c * W + (l + 1) * 128]
        _s5_scan(zs_ref.at[d], ps_ref.at[d], lam_ref[0, d], pw_ref.at[0, d], d == 0, nc)
    ls = nc // S5_SEGMENTS

    def chunk_order(ref):
        return jnp.concatenate([ref[pl.ds(s, ls, stride=S5_SEGMENTS), :] for s in range(S5_SEGMENTS)], axis=0)

    pv = jnp.concatenate([chunk_order(ps_ref.at[d, c, l]) for d in range(2) for c in range(2) for l in range(nl)],
                         axis=1).astype(BF16)
    for i in range(0, C, 2):
        w_loc = g2_ref[0, (C - 1 - i) * 128:(2 * C - 1 - i) * 128, :]
        y2 = _dot(lhs, w_loc) + _dot(pv, w2_ref[0, :, i * 128:(i + 2) * 128])
        y_ref[pl.ds(i, nc, stride=C), :] = y2[:, :128]
        y_ref[pl.ds(i + 1, nc, stride=C), :] = y2[:, 128:]


def _s5_core(hn, g2, wz, w2, lam, pw, n_seq, T):
    nc = T // S5_CHUNK
    ls = nc // S5_SEGMENTS
    W = S5_SLAB * S5_STATE
    n_slab = D_MODEL // 128
    once = pl.Buffered(1)
    slab = lambda c, s: (c, 0, 0)
    return pl.pallas_call(
        functools.partial(_s5_kernel, nc=nc),
        grid=(n_slab, n_seq),
        in_specs=[
            pl.BlockSpec((T, 128), lambda c, s: (s, c)),
            pl.BlockSpec((1, 2 * S5_CHUNK * 128, 256), slab, pipeline_mode=once),
            pl.BlockSpec((1, S5_CHUNK * 128, 4 * W), slab, pipeline_mode=once),
            pl.BlockSpec((1, 4 * W, S5_CHUNK * 128), slab, pipeline_mode=once),
            pl.BlockSpec((1, 2, 8, W), lambda c, s: (c, 0, 0, 0)),
            pl.BlockSpec((1, 2, 2, ls, W), lambda c, s: (c, 0, 0, 0, 0)),
        ],
        out_specs=pl.BlockSpec((T, 128), lambda c, s: (s, c)),
        out_shape=jax.ShapeDtypeStruct(hn.shape, F32),
        scratch_shapes=[
            pltpu.VMEM((nc, S5_CHUNK * 128), BF16),
            pltpu.VMEM((2, 2, W // 128, nc, 128), F32), pltpu.VMEM((2, 2, W // 128, nc, 128), F32)],
        compiler_params=_params(("parallel", "arbitrary")),
        name="s5_core",
    )(hn, g2, wz, w2, lam, pw)


def _post1_mlp_kernel(x_ref, ys_ref, gain_ref, d_ref, wg_ref, bg_ref, gain2_ref, wu_ref, wd_ref, o_ref, hn_ref):
    @pl.when(pl.program_id(1) == 0)
    def _():
        x = x_ref[...]
        y = _rms(x, gain_ref[...]) * d_ref[...] + ys_ref[...]
        z = 0.5 * y * (1.0 + jnp.tanh(math.sqrt(2.0 / math.pi) * (y + 0.044715 * (y * y * y))))
        gate = _sigmoid(_dot(z.astype(BF16), wg_ref[...]) + bg_ref[...])
        x1 = x + z * gate
        o_ref[...] = x1
        hn_ref[...] = _rms(x1, gain2_ref[...]).astype(BF16)

    _mlp_step(hn_ref, wu_ref, wd_ref, o_ref)


def _post1_mlp(x2d, ys, gain, d, wg, bg, gain2, w_up, w_down):
    N = x2d.shape[0]
    tm = min(MLP_TILE, N)
    row, const, weights = _mlp_specs(tm)
    wide = pl.BlockSpec((tm, D_MODEL), row)
    vec = pl.BlockSpec((1, D_MODEL), const)
    return pl.pallas_call(
        _post1_mlp_kernel,
        grid=(N // tm, D_FF // MLP_FF_CHUNK),
        in_specs=[wide, wide, vec, vec, pl.BlockSpec((D_MODEL, D_MODEL), const), vec, vec] + weights,
        out_specs=wide,
        out_shape=jax.ShapeDtypeStruct(x2d.shape, F32),
        scratch_shapes=[pltpu.VMEM((tm, D_MODEL), BF16)],
        compiler_params=_params(("parallel", "arbitrary")),
        name="post1_mlp",
    )(x2d, ys, gain, d, wg, bg, gain2, w_up, w_down)


def _s5_tables(p, T):
    C, G, P, SL = S5_CHUNK, S5_GROUPS, S5_STATE, S5_SLAB
    ns = G // SL
    ls = T // C // S5_SEGMENTS
    eye = jnp.eye(SL, dtype=F32)
    b_re = p['s5_b_re'][0].astype(F32)
    b_im = p['s5_b_im'][0].astype(F32)
    steps = jnp.arange(C, dtype=F32)

    def direction(sfx):
        lr = p['s5_lam_re_' + sfx][0].astype(F32)
        li = p['s5_lam_im_' + sfx][0].astype(F32)
        dt = jnp.exp(p['s5_log_dt_' + sfx][0].astype(F32))[:, None]

        def power(k):
            k = k[:, None, None]
            mag = jnp.exp(lr * dt * k)
            return mag * jnp.cos(li * dt * k), mag * jnp.sin(li * dt * k)

        l1r, l1i = power(jnp.ones((1,), F32))
        nr, ni = l1r[0] - 1.0, l1i[0]
        den = lr * lr + li * li
        cr = (nr * lr + ni * li) / den
        ci = (ni * lr - nr * li) / den
        cb_r = cr[:, :, None] * b_re - ci[:, :, None] * b_im
        cb_i = cr[:, :, None] * b_im + ci[:, :, None] * b_re
        c_r = p['s5_c_re_' + sfx][0].astype(F32)
        c_i = p['s5_c_im_' + sfx][0].astype(F32)
        return power, cb_r, cb_i, c_r, c_i

    def kernels(power, cb_r, cb_i, c_r, c_i):
        pr, pi = power(steps)
        d_r = pr[..., None] * cb_r - pi[..., None] * cb_i
        d_i = pr[..., None] * cb_i + pi[..., None] * cb_r
        k = jnp.einsum('gop,lgpi->lgio', c_r, d_r) - jnp.einsum('gop,lgpi->lgio', c_i, d_i)
        k = k.reshape(C, ns, SL, S5_GROUP, 1, S5_GROUP)
        return jnp.where(eye[:, None, :, None] > 0, k, 0.0).reshape(C, ns, 128, 128)

    def state_in(power, cb_r, cb_i, ks):
        pr, pi = power(ks)
        w_r = (pr[..., None] * cb_r - pi[..., None] * cb_i).reshape(C, ns, SL, P, S5_GROUP)
        w_i = (pr[..., None] * cb_i + pi[..., None] * cb_r).reshape(C, ns, SL, P, S5_GROUP)
        return [w_r, w_i]

    def state_out(power, c_r, c_i, ks):
        pr, pi = power(ks)
        e_r = (c_r[None] * pr[:, :, None, :] - c_i[None] * pi[:, :, None, :]).reshape(C, ns, SL, S5_GROUP, P)
        e_i = (c_r[None] * pi[:, :, None, :] + c_i[None] * pr[:, :, None, :]).reshape(C, ns, SL, S5_GROUP, P)
        return [e_r, -e_i]

    def scan_tables(power, ks_rows):
        lr_, li_ = power(C * jnp.array([1.0, ls], F32))
        lam = jnp.stack([lr_, li_], axis=1).reshape(4, ns, SL * P)
        lam = jnp.concatenate([lam, jnp.zeros_like(lam)], axis=0).transpose(1, 0, 2)
        pr, pi = power(C * ks_rows)
        pw = jnp.stack([pr.reshape(ls, ns, SL * P), pi.reshape(ls, ns, SL * P)], axis=0).transpose(2, 0, 1, 3)
        return lam, pw

    def expand(x, row_group, col_group):
        src = jnp.arange(2048) // (col_group * SL) * col_group + jnp.arange(2048) % col_group
        spread = (jnp.arange(256)[:, None] == src[None, :]).astype(BF16)
        full = jnp.einsum('srk,kc->src', x.astype(BF16), spread, preferred_element_type=F32)
        same = ((jnp.arange(2048) // row_group) % SL)[:, None] == ((jnp.arange(2048) // col_group) % SL)[None, :]
        return jnp.where(same, full, 0.0).astype(BF16)

    pf = direction('f')
    pb = direction('b')
    kf = kernels(*pf)
    kb = kernels(*pb)
    zero = jnp.zeros((1, ns, 128, 128), F32)
    gen = jnp.concatenate([kf[:0:-1], (kf[0] + kb[0])[None], kb[1:], zero], axis=0)
    gen_prev = jnp.concatenate([zero, gen[:-1]], axis=0)
    g2 = jnp.concatenate([gen, gen_prev], axis=-1).transpose(1, 0, 2, 3).reshape(ns, 2 * C * 128, 256)

    w4 = jnp.stack(state_in(pf[0], pf[1], pf[2], (C - 1) - steps) + state_in(pb[0], pb[1], pb[2], steps))
    wz = expand(w4.transpose(2, 1, 3, 5, 0, 4).reshape(ns, C * 128, 4 * P), S5_GROUP, P)
    e4 = jnp.stack(state_out(pf[0], pf[3], pf[4], steps + 1.0) + state_out(pb[0], pb[3], pb[4], C - steps))
    w2 = expand(e4.transpose(2, 0, 3, 5, 1, 4).reshape(ns, 4 * SL * P, C * S5_GROUP), P, S5_GROUP)
    pos = jnp.arange(ls, dtype=F32)
    lam_f, pw_f = scan_tables(pf[0], pos)
    lam_b, pw_b = scan_tables(pb[0], (ls - 1.0) - pos)
    lam = jnp.stack([lam_f, lam_b], axis=1)
    pw = jnp.stack([pw_f, pw_b], axis=1)
    return g2.astype(BF16), wz, w2, lam, pw


def _rope_tables(T):
    rows = T // GRID_W
    row_ids = np.repeat(np.arange(rows, dtype=np.float32), GRID_W)
    col_ids = np.tile(np.arange(GRID_W, dtype=np.float32), rows)
    pairs = HEAD // 4
    inv_freq = (np.float32(ROPE_THETA) ** (-np.arange(pairs, dtype=np.float32) / pairs)).astype(np.float32)
    ang = np.concatenate([row_ids[:, None] * inv_freq, col_ids[:, None] * inv_freq], axis=-1)
    cos = np.repeat(np.cos(ang), 2, axis=-1)
    sin = np.repeat(np.sin(ang), 2, axis=-1)
    sign = np.tile(np.array([-1.0, 1.0], np.float32), HEAD // 2)
    return jnp.asarray(np.tile(cos, (1, 2)), F32), jnp.asarray(np.tile(sin * sign, (1, 2)), F32)


def _layer0_weights(p, T):
    w_in = p['hyb_w_in'][0]
    zc = jnp.zeros((D_MODEL, 64), F32)
    w_all = jnp.concatenate([w_in[:, 0:1728], zc, w_in[:, 1728:2624]], axis=1).astype(BF16)
    mu = p['hyb_shift_mu'][0]
    mu_all = jnp.concatenate([mu[0:1728], jnp.zeros((64,), F32), mu[1728:1856]]).reshape(1, RW_COLS)
    wup = jnp.zeros((384, 2048), F32)
    wup = wup.at[0:64, 0:512].set(p['rwkv_w_up_f'][0])
    wup = wup.at[64:128, 512:1024].set(p['rwkv_w_up_b'][0])
    wup = wup.at[128:192, 1024:1536].set(p['rwkv_a_up'][0])
    wup = wup.at[256:384, 1536:2048].set(p['rwkv_g_up'][0])
    wup = wup.astype(BF16)
    zr = jnp.zeros((512,), F32)
    par = jnp.stack([p['rwkv_w0_f'][0], p['rwkv_w0_b'][0], p['rwkv_a0'][0], p['rwkv_k_k'][0],
                     p['rwkv_k_a'][0], p['rwkv_r_k'][0].reshape(-1), zr, zr]).astype(F32)
    qg = jnp.tile(p['att_q_norm'][0], 8).reshape(1, 512).astype(F32)
    kg = jnp.tile(p['att_k_norm'][0], 2).reshape(1, 128).astype(F32)
    cos_t, sin_t = _rope_tables(T)
    seg = jnp.arange(512) // HEAD
    ones_bd = (seg[:, None] == seg[None, :]).astype(BF16)
    ln = jnp.stack([p['rwkv_lnx_g'][0], p['rwkv_lnx_b'][0]] + [zr] * 6).astype(F32)
    return dict(w_all=w_all, mu_all=mu_all, wup=wup, par=par, qg=qg, kg=kg, cos_t=cos_t, sin_t=sin_t,
                ones_bd=ones_bd, ln=ln, wo=p['hyb_w_out'][0].astype(BF16))


def _row(v):
    return v.reshape(1, -1).astype(F32)


def _layer0(x2d, p, w, ffn_w, n_seq, T):
    r, k, v, a, b, lf, lb, g, bon, q, k_rep, v_t = _pre0(
        x2d, T, _row(p['mix_norm'][0]), w['w_all'], w['mu_all'], w['wup'], w['par'], w['qg'], w['kg'],
        w['cos_t'], w['sin_t'], w['ones_bd'])
    yf, yb = _wkv(r, k, v, a, b, lf, lb, n_seq, T)
    att = _attention(q, k_rep, v_t, n_seq, T)
    return _post0_mlp(x2d, yf, yb, g, bon, att, w['ln'], w['wo'], w['ones_bd'], _row(p['ffn_norm'][0]),
                      _row(p['mix_norm'][1]), ffn_w[0][0], ffn_w[0][1])


def _layer1(x2d, hn, p, s5, ffn_w, n_seq, T):
    g2, wz, w2, lam, pw = s5
    ys = _s5_core(hn, g2, wz, w2, lam, pw, n_seq, T)
    return _post1_mlp(x2d, ys, _row(p['mix_norm'][1]), _row(p['s5_d'][0]), p['s5_glu_w'][0].astype(BF16),
                      _row(p['s5_glu_b'][0]), _row(p['ffn_norm'][1]), ffn_w[1][0], ffn_w[1][1])


def _prepare(p, T):
    ffn_w = [(p['ffn_up'][l].astype(BF16), p['ffn_down'][l].astype(BF16)) for l in range(2)]
    return _layer0_weights(p, T), _s5_tables(p, T), ffn_w


def _trunk(x, p, prep=None):
    n_seq, T, _ = x.shape
    w0, s5, ffn_w = _prepare(p, T) if prep is None else prep
    x2d = x.reshape(n_seq * T, D_MODEL)
    x2d, hn = _layer0(x2d, p, w0, ffn_w, n_seq, T)
    x2d = _layer1(x2d, hn, p, s5, ffn_w, n_seq, T)
    return x2d.reshape(n_seq, T, D_MODEL)


def kernel(x_prompt, x_sample, mix_norm, ffn_norm, ffn_up, ffn_down, hyb_w_in, hyb_shift_mu, rwkv_w0_f, rwkv_w_up_f, rwkv_w0_b, rwkv_w_up_b, rwkv_a0, rwkv_a_up, rwkv_g_up, rwkv_k_k, rwkv_k_a, rwkv_r_k, rwkv_lnx_g, rwkv_lnx_b, att_q_norm, att_k_norm, hyb_w_out, s5_lam_re_f, s5_lam_im_f, s5_log_dt_f, s5_lam_re_b, s5_lam_im_b, s5_log_dt_b, s5_b_re, s5_b_im, s5_c_re_f, s5_c_im_f, s5_c_re_b, s5_c_im_b, s5_d, s5_glu_w, s5_glu_b):
    p = dict(mix_norm=mix_norm, ffn_norm=ffn_norm, ffn_up=ffn_up, ffn_down=ffn_down,
             hyb_w_in=hyb_w_in, hyb_shift_mu=hyb_shift_mu,
             rwkv_w0_f=rwkv_w0_f, rwkv_w_up_f=rwkv_w_up_f, rwkv_w0_b=rwkv_w0_b, rwkv_w_up_b=rwkv_w_up_b,
             rwkv_a0=rwkv_a0, rwkv_a_up=rwkv_a_up, rwkv_g_up=rwkv_g_up,
             rwkv_k_k=rwkv_k_k, rwkv_k_a=rwkv_k_a, rwkv_r_k=rwkv_r_k,
             rwkv_lnx_g=rwkv_lnx_g, rwkv_lnx_b=rwkv_lnx_b,
             att_q_norm=att_q_norm, att_k_norm=att_k_norm, hyb_w_out=hyb_w_out,
             s5_lam_re_f=s5_lam_re_f, s5_lam_im_f=s5_lam_im_f, s5_log_dt_f=s5_log_dt_f,
             s5_lam_re_b=s5_lam_re_b, s5_lam_im_b=s5_lam_im_b, s5_log_dt_b=s5_log_dt_b,
             s5_b_re=s5_b_re, s5_b_im=s5_b_im,
             s5_c_re_f=s5_c_re_f, s5_c_im_f=s5_c_im_f, s5_c_re_b=s5_c_re_b, s5_c_im_b=s5_c_im_b,
             s5_d=s5_d, s5_glu_w=s5_glu_w, s5_glu_b=s5_glu_b)
    assert x_prompt.shape[1] == x_sample.shape[1]
    prep = _prepare(p, x_prompt.shape[1])
    return (_trunk(x_prompt, p, prep), _trunk(x_sample, p, prep))
```

```python
import functools
import math

import jax
import jax.numpy as jnp
import numpy as np
from jax import lax
from jax.experimental import pallas as pl
from jax.experimental.pallas import tpu as pltpu

F32 = jnp.float32
BF16 = jnp.bfloat16

D_MODEL = 1024
D_FF = 4 * D_MODEL
RMS_EPS = 1e-6
GRID_W = 64
RWKV_DIM = 512
HEAD = 64
GN_EPS = 64e-5
N_KV = 2
ROPE_THETA = 10000.0
S5_GROUP = 16
S5_GROUPS = D_MODEL // S5_GROUP
S5_STATE = 64
S5_SLAB = 128 // S5_GROUP

WKV_CHUNK = 64
WKV_STEP = 2
S5_CHUNK = 16
RW_COLS = 1920
ALL_COLS = 2688
EXP_M05 = math.exp(-0.5)
LOG2_E = math.log2(math.e)
V_ROWS = HEAD + 16
VMEM_LIMIT = 56 * 1024 * 1024


def _dot(a, b):
    return jnp.dot(a, b, preferred_element_type=F32)


def _dot_nt(a, b):
    return lax.dot_general(a, b, (((1,), (1,)), ((), ())), preferred_element_type=F32)


def _dot_tn(a, b):
    return lax.dot_general(a, b, (((0,), (0,)), ((), ())), preferred_element_type=F32)


def _split2(x):
    hi = x.astype(BF16)
    lo = (x - hi.astype(F32)).astype(BF16)
    return hi, lo


def _seg_sum(x, ones_bd, split=True):
    if not split:
        return _dot(x.astype(BF16), ones_bd)
    hi, lo = _split2(x)
    return _dot(hi, ones_bd) + _dot(lo, ones_bd)


def _rms(x, gain):
    return x * lax.rsqrt(jnp.mean(x * x, axis=-1, keepdims=True) + RMS_EPS) * gain


def _sigmoid(x):
    return 1.0 / (1.0 + jnp.exp(-x))


def _params(sem):
    return pltpu.CompilerParams(dimension_semantics=sem, vmem_limit_bytes=VMEM_LIMIT)


def _pre0_kernel(x_ref, xp_ref, xn_ref, gain_ref, w_ref, mu_ref, wup_ref, par_ref, qg_ref, kg_ref,
                 cos_ref, sin_ref, ones_ref,
                 r_o, k_o, v_o, a_o, b_o, lf_o, lb_o, g_o, bon_o, q_o, ka_o, va_o,
                 *, tiles_per_seq, tm):
    pos = pl.program_id(0) % tiles_per_seq
    gain = gain_ref[...]
    xp = jnp.where(pos == 0, 0.0, xp_ref[...])
    xn = jnp.where(pos == tiles_per_seq - 1, 0.0, xn_ref[...])
    x_all = jnp.concatenate([xp, x_ref[...], xn], axis=0)
    H_all = _dot(_rms(x_all, gain).astype(BF16), w_ref[...])
    H = H_all[8:8 + tm]
    Hr = H[:, :RW_COLS]
    Hr_all = H_all[:, :RW_COLS]
    prev = pltpu.roll(Hr_all, 1, 0)[8:8 + tm]
    nxt = pltpu.roll(Hr_all, tm + 15, 0)[8:8 + tm]
    Hs = Hr + mu_ref[...] * (0.5 * (prev + nxt) - Hr)

    ones = ones_ref[...]
    par = par_ref[...]
    r = Hs[:, 0:512]
    k = Hs[:, 512:1024]
    v = Hs[:, 1024:1536]
    up_w = _dot(jnp.tanh(Hs[:, 1536:1664]).astype(BF16), wup_ref[0:128, 0:1024])
    up_a = _dot(Hs[:, 1664:1792].astype(BF16), wup_ref[128:256, 1024:1536])
    lf_o[...] = -EXP_M05 * _sigmoid(par[0:1] + up_w[:, 0:512])
    lb_o[...] = -EXP_M05 * _sigmoid(par[1:2] + up_w[:, 512:1024])
    a_sig = _sigmoid(par[2:3] + up_a)
    g_o[...] = _dot(_sigmoid(Hs[:, 1792:1920]).astype(BF16), wup_ref[256:384, 1536:2048])
    kk = k * par[3:4]
    kk = kk / jnp.maximum(jnp.sqrt(_seg_sum(kk * kk, ones, split=False)), 1e-12)
    k2 = k * (1.0 + (a_sig - 1.0) * par[4:5])
    r_o[...] = r
    k_o[...] = k2
    v_o[...] = v
    a_o[...] = -kk
    b_o[...] = kk * a_sig
    bon_o[...] = _seg_sum(r * k2 * par[5:6], ones) * v

    qa = H[:, RW_COLS:RW_COLS + 512]
    ka = H[:, RW_COLS + 512:RW_COLS + 640]
    va = H[:, RW_COLS + 640:RW_COLS + 768]
    cos = cos_ref[...]
    sin = sin_ref[...]

    def rope(x, c, s):
        n = x.shape[1]
        lane = lax.broadcasted_iota(jnp.int32, x.shape, 1)
        swapped = jnp.where(lane % 2 == 0, pltpu.roll(x, n - 1, 1), pltpu.roll(x, 1, 1))
        return x * c + swapped * s

    qn = qa * lax.rsqrt(_seg_sum(qa * qa, ones, split=False) * (1.0 / HEAD) + RMS_EPS) * qg_ref[...]
    qr = rope(qn, jnp.concatenate([cos] * 4, axis=1), jnp.concatenate([sin] * 4, axis=1))
    q_o[...] = (qr * (HEAD ** -0.5 * LOG2_E)).astype(BF16)
    kn = ka * lax.rsqrt(_seg_sum(ka * ka, ones[:128, :128], split=False) * (1.0 / HEAD) + RMS_EPS) * kg_ref[...]
    kr = rope(kn, cos, sin)
    lt64 = lax.broadcasted_iota(jnp.int32, kr.shape, 1) < HEAD

    def rep(x):
        sw = pltpu.roll(x, HEAD, 1)
        return jnp.concatenate([jnp.where(lt64, x, sw), jnp.where(lt64, sw, x)], axis=1).astype(BF16)

    ka_o[...] = rep(kr)
    vt = va.T.astype(BF16)
    one = jnp.ones((V_ROWS - HEAD, tm), BF16)
    va_o[0] = jnp.concatenate([vt[0:HEAD], one, vt[HEAD:2 * HEAD], one], axis=0)


def _pre0(x2d, T, gain, w_all, mu_all, wup, par, qg, kg, cos_t, sin_t, ones_bd):
    N = x2d.shape[0]
    tm = min(512, T)
    tps = T // tm
    nt = N // tm
    t8 = tm // 8
    nb8 = N // 8
    const = lambda i: (0, 0)
    tile = lambda i: (i, 0)
    f512 = jax.ShapeDtypeStruct((N, 512), F32)
    out_shape = [f512] * 9 + [jax.ShapeDtypeStruct((N, 512), BF16), jax.ShapeDtypeStruct((N, 256), BF16),
                              jax.ShapeDtypeStruct((N // T, 2 * V_ROWS, T), BF16)]
    out_specs = [pl.BlockSpec((tm, 512), tile)] * 10 + [
        pl.BlockSpec((tm, 256), tile), pl.BlockSpec((1, 2 * V_ROWS, tm), lambda i: (i // tps, 0, i % tps))]
    return pl.pallas_call(
        functools.partial(_pre0_kernel, tiles_per_seq=tps, tm=tm),
        grid=(nt,),
        in_specs=[
            pl.BlockSpec((tm, D_MODEL), tile),
            pl.BlockSpec((8, D_MODEL), lambda i: (jnp.maximum(i * t8 - 1, 0), 0)),
            pl.BlockSpec((8, D_MODEL), lambda i: (jnp.minimum((i + 1) * t8, nb8 - 1), 0)),
            pl.BlockSpec((1, D_MODEL), const),
            pl.BlockSpec((D_MODEL, ALL_COLS), const),
            pl.BlockSpec((1, RW_COLS), const),
            pl.BlockSpec((384, 2048), const),
            pl.BlockSpec((8, 512), const),
            pl.BlockSpec((1, 512), const),
            pl.BlockSpec((1, 128), const),
            pl.BlockSpec((tm, 128), lambda i: (i % tps, 0)),
            pl.BlockSpec((tm, 128), lambda i: (i % tps, 0)),
            pl.BlockSpec((512, 512), const),
        ],
        out_specs=out_specs,
        out_shape=out_shape,
        compiler_params=_params(("parallel",)),
        name="pre0",
    )(x2d, x2d, x2d, gain, w_all, mu_all, wup, par, qg, kg, cos_t, sin_t, ones_bd)


def _wkv_direction(r, k, v, a, b, L, h_ref, fwd):
    C = WKV_CHUNK
    Q = 4 * HEAD
    row = lax.broadcasted_iota(jnp.int32, L.shape, 0)
    cs = L
    for lvl in range(int(math.log2(C))):
        cs = cs + jnp.where(row >= 2 ** lvl, pltpu.roll(cs, 2 ** lvl, 0), 0.0)
    total = cs[C - 1:C, :]
    if fwd:
        cs_incl = cs
        cs_excl = cs - L
    else:
        cs_incl = total - (cs - L)
        cs_excl = total - cs
    e_incl = jnp.exp(cs_incl)
    e_inv = jnp.exp(-cs_incl)
    e_rem = jnp.exp(total - cs_incl)
    a_t = a * jnp.exp(cs_excl)
    r_t = r * e_incl
    b_t = b * e_inv
    k_t = k * e_inv
    b_h = b * e_rem
    k_h = k * e_rem
    gam = jnp.exp(total)

    streams = []
    for q in range(RWKV_DIM // Q):
        sl = slice(q * Q, (q + 1) * Q)
        streams.append(dict(
            q=q, h_ref=h_ref, fwd=fwd, gam=gam[:, sl], a_t=a_t[:, sl], r_t=r_t[:, sl], b_t=b_t[:, sl],
            k_t=k_t[:, sl], v=v[:, sl], b_h=b_h[:, sl], k_h=k_h[:, sl]))
    return streams


def _wkv_solve(streams):
    C = WKV_CHUNK
    Q = 4 * HEAD
    lane_q = lax.broadcasted_iota(jnp.int32, (C, Q), 1) // HEAD
    ri = lax.broadcasted_iota(jnp.int32, (Q, Q), 0)
    ci = lax.broadcasted_iota(jnp.int32, (Q, Q), 1)
    same = (ri // C) == (ci // C)
    tw = lax.broadcasted_iota(jnp.int32, (C, Q), 0)
    sw = lax.broadcasted_iota(jnp.int32, (C, Q), 1) % C
    masks = {True: (sw < tw, sw <= tw), False: (sw > tw, sw >= tw)}
    eye_w = sw == tw

    def stack(xq):
        return jnp.concatenate([jnp.where(lane_q == h, xq, 0.0) for h in range(4)], axis=0).astype(BF16)

    def blockdiag(w):
        return jnp.where(same, jnp.concatenate([w] * 4, axis=0), 0.0).astype(BF16)

    for s in streams:
        strict, incl = masks[s['fwd']]
        P = _dot_nt(jnp.concatenate([s['a_t'], s['r_t']], axis=0).astype(BF16),
                    jnp.concatenate([stack(s['b_t']), stack(s['k_t'])], axis=0))
        A_ab = jnp.where(strict, P[:C, :Q], 0.0)
        s['A_k'] = jnp.concatenate([jnp.where(strict, P[:C, Q:], 0.0), jnp.where(incl, P[C:, Q:], 0.0)],
                                   axis=0).astype(BF16)
        s['A_rb'] = jnp.where(incl, P[C:, :Q], 0.0).astype(BF16)
        s['T'] = jnp.where(eye_w, 1.0, A_ab)
        s['Ap'] = _dot(A_ab.astype(BF16), blockdiag(A_ab))
        s['Vs'] = stack(s['v'])
    levels = int(math.log2(C))
    for lvl in range(1, levels):
        for s in streams:
            rhs = blockdiag(s['Ap'])
            if lvl < levels - 1:
                both = _dot(jnp.concatenate([s['T'], s['Ap']], axis=0).astype(BF16), rhs)
                s['T'] = s['T'] + both[:C]
                s['Ap'] = both[C:]
            else:
                s['T'] = s['T'] + _dot(s['T'].astype(BF16), rhs)
    for s in streams:
        zy = _dot(s['A_k'], s['Vs'])
        s['Z'] = zy[:C]
        s['y_k'] = zy[C:]
        s['n_k'] = _dot_tn(s['k_h'].astype(BF16), s['v'].astype(BF16))
    for s in streams:
        s['X'] = _dot(s['T'].astype(BF16), jnp.concatenate([stack(s['a_t']), stack(s['Z'])], axis=1))
    for s in streams:
        X = s['X']
        W1 = _dot(s['A_rb'], jnp.concatenate([stack(X[:, :Q]), stack(X[:, Q:])], axis=1))
        MN = _dot_tn(s['b_h'].astype(BF16), X.astype(BF16))
        s['r_p'] = s['r_t'] + W1[:, :Q]
        s['y_p'] = W1[:, Q:] + s['y_k']
        s['M'] = jnp.where(same, MN[:, :Q], 0.0)
        s['N'] = jnp.where(same, MN[:, Q:] + s['n_k'], 0.0)
    ys = []
    for s in streams:
        h0 = s['h_ref'][s['q']]
        gam_col = jnp.sum(jnp.where(ri == ci, s['gam'], 0.0), axis=1, keepdims=True)
        seq = _dot(jnp.concatenate([s['r_p'], s['M']], axis=0).astype(BF16), h0.astype(BF16))
        ys.append(seq[:C] + s['y_p'])
        s['h_ref'][s['q']] = gam_col * h0 + seq[C:] + s['N']
    return ys


def _wkv_kernel(rf, kf, vf, af, bf, lf, rb, kb, vb, ab, bb, lb, yf_o, yb_o, hf_ref, hb_ref):
    @pl.when(pl.program_id(1) == 0)
    def _():
        hf_ref[...] = jnp.zeros_like(hf_ref)
        hb_ref[...] = jnp.zeros_like(hb_ref)

    C = WKV_CHUNK
    streams = []
    for refs, h_ref, fwd in (((rf, kf, vf, af, bf, lf), hf_ref, True), ((rb, kb, vb, ab, bb, lb), hb_ref, False)):
        for n in (range(WKV_STEP) if fwd else range(WKV_STEP - 1, -1, -1)):
            streams += _wkv_direction(*(x[n * C:(n + 1) * C, :] for x in refs), h_ref, fwd)
    ys = _wkv_solve(streams)
    for n in range(WKV_STEP):
        f0 = 2 * n
        b0 = 2 * WKV_STEP + 2 * (WKV_STEP - 1 - n)
        yf_o[n * C:(n + 1) * C, :] = jnp.concatenate(ys[f0:f0 + 2], axis=1)
        yb_o[n * C:(n + 1) * C, :] = jnp.concatenate(ys[b0:b0 + 2], axis=1)


def _wkv(r, k, v, a, b, lf, lb, n_seq, T):
    C = WKV_CHUNK * WKV_STEP
    nc = T // C
    fw = lambda s, i: (s * nc + i, 0)
    bw = lambda s, i: (s * nc + nc - 1 - i, 0)
    spec_f = pl.BlockSpec((C, 512), fw)
    spec_b = pl.BlockSpec((C, 512), bw)
    shp = jax.ShapeDtypeStruct(r.shape, F32)
    return pl.pallas_call(
        _wkv_kernel,
        grid=(n_seq, nc),
        in_specs=[spec_f] * 6 + [spec_b] * 6,
        out_specs=[spec_f, spec_b],
        out_shape=[shp, shp],
        scratch_shapes=[pltpu.VMEM((2, 256, 256), F32), pltpu.VMEM((2, 256, 256), F32)],
        compiler_params=_params(("parallel", "arbitrary")),
        name="wkv",
    )(r, k, v, a, b, lf, r, k, v, a, b, lb)


def _attn_kernel(q_ref, k_ref, vt_ref, o_ref, s0_ref, s1_ref, p0_ref, p1_ref, acc_ref, *, tq, tk, T):
    q = q_ref[...].astype(F32)
    lo = lax.broadcasted_iota(jnp.int32, (tq, 128), 1) < HEAD
    qs = jnp.concatenate([
        jnp.where(lo, q[:, 0:128], 0.0), jnp.where(lo, 0.0, q[:, 0:128]),
        jnp.where(lo, q[:, 128:256], 0.0), jnp.where(lo, 0.0, q[:, 128:256])], axis=0)
    qst = qs.T.astype(BF16)
    R = 4 * tq
    n = T // tk

    def chunk(j):
        return pl.ds(j * tk if isinstance(j, int) else pl.multiple_of(j * tk, tk), tk)

    def scores(j):
        return _dot(k_ref[chunk(j), :], qst)

    def values(j, p):
        return _dot(vt_ref[0, :, chunk(j)], p)

    s_buf = (s0_ref, s1_ref)
    p_buf = (p0_ref, p1_ref)
    s_buf[0][...] = scores(0)
    p_buf[1][...] = jnp.zeros((tk, R), BF16)
    acc_ref[...] = jnp.zeros((V_ROWS, R), F32)

    def step(j, b, carry, ahead=True):
        m, alpha_prev = carry
        if ahead:
            s_buf[1 - b][...] = scores(j + 1)
        prev = max(j - 1, 0) if isinstance(j, int) else jnp.maximum(j - 1, 0)
        acc_ref[...] = alpha_prev * acc_ref[...] + values(prev, p_buf[1 - b][...])
        s = s_buf[b][...]
        m_new = jnp.maximum(m, jnp.max(s, axis=0, keepdims=True))
        p_buf[b][...] = jnp.exp2(s - m_new).astype(BF16)
        return m_new, jnp.exp2(m - m_new)

    def body(i, carry):
        return step(2 * i + 1, 1, step(2 * i, 0, carry))

    carry = lax.fori_loop(0, n // 2 - 1, body, (jnp.full((1, R), -1e30, F32), jnp.ones((1, R), F32)))
    carry = step(n - 2, 0, carry)
    _, alpha = step(n - 1, 1, carry, ahead=False)
    acc = alpha * acc_ref[...] + values(n - 1, p_buf[1][...])
    ot = acc[0:HEAD] / acc[HEAD:HEAD + 1]
    o01 = jnp.concatenate([ot[:, 0:tq], ot[:, tq:2 * tq]], axis=0).T
    o23 = jnp.concatenate([ot[:, 2 * tq:3 * tq], ot[:, 3 * tq:4 * tq]], axis=0).T
    o_ref[...] = jnp.concatenate([o01, o23], axis=1)


def _attention(q, k_rep, v_t, n_seq, T):
    tq = min(1024, T)
    tk = min(512, T // 2)
    nq = T // tq
    assert (T // tk) % 2 == 0
    return pl.pallas_call(
        functools.partial(_attn_kernel, tq=tq, tk=tk, T=T),
        grid=(n_seq, N_KV, nq),
        in_specs=[
            pl.BlockSpec((tq, 256), lambda s, h, i: (s * nq + i, h)),
            pl.BlockSpec((T, 128), lambda s, h, i: (s, h)),
            pl.BlockSpec((1, V_ROWS, T), lambda s, h, i: (s, h, 0)),
        ],
        out_specs=pl.BlockSpec((tq, 256), lambda s, h, i: (s * nq + i, h)),
        out_shape=jax.ShapeDtypeStruct(q.shape, F32),
        scratch_shapes=[pltpu.VMEM((tk, 4 * tq), F32)] * 2 + [pltpu.VMEM((tk, 4 * tq), BF16)] * 2 + [
            pltpu.VMEM((V_ROWS, 4 * tq), F32)],
        compiler_params=_params(("parallel", "parallel", "arbitrary")),
        name="attention",
    )(q, k_rep, v_t)


MLP_TILE = 512
MLP_FF_CHUNK = 2048


def _mlp_step(hn_ref, wu_ref, wd_ref, o_ref):
    u = jnp.maximum(_dot(hn_ref[...], wu_ref[...]), 0.0)
    o_ref[...] += _dot((u * u).astype(BF16), wd_ref[...])


def _post0_mlp_kernel(x_ref, yf_ref, yb_ref, g_ref, bon_ref, att_ref, ln_ref, wo_ref, ones_ref, gain_ref,
                      gain_next_ref, wu_ref, wd_ref, o_ref, hn_next_ref, hn_ref):
    @pl.when(pl.program_id(1) == 0)
    def _():
        ones = ones_ref[...]
        y = yf_ref[...] + yb_ref[...]
        mean = _seg_sum(y, ones) * (1.0 / HEAD)
        d = y - mean
        var = _seg_sum(d * d, ones) * (1.0 / HEAD)
        yn = d * lax.rsqrt(var + GN_EPS) * ln_ref[0:1] + ln_ref[1:2]
        ya = ((yn + bon_ref[...]) * g_ref[...]).astype(BF16)
        mix = _dot(ya, wo_ref[0:512, :]) + _dot(att_ref[...].astype(BF16), wo_ref[512:1024, :])
        x1 = x_ref[...] + mix
        o_ref[...] = x1
        hn_ref[...] = _rms(x1, gain_ref[...]).astype(BF16)

    _mlp_step(hn_ref, wu_ref, wd_ref, o_ref)

    @pl.when(pl.program_id(1) == pl.num_programs(1) - 1)
    def _():
        hn_next_ref[...] = _rms(o_ref[...], gain_next_ref[...])


def _mlp_specs(tm):
    row = lambda i, j: (i, 0)
    const = lambda i, j: (0, 0)
    weights = [pl.BlockSpec((D_MODEL, MLP_FF_CHUNK), lambda i, j: (0, j)),
               pl.BlockSpec((MLP_FF_CHUNK, D_MODEL), lambda i, j: (j, 0))]
    return row, const, weights


def _post0_mlp(x2d, yf, yb, g, bon, att, ln, wo, ones_bd, gain, gain_next, w_up, w_down):
    N = x2d.shape[0]
    tm = min(MLP_TILE, N)
    row, const, weights = _mlp_specs(tm)
    wide = pl.BlockSpec((tm, D_MODEL), row)
    vec = pl.BlockSpec((1, D_MODEL), const)
    shp = jax.ShapeDtypeStruct(x2d.shape, F32)
    return pl.pallas_call(
        _post0_mlp_kernel,
        grid=(N // tm, D_FF // MLP_FF_CHUNK),
        in_specs=[wide] + [pl.BlockSpec((tm, 512), row)] * 5 + [
            pl.BlockSpec((8, 512), const), pl.BlockSpec((D_MODEL, D_MODEL), const), pl.BlockSpec((512, 512), const),
            vec, vec] + weights,
        out_specs=[wide, wide],
        out_shape=[shp, shp],
        scratch_shapes=[pltpu.VMEM((tm, D_MODEL), BF16)],
        compiler_params=_params(("parallel", "arbitrary")),
        name="post0_mlp",
    )(x2d, yf, yb, g, bon, att, ln, wo, ones_bd, gain, gain_next, w_up, w_down)


def _cmul_add(ar, ai, br, bi, cr, ci):
    return ar * br - ai * bi + cr, ar * bi + ai * br + ci


S5_SEGMENTS = 8


def _s5_scan(zs_ref, ps_ref, lam, pw_ref, fwd, nc):
    nl = zs_ref.shape[1]
    W = nl * 128
    ls = nc // S5_SEGMENTS
    ns = S5_SEGMENTS

    def gather(c, k):
        return jnp.concatenate([zs_ref[c, l, pl.ds(k, ns, stride=ls), :] for l in range(nl)], axis=1)

    def load(c, k):
        return jnp.concatenate([ps_ref[c, l, k * ns:(k + 1) * ns, :] for l in range(nl)], axis=1)

    def store(c, k, v):
        for l in range(nl):
            ps_ref[c, l, k * ns:(k + 1) * ns, :] = v[:, l * 128:(l + 1) * 128]

    xr = xi = jnp.zeros((ns, W), F32)
    for k in (range(ls) if fwd else range(ls - 1, -1, -1)):
        store(0, k, xr)
        store(1, k, xi)
        xr, xi = _cmul_add(lam[0:1], lam[1:2], xr, xi, gather(0, k), gather(1, k))
    cr = ci = jnp.zeros((1, W), F32)
    rows_r = [None] * S5_SEGMENTS
    rows_i = [None] * S5_SEGMENTS
    for s in (range(S5_SEGMENTS) if fwd else range(S5_SEGMENTS - 1, -1, -1)):
        rows_r[s] = cr
        rows_i[s] = ci
        cr, ci = _cmul_add(lam[2:3], lam[3:4], cr, ci, xr[s:s + 1], xi[s:s + 1])
    car_r = jnp.concatenate(rows_r, axis=0)
    car_i = jnp.concatenate(rows_i, axis=0)
    for k in range(ls):
        pr, pi = _cmul_add(pw_ref[0, k:k + 1, :], pw_ref[1, k:k + 1, :], car_r, car_i, load(0, k), load(1, k))
        store(0, k, pr)
        store(1, k, pi)


def _s5_kernel(x_ref, g2_ref, wz_ref, w2_ref, lam_ref, pw_ref, y_ref, lhs_ref, zs_ref, ps_ref, *, nc):
    C = S5_CHUNK
    W = S5_SLAB * S5_STATE
    nl = W // 128
    for j in range(C):
        lhs_ref[:, j * 128:(j + 1) * 128] = x_ref[pl.ds(j, nc, stride=C), :].astype(BF16)
    lhs = lhs_ref[...]
    for d in range(2):
        z = _dot(lhs, wz_ref[0, :, d * 2 * W:(d + 1) * 2 * W])
        for c in range(2):
            for l in range(nl):
                zs_ref[d, c, l] = z[:, c * W + l * 128:c * W + (l + 1) * 128]
        _s5_scan(zs_ref.at[d], ps_ref.at[d], lam_ref[0, d], pw_ref.at[0, d], d == 0, nc)
    ls = nc // S5_SEGMENTS

    def chunk_order(ref):
        return jnp.concatenate([ref[pl.ds(s, ls, stride=S5_SEGMENTS), :] for s in range(S5_SEGMENTS)], axis=0)

    pv = jnp.concatenate([chunk_order(ps_ref.at[d, c, l]) for d in range(2) for c in range(2) for l in range(nl)],
                         axis=1).astype(BF16)
    for i in range(0, C, 2):
        w_loc = g2_ref[0, (C - 1 - i) * 128:(2 * C - 1 - i) * 128, :]
        y2 = _dot(lhs, w_loc) + _dot(pv, w2_ref[0, :, i * 128:(i + 2) * 128])
        y_ref[pl.ds(i, nc, stride=C), :] = y2[:, :128]
        y_ref[pl.ds(i + 1, nc, stride=C), :] = y2[:, 128:]


def _s5_core(hn, g2, wz, w2, lam, pw, n_seq, T):
    nc = T // S5_CHUNK
    ls = nc // S5_SEGMENTS
    W = S5_SLAB * S5_STATE
    n_slab = D_MODEL // 128
    once = pl.Buffered(1)
    slab = lambda c, s: (c, 0, 0)
    return pl.pallas_call(
        functools.partial(_s5_kernel, nc=nc),
        grid=(n_slab, n_seq),
        in_specs=[
            pl.BlockSpec((T, 128), lambda c, s: (s, c)),
            pl.BlockSpec((1, 2 * S5_CHUNK * 128, 256), slab, pipeline_mode=once),
            pl.BlockSpec((1, S5_CHUNK * 128, 4 * W), slab, pipeline_mode=once),
            pl.BlockSpec((1, 4 * W, S5_CHUNK * 128), slab, pipeline_mode=once),
            pl.BlockSpec((1, 2, 8, W), lambda c, s: (c, 0, 0, 0)),
            pl.BlockSpec((1, 2, 2, ls, W), lambda c, s: (c, 0, 0, 0, 0)),
        ],
        out_specs=pl.BlockSpec((T, 128), lambda c, s: (s, c)),
        out_shape=jax.ShapeDtypeStruct(hn.shape, F32),
        scratch_shapes=[
            pltpu.VMEM((nc, S5_CHUNK * 128), BF16),
            pltpu.VMEM((2, 2, W // 128, nc, 128), F32), pltpu.VMEM((2, 2, W // 128, nc, 128), F32)],
        compiler_params=_params(("parallel", "arbitrary")),
        name="s5_core",
    )(hn, g2, wz, w2, lam, pw)


def _post1_mlp_kernel(x_ref, ys_ref, gain_ref, d_ref, wg_ref, bg_ref, gain2_ref, wu_ref, wd_ref, o_ref, hn_ref):
    @pl.when(pl.program_id(1) == 0)
    def _():
        x = x_ref[...]
        y = _rms(x, gain_ref[...]) * d_ref[...] + ys_ref[...]
        z = 0.5 * y * (1.0 + jnp.tanh(math.sqrt(2.0 / math.pi) * (y + 0.044715 * (y * y * y))))
        gate = _sigmoid(_dot(z.astype(BF16), wg_ref[...]) + bg_ref[...])
        x1 = x + z * gate
        o_ref[...] = x1
        hn_ref[...] = _rms(x1, gain2_ref[...]).astype(BF16)

    _mlp_step(hn_ref, wu_ref, wd_ref, o_ref)


def _post1_mlp(x2d, ys, gain, d, wg, bg, gain2, w_up, w_down):
    N = x2d.shape[0]
    tm = min(MLP_TILE, N)
    row, const, weights = _mlp_specs(tm)
    wide = pl.BlockSpec((tm, D_MODEL), row)
    vec = pl.BlockSpec((1, D_MODEL), const)
    return pl.pallas_call(
        _post1_mlp_kernel,
        grid=(N // tm, D_FF // MLP_FF_CHUNK),
        in_specs=[wide, wide, vec, vec, pl.BlockSpec((D_MODEL, D_MODEL), const), vec, vec] + weights,
        out_specs=wide,
        out_shape=jax.ShapeDtypeStruct(x2d.shape, F32),
        scratch_shapes=[pltpu.VMEM((tm, D_MODEL), BF16)],
        compiler_params=_params(("parallel", "arbitrary")),
        name="post1_mlp",
    )(x2d, ys, gain, d, wg, bg, gain2, w_up, w_down)


def _s5_tables(p, T):
    C, G, P, SL = S5_CHUNK, S5_GROUPS, S5_STATE, S5_SLAB
    ns = G // SL
    ls = T // C // S5_SEGMENTS
    eye = jnp.eye(SL, dtype=F32)
    b_re = p['s5_b_re'][0].astype(F32)
    b_im = p['s5_b_im'][0].astype(F32)
    steps = jnp.arange(C, dtype=F32)

    def direction(sfx):
        lr = p['s5_lam_re_' + sfx][0].astype(F32)
        li = p['s5_lam_im_' + sfx][0].astype(F32)
        dt = jnp.exp(p['s5_log_dt_' + sfx][0].astype(F32))[:, None]

        def power(k):
            k = k[:, None, None]
            mag = jnp.exp(lr * dt * k)
            return mag * jnp.cos(li * dt * k), mag * jnp.sin(li * dt * k)

        l1r, l1i = power(jnp.ones((1,), F32))
        nr, ni = l1r[0] - 1.0, l1i[0]
        den = lr * lr + li * li
        cr = (nr * lr + ni * li) / den
        ci = (ni * lr - nr * li) / den
        cb_r = cr[:, :, None] * b_re - ci[:, :, None] * b_im
        cb_i = cr[:, :, None] * b_im + ci[:, :, None] * b_re
        c_r = p['s5_c_re_' + sfx][0].astype(F32)
        c_i = p['s5_c_im_' + sfx][0].astype(F32)
        return power, cb_r, cb_i, c_r, c_i

    def kernels(power, cb_r, cb_i, c_r, c_i):
        pr, pi = power(steps)
        d_r = pr[..., None] * cb_r - pi[..., None] * cb_i
        d_i = pr[..., None] * cb_i + pi[..., None] * cb_r
        k = jnp.einsum('gop,lgpi->lgio', c_r, d_r) - jnp.einsum('gop,lgpi->lgio', c_i, d_i)
        k = k.reshape(C, ns, SL, S5_GROUP, 1, S5_GROUP)
        return jnp.where(eye[:, None, :, None] > 0, k, 0.0).reshape(C, ns, 128, 128)

    def state_in(power, cb_r, cb_i, ks):
        pr, pi = power(ks)
        w_r = (pr[..., None] * cb_r - pi[..., None] * cb_i).reshape(C, ns, SL, P, S5_GROUP)
        w_i = (pr[..., None] * cb_i + pi[..., None] * cb_r).reshape(C, ns, SL, P, S5_GROUP)
        return [w_r, w_i]

    def state_out(power, c_r, c_i, ks):
        pr, pi = power(ks)
        e_r = (c_r[None] * pr[:, :, None, :] - c_i[None] * pi[:, :, None, :]).reshape(C, ns, SL, S5_GROUP, P)
        e_i = (c_r[None] * pi[:, :, None, :] + c_i[None] * pr[:, :, None, :]).reshape(C, ns, SL, S5_GROUP, P)
        return [e_r, -e_i]

    def scan_tables(power, ks_rows):
        lr_, li_ = power(C * jnp.array([1.0, ls], F32))
        lam = jnp.stack([lr_, li_], axis=1).reshape(4, ns, SL * P)
        lam = jnp.concatenate([lam, jnp.zeros_like(lam)], axis=0).transpose(1, 0, 2)
        pr, pi = power(C * ks_rows)
        pw = jnp.stack([pr.reshape(ls, ns, SL * P), pi.reshape(ls, ns, SL * P)], axis=0).transpose(2, 0, 1, 3)
        return lam, pw

    def expand(x, row_group, col_group):
        src = jnp.arange(2048) // (col_group * SL) * col_group + jnp.arange(2048) % col_group
        spread = (jnp.arange(256)[:, None] == src[None, :]).astype(BF16)
        full = jnp.einsum('srk,kc->src', x.astype(BF16), spread, preferred_element_type=F32)
        same = ((jnp.arange(2048) // row_group) % SL)[:, None] == ((jnp.arange(2048) // col_group) % SL)[None, :]
        return jnp.where(same, full, 0.0).astype(BF16)

    pf = direction('f')
    pb = direction('b')
    kf = kernels(*pf)
    kb = kernels(*pb)
    zero = jnp.zeros((1, ns, 128, 128), F32)
    gen = jnp.concatenate([kf[:0:-1], (kf[0] + kb[0])[None], kb[1:], zero], axis=0)
    gen_prev = jnp.concatenate([zero, gen[:-1]], axis=0)
    g2 = jnp.concatenate([gen, gen_prev], axis=-1).transpose(1, 0, 2, 3).reshape(ns, 2 * C * 128, 256)

    w4 = jnp.stack(state_in(pf[0], pf[1], pf[2], (C - 1) - steps) + state_in(pb[0], pb[1], pb[2], steps))
    wz = expand(w4.transpose(2, 1, 3, 5, 0, 4).reshape(ns, C * 128, 4 * P), S5_GROUP, P)
    e4 = jnp.stack(state_out(pf[0], pf[3], pf[4], steps + 1.0) + state_out(pb[0], pb[3], pb[4], C - steps))
    w2 = expand(e4.transpose(2, 0, 3, 5, 1, 4).reshape(ns, 4 * SL * P, C * S5_GROUP), P, S5_GROUP)
    pos = jnp.arange(ls, dtype=F32)
    lam_f, pw_f = scan_tables(pf[0], pos)
    lam_b, pw_b = scan_tables(pb[0], (ls - 1.0) - pos)
    lam = jnp.stack([lam_f, lam_b], axis=1)
    pw = jnp.stack([pw_f, pw_b], axis=1)
    return g2.astype(BF16), wz, w2, lam, pw


def _rope_tables(T):
    rows = T // GRID_W
    row_ids = np.repeat(np.arange(rows, dtype=np.float32), GRID_W)
    col_ids = np.tile(np.arange(GRID_W, dtype=np.float32), rows)
    pairs = HEAD // 4
    inv_freq = (np.float32(ROPE_THETA) ** (-np.arange(pairs, dtype=np.float32) / pairs)).astype(np.float32)
    ang = np.concatenate([row_ids[:, None] * inv_freq, col_ids[:, None] * inv_freq], axis=-1)
    cos = np.repeat(np.cos(ang), 2, axis=-1)
    sin = np.repeat(np.sin(ang), 2, axis=-1)
    sign = np.tile(np.array([-1.0, 1.0], np.float32), HEAD // 2)
    return jnp.asarray(np.tile(cos, (1, 2)), F32), jnp.asarray(np.tile(sin * sign, (1, 2)), F32)


def _layer0_weights(p, T):
    w_in = p['hyb_w_in'][0]
    zc = jnp.zeros((D_MODEL, 64), F32)
    w_all = jnp.concatenate([w_in[:, 0:1728], zc, w_in[:, 1728:2624]], axis=1).astype(BF16)
    mu = p['hyb_shift_mu'][0]
    mu_all = jnp.concatenate([mu[0:1728], jnp.zeros((64,), F32), mu[1728:1856]]).reshape(1, RW_COLS)
    wup = jnp.zeros((384, 2048), F32)
    wup = wup.at[0:64, 0:512].set(p['rwkv_w_up_f'][0])
    wup = wup.at[64:128, 512:1024].set(p['rwkv_w_up_b'][0])
    wup = wup.at[128:192, 1024:1536].set(p['rwkv_a_up'][0])
    wup = wup.at[256:384, 1536:2048].set(p['rwkv_g_up'][0])
    wup = wup.astype(BF16)
    zr = jnp.zeros((512,), F32)
    par = jnp.stack([p['rwkv_w0_f'][0], p['rwkv_w0_b'][0], p['rwkv_a0'][0], p['rwkv_k_k'][0],
                     p['rwkv_k_a'][0], p['rwkv_r_k'][0].reshape(-1), zr, zr]).astype(F32)
    qg = jnp.tile(p['att_q_norm'][0], 8).reshape(1, 512).astype(F32)
    kg = jnp.tile(p['att_k_norm'][0], 2).reshape(1, 128).astype(F32)
    cos_t, sin_t = _rope_tables(T)
    seg = jnp.arange(512) // HEAD
    ones_bd = (seg[:, None] == seg[None, :]).astype(BF16)
    ln = jnp.stack([p['rwkv_lnx_g'][0], p['rwkv_lnx_b'][0]] + [zr] * 6).astype(F32)
    return dict(w_all=w_all, mu_all=mu_all, wup=wup, par=par, qg=qg, kg=kg, cos_t=cos_t, sin_t=sin_t,
                ones_bd=ones_bd, ln=ln, wo=p['hyb_w_out'][0].astype(BF16))


def _row(v):
    return v.reshape(1, -1).astype(F32)


def _layer0(x2d, p, w, ffn_w, n_seq, T):
    r, k, v, a, b, lf, lb, g, bon, q, k_rep, v_t = _pre0(
        x2d, T, _row(p['mix_norm'][0]), w['w_all'], w['mu_all'], w['wup'], w['par'], w['qg'], w['kg'],
        w['cos_t'], w['sin_t'], w['ones_bd'])
    yf, yb = _wkv(r, k, v, a, b, lf, lb, n_seq, T)
    att = _attention(q, k_rep, v_t, n_seq, T)
    return _post0_mlp(x2d, yf, yb, g, bon, att, w['ln'], w['wo'], w['ones_bd'], _row(p['ffn_norm'][0]),
                      _row(p['mix_norm'][1]), ffn_w[0][0], ffn_w[0][1])


def _layer1(x2d, hn, p, s5, ffn_w, n_seq, T):
    g2, wz, w2, lam, pw = s5
    ys = _s5_core(hn, g2, wz, w2, lam, pw, n_seq, T)
    return _post1_mlp(x2d, ys, _row(p['mix_norm'][1]), _row(p['s5_d'][0]), p['s5_glu_w'][0].astype(BF16),
                      _row(p['s5_glu_b'][0]), _row(p['ffn_norm'][1]), ffn_w[1][0], ffn_w[1][1])


def _prepare(p, T):
    ffn_w = [(p['ffn_up'][l].astype(BF16), p['ffn_down'][l].astype(BF16)) for l in range(2)]
    return _layer0_weights(p, T), _s5_tables(p, T), ffn_w


def _trunk(x, p, prep=None):
    n_seq, T, _ = x.shape
    w0, s5, ffn_w = _prepare(p, T) if prep is None else prep
    x2d = x.reshape(n_seq * T, D_MODEL)
    x2d, hn = _layer0(x2d, p, w0, ffn_w, n_seq, T)
    x2d = _layer1(x2d, hn, p, s5, ffn_w, n_seq, T)
    return x2d.reshape(n_seq, T, D_MODEL)


def kernel(x_prompt, x_sample, mix_norm, ffn_norm, ffn_up, ffn_down, hyb_w_in, hyb_shift_mu, rwkv_w0_f, rwkv_w_up_f, rwkv_w0_b, rwkv_w_up_b, rwkv_a0, rwkv_a_up, rwkv_g_up, rwkv_k_k, rwkv_k_a, rwkv_r_k, rwkv_lnx_g, rwkv_lnx_b, att_q_norm, att_k_norm, hyb_w_out, s5_lam_re_f, s5_lam_im_f, s5_log_dt_f, s5_lam_re_b, s5_lam_im_b, s5_log_dt_b, s5_b_re, s5_b_im, s5_c_re_f, s5_c_im_f, s5_c_re_b, s5_c_im_b, s5_d, s5_glu_w, s5_glu_b):
    p = dict(mix_norm=mix_norm, ffn_norm=ffn_norm, ffn_up=ffn_up, ffn_down=ffn_down,
             hyb_w_in=hyb_w_in, hyb_shift_mu=hyb_shift_mu,
             rwkv_w0_f=rwkv_w0_f, rwkv_w_up_f=rwkv_w_up_f, rwkv_w0_b=rwkv_w0_b, rwkv_w_up_b=rwkv_w_up_b,
             rwkv_a0=rwkv_a0, rwkv_a_up=rwkv_a_up, rwkv_g_up=rwkv_g_up,
             rwkv_k_k=rwkv_k_k, rwkv_k_a=rwkv_k_a, rwkv_r_k=rwkv_r_k,
             rwkv_lnx_g=rwkv_lnx_g, rwkv_lnx_b=rwkv_lnx_b,
             att_q_norm=att_q_norm, att_k_norm=att_k_norm, hyb_w_out=hyb_w_out,
             s5_lam_re_f=s5_lam_re_f, s5_lam_im_f=s5_lam_im_f, s5_log_dt_f=s5_log_dt_f,
             s5_lam_re_b=s5_lam_re_b, s5_lam_im_b=s5_lam_im_b, s5_log_dt_b=s5_log_dt_b,
             s5_b_re=s5_b_re, s5_b_im=s5_b_im,
             s5_c_re_f=s5_c_re_f, s5_c_im_f=s5_c_im_f, s5_c_re_b=s5_c_re_b, s5_c_im_b=s5_c_im_b,
             s5_d=s5_d, s5_glu_w=s5_glu_w, s5_glu_b=s5_glu_b)
    assert x_prompt.shape[1] == x_sample.shape[1]
    prep = _prepare(p, x_prompt.shape[1])
    return (_trunk(x_prompt, p, prep), _trunk(x_sample, p, prep))
```

```python
import functools
import math

import jax
import jax.numpy as jnp
import numpy as np
from jax import lax
from jax.experimental import pallas as pl
from jax.experimental.pallas import tpu as pltpu

F32 = jnp.float32
BF16 = jnp.bfloat16

D_MODEL = 1024
D_FF = 4 * D_MODEL
RMS_EPS = 1e-6
GRID_W = 64
RWKV_DIM = 512
HEAD = 64
GN_EPS = 64e-5
N_KV = 2
ROPE_THETA = 10000.0
S5_GROUP = 16
S5_GROUPS = D_MODEL // S5_GROUP
S5_STATE = 64
S5_SLAB = 128 // S5_GROUP

WKV_CHUNK = 64
WKV_STEP = 2
S5_CHUNK = 16
RW_COLS = 1920
ALL_COLS = 2688
EXP_M05 = math.exp(-0.5)
LOG2_E = math.log2(math.e)
V_ROWS = HEAD + 16
VMEM_LIMIT = 56 * 1024 * 1024


def _dot(a, b):
    return jnp.dot(a, b, preferred_element_type=F32)


def _dot_nt(a, b):
    return lax.dot_general(a, b, (((1,), (1,)), ((), ())), preferred_element_type=F32)


def _dot_tn(a, b):
    return lax.dot_general(a, b, (((0,), (0,)), ((), ())), preferred_element_type=F32)


def _split2(x):
    hi = x.astype(BF16)
    lo = (x - hi.astype(F32)).astype(BF16)
    return hi, lo


def _seg_sum(x, ones_bd, split=True):
    if not split:
        return _dot(x.astype(BF16), ones_bd)
    hi, lo = _split2(x)
    return _dot(hi, ones_bd) + _dot(lo, ones_bd)


def _rms(x, gain):
    return x * lax.rsqrt(jnp.mean(x * x, axis=-1, keepdims=True) + RMS_EPS) * gain


def _sigmoid(x):
    return 1.0 / (1.0 + jnp.exp(-x))


def _params(sem):
    return pltpu.CompilerParams(dimension_semantics=sem, vmem_limit_bytes=VMEM_LIMIT)


def _pre0_kernel(x_ref, xp_ref, xn_ref, gain_ref, w_ref, mu_ref, wup_ref, par_ref, qg_ref, kg_ref,
                 cos_ref, sin_ref, ones_ref,
                 r_o, k_o, v_o, a_o, b_o, lf_o, lb_o, g_o, bon_o, q_o, ka_o, va_o,
                 *, tiles_per_seq, tm):
    pos = pl.program_id(0) % tiles_per_seq
    gain = gain_ref[...]
    xp = jnp.where(pos == 0, 0.0, xp_ref[...])
    xn = jnp.where(pos == tiles_per_seq - 1, 0.0, xn_ref[...])
    x_all = jnp.concatenate([xp, x_ref[...], xn], axis=0)
    H_all = _dot(_rms(x_all, gain).astype(BF16), w_ref[...])
    H = H_all[8:8 + tm]
    Hr = H[:, :RW_COLS]
    Hr_all = H_all[:, :RW_COLS]
    prev = pltpu.roll(Hr_all, 1, 0)[8:8 + tm]
    nxt = pltpu.roll(Hr_all, tm + 15, 0)[8:8 + tm]
    Hs = Hr + mu_ref[...] * (0.5 * (prev + nxt) - Hr)

    ones = ones_ref[...]
    par = par_ref[...]
    r = Hs[:, 0:512]
    k = Hs[:, 512:1024]
    v = Hs[:, 1024:1536]
    up_w = _dot(jnp.tanh(Hs[:, 1536:1664]).astype(BF16), wup_ref[0:128, 0:1024])
    up_a = _dot(Hs[:, 1664:1792].astype(BF16), wup_ref[128:256, 1024:1536])
    lf_o[...] = -EXP_M05 * _sigmoid(par[0:1] + up_w[:, 0:512])
    lb_o[...] = -EXP_M05 * _sigmoid(par[1:2] + up_w[:, 512:1024])
    a_sig = _sigmoid(par[2:3] + up_a)
    g_o[...] = _dot(_sigmoid(Hs[:, 1792:1920]).astype(BF16), wup_ref[256:384, 1536:2048])
    kk = k * par[3:4]
    kk = kk / jnp.maximum(jnp.sqrt(_seg_sum(kk * kk, ones, split=False)), 1e-12)
    k2 = k * (1.0 + (a_sig - 1.0) * par[4:5])
    r_o[...] = r
    k_o[...] = k2
    v_o[...] = v
    a_o[...] = -kk
    b_o[...] = kk * a_sig
    bon_o[...] = _seg_sum(r * k2 * par[5:6], ones) * v

    qa = H[:, RW_COLS:RW_COLS + 512]
    ka = H[:, RW_COLS + 512:RW_COLS + 640]
    va = H[:, RW_COLS + 640:RW_COLS + 768]
    cos = cos_ref[...]
    sin = sin_ref[...]

    def rope(x, c, s):
        n = x.shape[1]
        lane = lax.broadcasted_iota(jnp.int32, x.shape, 1)
        swapped = jnp.where(lane % 2 == 0, pltpu.roll(x, n - 1, 1), pltpu.roll(x, 1, 1))
        return x * c + swapped * s

    qn = qa * lax.rsqrt(_seg_sum(qa * qa, ones, split=False) * (1.0 / HEAD) + RMS_EPS) * qg_ref[...]
    qr = rope(qn, jnp.concatenate([cos] * 4, axis=1), jnp.concatenate([sin] * 4, axis=1))
    q_o[...] = (qr * (HEAD ** -0.5 * LOG2_E)).astype(BF16)
    kn = ka * lax.rsqrt(_seg_sum(ka * ka, ones[:128, :128], split=False) * (1.0 / HEAD) + RMS_EPS) * kg_ref[...]
    kr = rope(kn, cos, sin)
    lt64 = lax.broadcasted_iota(jnp.int32, kr.shape, 1) < HEAD

    def rep(x):
        sw = pltpu.roll(x, HEAD, 1)
        return jnp.concatenate([jnp.where(lt64, x, sw), jnp.where(lt64, sw, x)], axis=1).astype(BF16)

    ka_o[...] = rep(kr)
    vt = va.T.astype(BF16)
    one = jnp.ones((V_ROWS - HEAD, tm), BF16)
    va_o[0] = jnp.concatenate([vt[0:HEAD], one, vt[HEAD:2 * HEAD], one], axis=0)


def _pre0(x2d, T, gain, w_all, mu_all, wup, par, qg, kg, cos_t, sin_t, ones_bd):
    N = x2d.shape[0]
    tm = min(512, T)
    tps = T // tm
    nt = N // tm
    t8 = tm // 8
    nb8 = N // 8
    const = lambda i: (0, 0)
    tile = lambda i: (i, 0)
    f512 = jax.ShapeDtypeStruct((N, 512), F32)
    out_shape = [f512] * 9 + [jax.ShapeDtypeStruct((N, 512), BF16), jax.ShapeDtypeStruct((N, 256), BF16),
                              jax.ShapeDtypeStruct((N // T, 2 * V_ROWS, T), BF16)]
    out_specs = [pl.BlockSpec((tm, 512), tile)] * 10 + [
        pl.BlockSpec((tm, 256), tile), pl.BlockSpec((1, 2 * V_ROWS, tm), lambda i: (i // tps, 0, i % tps))]
    return pl.pallas_call(
        functools.partial(_pre0_kernel, tiles_per_seq=tps, tm=tm),
        grid=(nt,),
        in_specs=[
            pl.BlockSpec((tm, D_MODEL), tile),
            pl.BlockSpec((8, D_MODEL), lambda i: (jnp.maximum(i * t8 - 1, 0), 0)),
            pl.BlockSpec((8, D_MODEL), lambda i: (jnp.minimum((i + 1) * t8, nb8 - 1), 0)),
            pl.BlockSpec((1, D_MODEL), const),
            pl.BlockSpec((D_MODEL, ALL_COLS), const),
            pl.BlockSpec((1, RW_COLS), const),
            pl.BlockSpec((384, 2048), const),
            pl.BlockSpec((8, 512), const),
            pl.BlockSpec((1, 512), const),
            pl.BlockSpec((1, 128), const),
            pl.BlockSpec((tm, 128), lambda i: (i % tps, 0)),
            pl.BlockSpec((tm, 128), lambda i: (i % tps, 0)),
            pl.BlockSpec((512, 512), const),
        ],
        out_specs=out_specs,
        out_shape=out_shape,
        compiler_params=_params(("parallel",)),
        name="pre0",
    )(x2d, x2d, x2d, gain, w_all, mu_all, wup, par, qg, kg, cos_t, sin_t, ones_bd)


def _wkv_direction(r, k, v, a, b, L, h_ref, fwd):
    C = WKV_CHUNK
    Q = 4 * HEAD
    row = lax.broadcasted_iota(jnp.int32, L.shape, 0)
    cs = L
    for lvl in range(int(math.log2(C))):
        cs = cs + jnp.where(row >= 2 ** lvl, pltpu.roll(cs, 2 ** lvl, 0), 0.0)
    total = cs[C - 1:C, :]
    if fwd:
        cs_incl = cs
        cs_excl = cs - L
    else:
        cs_incl = total - (cs - L)
        cs_excl = total - cs
    e_incl = jnp.exp(cs_incl)
    e_inv = jnp.exp(-cs_incl)
    e_rem = jnp.exp(total - cs_incl)
    a_t = a * jnp.exp(cs_excl)
    r_t = r * e_incl
    b_t = b * e_inv
    k_t = k * e_inv
    b_h = b * e_rem
    k_h = k * e_rem
    gam = jnp.exp(total)

    streams = []
    for q in range(RWKV_DIM // Q):
        sl = slice(q * Q, (q + 1) * Q)
        streams.append(dict(
            q=q, h_ref=h_ref, fwd=fwd, gam=gam[:, sl], a_t=a_t[:, sl], r_t=r_t[:, sl], b_t=b_t[:, sl],
            k_t=k_t[:, sl], v=v[:, sl], b_h=b_h[:, sl], k_h=k_h[:, sl]))
    return streams


def _wkv_solve(streams):
    C = WKV_CHUNK
    Q = 4 * HEAD
    lane_q = lax.broadcasted_iota(jnp.int32, (C, Q), 1) // HEAD
    ri = lax.broadcasted_iota(jnp.int32, (Q, Q), 0)
    ci = lax.broadcasted_iota(jnp.int32, (Q, Q), 1)
    same = (ri // C) == (ci // C)
    tw = lax.broadcasted_iota(jnp.int32, (C, Q), 0)
    sw = lax.broadcasted_iota(jnp.int32, (C, Q), 1) % C
    masks = {True: (sw < tw, sw <= tw), False: (sw > tw, sw >= tw)}
    eye_w = sw == tw

    def stack(xq):
        return jnp.concatenate([jnp.where(lane_q == h, xq, 0.0) for h in range(4)], axis=0).astype(BF16)

    def blockdiag(w):
        return jnp.where(same, jnp.concatenate([w] * 4, axis=0), 0.0).astype(BF16)

    for s in streams:
        strict, incl = masks[s['fwd']]
        P = _dot_nt(jnp.concatenate([s['a_t'], s['r_t']], axis=0).astype(BF16),
                    jnp.concatenate([stack(s['b_t']), stack(s['k_t'])], axis=0))
        A_ab = jnp.where(strict, P[:C, :Q], 0.0)
        s['A_k'] = jnp.concatenate([jnp.where(strict, P[:C, Q:], 0.0), jnp.where(incl, P[C:, Q:], 0.0)],
                                   axis=0).astype(BF16)
        s['A_rb'] = jnp.where(incl, P[C:, :Q], 0.0).astype(BF16)
        s['T'] = jnp.where(eye_w, 1.0, A_ab)
        s['Ap'] = _dot(A_ab.astype(BF16), blockdiag(A_ab))
        s['Vs'] = stack(s['v'])
    levels = int(math.log2(C))
    for lvl in range(1, levels):
        for s in streams:
            rhs = blockdiag(s['Ap'])
            if lvl < levels - 1:
                both = _dot(jnp.concatenate([s['T'], s['Ap']], axis=0).astype(BF16), rhs)
                s['T'] = s['T'] + both[:C]
                s['Ap'] = both[C:]
            else:
                s['T'] = s['T'] + _dot(s['T'].astype(BF16), rhs)
    for s in streams:
        zy = _dot(s['A_k'], s['Vs'])
        s['Z'] = zy[:C]
        s['y_k'] = zy[C:]
        s['n_k'] = _dot_tn(s['k_h'].astype(BF16), s['v'].astype(BF16))
    for s in streams:
        s['X'] = _dot(s['T'].astype(BF16), jnp.concatenate([stack(s['a_t']), stack(s['Z'])], axis=1))
    for s in streams:
        X = s['X']
        W1 = _dot(s['A_rb'], jnp.concatenate([stack(X[:, :Q]), stack(X[:, Q:])], axis=1))
        MN = _dot_tn(s['b_h'].astype(BF16), X.astype(BF16))
        s['r_p'] = s['r_t'] + W1[:, :Q]
        s['y_p'] = W1[:, Q:] + s['y_k']
        s['M'] = jnp.where(same, MN[:, :Q], 0.0)
        s['N'] = jnp.where(same, MN[:, Q:] + s['n_k'], 0.0)
    ys = []
    for s in streams:
        h0 = s['h_ref'][s['q']]
        gam_col = jnp.sum(jnp.where(ri == ci, s['gam'], 0.0), axis=1, keepdims=True)
        seq = _dot(jnp.concatenate([s['r_p'], s['M']], axis=0).astype(BF16), h0.astype(BF16))
        ys.append(seq[:C] + s['y_p'])
        s['h_ref'][s['q']] = gam_col * h0 + seq[C:] + s['N']
    return ys


def _wkv_kernel(rf, kf, vf, af, bf, lf, rb, kb, vb, ab, bb, lb, yf_o, yb_o, hf_ref, hb_ref):
    @pl.when(pl.program_id(1) == 0)
    def _():
        hf_ref[...] = jnp.zeros_like(hf_ref)
        hb_ref[...] = jnp.zeros_like(hb_ref)

    C = WKV_CHUNK
    streams = []
    for refs, h_ref, fwd in (((rf, kf, vf, af, bf, lf), hf_ref, True), ((rb, kb, vb, ab, bb, lb), hb_ref, False)):
        for n in (range(WKV_STEP) if fwd else range(WKV_STEP - 1, -1, -1)):
            streams += _wkv_direction(*(x[n * C:(n + 1) * C, :] for x in refs), h_ref, fwd)
    ys = _wkv_solve(streams)
    for n in range(WKV_STEP):
        f0 = 2 * n
        b0 = 2 * WKV_STEP + 2 * (WKV_STEP - 1 - n)
        yf_o[n * C:(n + 1) * C, :] = jnp.concatenate(ys[f0:f0 + 2], axis=1)
        yb_o[n * C:(n + 1) * C, :] = jnp.concatenate(ys[b0:b0 + 2], axis=1)


def _wkv(r, k, v, a, b, lf, lb, n_seq, T):
    C = WKV_CHUNK * WKV_STEP
    nc = T // C
    fw = lambda s, i: (s * nc + i, 0)
    bw = lambda s, i: (s * nc + nc - 1 - i, 0)
    spec_f = pl.BlockSpec((C, 512), fw)
    spec_b = pl.BlockSpec((C, 512), bw)
    shp = jax.ShapeDtypeStruct(r.shape, F32)
    return pl.pallas_call(
        _wkv_kernel,
        grid=(n_seq, nc),
        in_specs=[spec_f] * 6 + [spec_b] * 6,
        out_specs=[spec_f, spec_b],
        out_shape=[shp, shp],
        scratch_shapes=[pltpu.VMEM((2, 256, 256), F32), pltpu.VMEM((2, 256, 256), F32)],
        compiler_params=_params(("parallel", "arbitrary")),
        name="wkv",
    )(r, k, v, a, b, lf, r, k, v, a, b, lb)


def _attn_kernel(q_ref, k_ref, vt_ref, o_ref, s0_ref, s1_ref, p0_ref, p1_ref, acc_ref, *, tq, tk, T):
    q = q_ref[...].astype(F32)
    lo = lax.broadcasted_iota(jnp.int32, (tq, 128), 1) < HEAD
    qs = jnp.concatenate([
        jnp.where(lo, q[:, 0:128], 0.0), jnp.where(lo, 0.0, q[:, 0:128]),
        jnp.where(lo, q[:, 128:256], 0.0), jnp.where(lo, 0.0, q[:, 128:256])], axis=0)
    qst = qs.T.astype(BF16)
    R = 4 * tq
    n = T // tk

    def chunk(j):
        return pl.ds(j * tk if isinstance(j, int) else pl.multiple_of(j * tk, tk), tk)

    def scores(j):
        return _dot(k_ref[chunk(j), :], qst)

    def values(j, p):
        return _dot(vt_ref[0, :, chunk(j)], p)

    s_buf = (s0_ref, s1_ref)
    p_buf = (p0_ref, p1_ref)
    s_buf[0][...] = scores(0)
    p_buf[1][...] = jnp.zeros((tk, R), BF16)
    acc_ref[...] = jnp.zeros((V_ROWS, R), F32)

    def step(j, b, carry, ahead=True):
        m, alpha_prev = carry
        if ahead:
            s_buf[1 - b][...] = scores(j + 1)
        prev = max(j - 1, 0) if isinstance(j, int) else jnp.maximum(j - 1, 0)
        acc_ref[...] = alpha_prev * acc_ref[...] + values(prev, p_buf[1 - b][...])
        s = s_buf[b][...]
        m_new = jnp.maximum(m, jnp.max(s, axis=0, keepdims=True))
        p_buf[b][...] = jnp.exp2(s - m_new).astype(BF16)
        return m_new, jnp.exp2(m - m_new)

    def body(i, carry):
        return step(2 * i + 1, 1, step(2 * i, 0, carry))

    carry = lax.fori_loop(0, n // 2 - 1, body, (jnp.full((1, R), -1e30, F32), jnp.ones((1, R), F32)))
    carry = step(n - 2, 0, carry)
    _, alpha = step(n - 1, 1, carry, ahead=False)
    acc = alpha * acc_ref[...] + values(n - 1, p_buf[1][...])
    ot = acc[0:HEAD] / acc[HEAD:HEAD + 1]
    o01 = jnp.concatenate([ot[:, 0:tq], ot[:, tq:2 * tq]], axis=0).T
    o23 = jnp.concatenate([ot[:, 2 * tq:3 * tq], ot[:, 3 * tq:4 * tq]], axis=0).T
    o_ref[...] = jnp.concatenate([o01, o23], axis=1)


def _attention(q, k_rep, v_t, n_seq, T):
    tq = min(1024, T)
    tk = min(512, T // 2)
    nq = T // tq
    assert (T // tk) % 2 == 0
    return pl.pallas_call(
        functools.partial(_attn_kernel, tq=tq, tk=tk, T=T),
        grid=(n_seq, N_KV, nq),
        in_specs=[
            pl.BlockSpec((tq, 256), lambda s, h, i: (s * nq + i, h)),
            pl.BlockSpec((T, 128), lambda s, h, i: (s, h)),
            pl.BlockSpec((1, V_ROWS, T), lambda s, h, i: (s, h, 0)),
        ],
        out_specs=pl.BlockSpec((tq, 256), lambda s, h, i: (s * nq + i, h)),
        out_shape=jax.ShapeDtypeStruct(q.shape, F32),
        scratch_shapes=[pltpu.VMEM((tk, 4 * tq), F32)] * 2 + [pltpu.VMEM((tk, 4 * tq), BF16)] * 2 + [
            pltpu.VMEM((V_ROWS, 4 * tq), F32)],
        compiler_params=_params(("parallel", "parallel", "arbitrary")),
        name="attention",
    )(q, k_rep, v_t)


MLP_TILE = 512
MLP_FF_CHUNK = 4096


def _mlp_step(hn_ref, wu_ref, wd_ref, o_ref):
    u = jnp.maximum(_dot(hn_ref[...], wu_ref[...]), 0.0)
    o_ref[...] += _dot((u * u).astype(BF16), wd_ref[...])


def _post0_mlp_kernel(x_ref, yf_ref, yb_ref, g_ref, bon_ref, att_ref, ln_ref, wo_ref, ones_ref, gain_ref,
                      gain_next_ref, wu_ref, wd_ref, o_ref, hn_next_ref, hn_ref):
    @pl.when(pl.program_id(1) == 0)
    def _():
        ones = ones_ref[...]
        y = yf_ref[...] + yb_ref[...]
        mean = _seg_sum(y, ones) * (1.0 / HEAD)
        d = y - mean
        var = _seg_sum(d * d, ones) * (1.0 / HEAD)
        yn = d * lax.rsqrt(var + GN_EPS) * ln_ref[0:1] + ln_ref[1:2]
        ya = ((yn + bon_ref[...]) * g_ref[...]).astype(BF16)
        mix = _dot(ya, wo_ref[0:512, :]) + _dot(att_ref[...].astype(BF16), wo_ref[512:1024, :])
        x1 = x_ref[...] + mix
        o_ref[...] = x1
        hn_ref[...] = _rms(x1, gain_ref[...]).astype(BF16)

    _mlp_step(hn_ref, wu_ref, wd_ref, o_ref)

    @pl.when(pl.program_id(1) == pl.num_programs(1) - 1)
    def _():
        hn_next_ref[...] = _rms(o_ref[...], gain_next_ref[...])


def _mlp_specs(tm):
    row = lambda i, j: (i, 0)
    const = lambda i, j: (0, 0)
    mode = pl.Buffered(1) if MLP_FF_CHUNK == D_FF else None
    weights = [pl.BlockSpec((D_MODEL, MLP_FF_CHUNK), lambda i, j: (0, j), pipeline_mode=mode),
               pl.BlockSpec((MLP_FF_CHUNK, D_MODEL), lambda i, j: (j, 0), pipeline_mode=mode)]
    return row, const, weights


def _post0_mlp(x2d, yf, yb, g, bon, att, ln, wo, ones_bd, gain, gain_next, w_up, w_down):
    N = x2d.shape[0]
    tm = min(MLP_TILE, N)
    row, const, weights = _mlp_specs(tm)
    wide = pl.BlockSpec((tm, D_MODEL), row)
    vec = pl.BlockSpec((1, D_MODEL), const)
    shp = jax.ShapeDtypeStruct(x2d.shape, F32)
    return pl.pallas_call(
        _post0_mlp_kernel,
        grid=(N // tm, D_FF // MLP_FF_CHUNK),
        in_specs=[wide] + [pl.BlockSpec((tm, 512), row)] * 5 + [
            pl.BlockSpec((8, 512), const), pl.BlockSpec((D_MODEL, D_MODEL), const), pl.BlockSpec((512, 512), const),
            vec, vec] + weights,
        out_specs=[wide, wide],
        out_shape=[shp, shp],
        scratch_shapes=[pltpu.VMEM((tm, D_MODEL), BF16)],
        compiler_params=_params(("parallel", "arbitrary")),
        name="post0_mlp",
    )(x2d, yf, yb, g, bon, att, ln, wo, ones_bd, gain, gain_next, w_up, w_down)


def _cmul_add(ar, ai, br, bi, cr, ci):
    return ar * br - ai * bi + cr, ar * bi + ai * br + ci


S5_SEGMENTS = 8


def _s5_scan(zs_ref, ps_ref, lam, pw_ref, fwd, nc):
    nl = zs_ref.shape[1]
    W = nl * 128
    ls = nc // S5_SEGMENTS
    ns = S5_SEGMENTS

    def gather(c, k):
        return jnp.concatenate([zs_ref[c, l, pl.ds(k, ns, stride=ls), :] for l in range(nl)], axis=1)

    def load(c, k):
        return jnp.concatenate([ps_ref[c, l, k * ns:(k + 1) * ns, :] for l in range(nl)], axis=1)

    def store(c, k, v):
        for l in range(nl):
            ps_ref[c, l, k * ns:(k + 1) * ns, :] = v[:, l * 128:(l + 1) * 128]

    xr = xi = jnp.zeros((ns, W), F32)
    for k in (range(ls) if fwd else range(ls - 1, -1, -1)):
        store(0, k, xr)
        store(1, k, xi)
        xr, xi = _cmul_add(lam[0:1], lam[1:2], xr, xi, gather(0, k), gather(1, k))
    cr = ci = jnp.zeros((1, W), F32)
    rows_r = [None] * S5_SEGMENTS
    rows_i = [None] * S5_SEGMENTS
    for s in (range(S5_SEGMENTS) if fwd else range(S5_SEGMENTS - 1, -1, -1)):
        rows_r[s] = cr
        rows_i[s] = ci
        cr, ci = _cmul_add(lam[2:3], lam[3:4], cr, ci, xr[s:s + 1], xi[s:s + 1])
    car_r = jnp.concatenate(rows_r, axis=0)
    car_i = jnp.concatenate(rows_i, axis=0)
    for k in range(ls):
        pr, pi = _cmul_add(pw_ref[0, k:k + 1, :], pw_ref[1, k:k + 1, :], car_r, car_i, load(0, k), load(1, k))
        store(0, k, pr)
        store(1, k, pi)


def _s5_kernel(x_ref, g2_ref, wz_ref, w2_ref, lam_ref, pw_ref, y_ref, lhs_ref, zs_ref, ps_ref, *, nc):
    C = S5_CHUNK
    W = S5_SLAB * S5_STATE
    nl = W // 128
    for j in range(C):
        lhs_ref[:, j * 128:(j + 1) * 128] = x_ref[pl.ds(j, nc, stride=C), :].astype(BF16)
    lhs = lhs_ref[...]
    for d in range(2):
        z = _dot(lhs, wz_ref[0, :, d * 2 * W:(d + 1) * 2 * W])
        for c in range(2):
            for l in range(nl):
                zs_ref[d, c, l] = z[:, c * W + l * 128:c * W + (l + 1) * 128]
        _s5_scan(zs_ref.at[d], ps_ref.at[d], lam_ref[0, d], pw_ref.at[0, d], d == 0, nc)
    ls = nc // S5_SEGMENTS

    def chunk_order(ref):
        return jnp.concatenate([ref[pl.ds(s, ls, stride=S5_SEGMENTS), :] for s in range(S5_SEGMENTS)], axis=0)

    pv = jnp.concatenate([chunk_order(ps_ref.at[d, c, l]) for d in range(2) for c in range(2) for l in range(nl)],
                         axis=1).astype(BF16)
    for i in range(0, C, 2):
        w_loc = g2_ref[0, (C - 1 - i) * 128:(2 * C - 1 - i) * 128, :]
        y2 = _dot(lhs, w_loc) + _dot(pv, w2_ref[0, :, i * 128:(i + 2) * 128])
        y_ref[pl.ds(i, nc, stride=C), :] = y2[:, :128]
        y_ref[pl.ds(i + 1, nc, stride=C), :] = y2[:, 128:]


def _s5_core(hn, g2, wz, w2, lam, pw, n_seq, T):
    nc = T // S5_CHUNK
    ls = nc // S5_SEGMENTS
    W = S5_SLAB * S5_STATE
    n_slab = D_MODEL // 128
    once = pl.Buffered(1)
    slab = lambda c, s: (c, 0, 0)
    return pl.pallas_call(
        functools.partial(_s5_kernel, nc=nc),
        grid=(n_slab, n_seq),
        in_specs=[
            pl.BlockSpec((T, 128), lambda c, s: (s, c)),
            pl.BlockSpec((1, 2 * S5_CHUNK * 128, 256), slab, pipeline_mode=once),
            pl.BlockSpec((1, S5_CHUNK * 128, 4 * W), slab, pipeline_mode=once),
            pl.BlockSpec((1, 4 * W, S5_CHUNK * 128), slab, pipeline_mode=once),
            pl.BlockSpec((1, 2, 8, W), lambda c, s: (c, 0, 0, 0)),
            pl.BlockSpec((1, 2, 2, ls, W), lambda c, s: (c, 0, 0, 0, 0)),
        ],
        out_specs=pl.BlockSpec((T, 128), lambda c, s: (s, c)),
        out_shape=jax.ShapeDtypeStruct(hn.shape, F32),
        scratch_shapes=[
            pltpu.VMEM((nc, S5_CHUNK * 128), BF16),
            pltpu.VMEM((2, 2, W // 128, nc, 128), F32), pltpu.VMEM((2, 2, W // 128, nc, 128), F32)],
        compiler_params=_params(("parallel", "arbitrary")),
        name="s5_core",
    )(hn, g2, wz, w2, lam, pw)


def _post1_mlp_kernel(x_ref, ys_ref, gain_ref, d_ref, wg_ref, bg_ref, gain2_ref, wu_ref, wd_ref, o_ref, hn_ref):
    @pl.when(pl.program_id(1) == 0)
    def _():
        x = x_ref[...]
        y = _rms(x, gain_ref[...]) * d_ref[...] + ys_ref[...]
        z = 0.5 * y * (1.0 + jnp.tanh(math.sqrt(2.0 / math.pi) * (y + 0.044715 * (y * y * y))))
        gate = _sigmoid(_dot(z.astype(BF16), wg_ref[...]) + bg_ref[...])
        x1 = x + z * gate
        o_ref[...] = x1
        hn_ref[...] = _rms(x1, gain2_ref[...]).astype(BF16)

    _mlp_step(hn_ref, wu_ref, wd_ref, o_ref)


def _post1_mlp(x2d, ys, gain, d, wg, bg, gain2, w_up, w_down):
    N = x2d.shape[0]
    tm = min(MLP_TILE, N)
    row, const, weights = _mlp_specs(tm)
    wide = pl.BlockSpec((tm, D_MODEL), row)
    vec = pl.BlockSpec((1, D_MODEL), const)
    return pl.pallas_call(
        _post1_mlp_kernel,
        grid=(N // tm, D_FF // MLP_FF_CHUNK),
        in_specs=[wide, wide, vec, vec, pl.BlockSpec((D_MODEL, D_MODEL), const), vec, vec] + weights,
        out_specs=wide,
        out_shape=jax.ShapeDtypeStruct(x2d.shape, F32),
        scratch_shapes=[pltpu.VMEM((tm, D_MODEL), BF16)],
        compiler_params=_params(("parallel", "arbitrary")),
        name="post1_mlp",
    )(x2d, ys, gain, d, wg, bg, gain2, w_up, w_down)


def _s5_tables(p, T):
    C, G, P, SL = S5_CHUNK, S5_GROUPS, S5_STATE, S5_SLAB
    ns = G // SL
    ls = T // C // S5_SEGMENTS
    eye = jnp.eye(SL, dtype=F32)
    b_re = p['s5_b_re'][0].astype(F32)
    b_im = p['s5_b_im'][0].astype(F32)
    steps = jnp.arange(C, dtype=F32)

    def direction(sfx):
        lr = p['s5_lam_re_' + sfx][0].astype(F32)
        li = p['s5_lam_im_' + sfx][0].astype(F32)
        dt = jnp.exp(p['s5_log_dt_' + sfx][0].astype(F32))[:, None]

        def power(k):
            k = k[:, None, None]
            mag = jnp.exp(lr * dt * k)
            return mag * jnp.cos(li * dt * k), mag * jnp.sin(li * dt * k)

        l1r, l1i = power(jnp.ones((1,), F32))
        nr, ni = l1r[0] - 1.0, l1i[0]
        den = lr * lr + li * li
        cr = (nr * lr + ni * li) / den
        ci = (ni * lr - nr * li) / den
        cb_r = cr[:, :, None] * b_re - ci[:, :, None] * b_im
        cb_i = cr[:, :, None] * b_im + ci[:, :, None] * b_re
        c_r = p['s5_c_re_' + sfx][0].astype(F32)
        c_i = p['s5_c_im_' + sfx][0].astype(F32)
        return power, cb_r, cb_i, c_r, c_i

    def kernels(power, cb_r, cb_i, c_r, c_i):
        pr, pi = power(steps)
        d_r = pr[..., None] * cb_r - pi[..., None] * cb_i
        d_i = pr[..., None] * cb_i + pi[..., None] * cb_r
        k = jnp.einsum('gop,lgpi->lgio', c_r, d_r) - jnp.einsum('gop,lgpi->lgio', c_i, d_i)
        k = k.reshape(C, ns, SL, S5_GROUP, 1, S5_GROUP)
        return jnp.where(eye[:, None, :, None] > 0, k, 0.0).reshape(C, ns, 128, 128)

    def state_in(power, cb_r, cb_i, ks):
        pr, pi = power(ks)
        w_r = (pr[..., None] * cb_r - pi[..., None] * cb_i).reshape(C, ns, SL, P, S5_GROUP)
        w_i = (pr[..., None] * cb_i + pi[..., None] * cb_r).reshape(C, ns, SL, P, S5_GROUP)
        return [w_r, w_i]

    def state_out(power, c_r, c_i, ks):
        pr, pi = power(ks)
        e_r = (c_r[None] * pr[:, :, None, :] - c_i[None] * pi[:, :, None, :]).reshape(C, ns, SL, S5_GROUP, P)
        e_i = (c_r[None] * pi[:, :, None, :] + c_i[None] * pr[:, :, None, :]).reshape(C, ns, SL, S5_GROUP, P)
        return [e_r, -e_i]

    def scan_tables(power, ks_rows):
        lr_, li_ = power(C * jnp.array([1.0, ls], F32))
        lam = jnp.stack([lr_, li_], axis=1).reshape(4, ns, SL * P)
        lam = jnp.concatenate([lam, jnp.zeros_like(lam)], axis=0).transpose(1, 0, 2)
        pr, pi = power(C * ks_rows)
        pw = jnp.stack([pr.reshape(ls, ns, SL * P), pi.reshape(ls, ns, SL * P)], axis=0).transpose(2, 0, 1, 3)
        return lam, pw

    def expand(x, row_group, col_group):
        src = jnp.arange(2048) // (col_group * SL) * col_group + jnp.arange(2048) % col_group
        spread = (jnp.arange(256)[:, None] == src[None, :]).astype(BF16)
        full = jnp.einsum('srk,kc->src', x.astype(BF16), spread, preferred_element_type=F32)
        same = ((jnp.arange(2048) // row_group) % SL)[:, None] == ((jnp.arange(2048) // col_group) % SL)[None, :]
        return jnp.where(same, full, 0.0).astype(BF16)

    pf = direction('f')
    pb = direction('b')
    kf = kernels(*pf)
    kb = kernels(*pb)
    zero = jnp.zeros((1, ns, 128, 128), F32)
    gen = jnp.concatenate([kf[:0:-1], (kf[0] + kb[0])[None], kb[1:], zero], axis=0)
    gen_prev = jnp.concatenate([zero, gen[:-1]], axis=0)
    g2 = jnp.concatenate([gen, gen_prev], axis=-1).transpose(1, 0, 2, 3).reshape(ns, 2 * C * 128, 256)

    w4 = jnp.stack(state_in(pf[0], pf[1], pf[2], (C - 1) - steps) + state_in(pb[0], pb[1], pb[2], steps))
    wz = expand(w4.transpose(2, 1, 3, 5, 0, 4).reshape(ns, C * 128, 4 * P), S5_GROUP, P)
    e4 = jnp.stack(state_out(pf[0], pf[3], pf[4], steps + 1.0) + state_out(pb[0], pb[3], pb[4], C - steps))
    w2 = expand(e4.transpose(2, 0, 3, 5, 1, 4).reshape(ns, 4 * SL * P, C * S5_GROUP), P, S5_GROUP)
    pos = jnp.arange(ls, dtype=F32)
    lam_f, pw_f = scan_tables(pf[0], pos)
    lam_b, pw_b = scan_tables(pb[0], (ls - 1.0) - pos)
    lam = jnp.stack([lam_f, lam_b], axis=1)
    pw = jnp.stack([pw_f, pw_b], axis=1)
    return g2.astype(BF16), wz, w2, lam, pw


def _rope_tables(T):
    rows = T // GRID_W
    row_ids = np.repeat(np.arange(rows, dtype=np.float32), GRID_W)
    col_ids = np.tile(np.arange(GRID_W, dtype=np.float32), rows)
    pairs = HEAD // 4
    inv_freq = (np.float32(ROPE_THETA) ** (-np.arange(pairs, dtype=np.float32) / pairs)).astype(np.float32)
    ang = np.concatenate([row_ids[:, None] * inv_freq, col_ids[:, None] * inv_freq], axis=-1)
    cos = np.repeat(np.cos(ang), 2, axis=-1)
    sin = np.repeat(np.sin(ang), 2, axis=-1)
    sign = np.tile(np.array([-1.0, 1.0], np.float32), HEAD // 2)
    return jnp.asarray(np.tile(cos, (1, 2)), F32), jnp.asarray(np.tile(sin * sign, (1, 2)), F32)


def _layer0_weights(p, T):
    w_in = p['hyb_w_in'][0]
    zc = jnp.zeros((D_MODEL, 64), F32)
    w_all = jnp.concatenate([w_in[:, 0:1728], zc, w_in[:, 1728:2624]], axis=1).astype(BF16)
    mu = p['hyb_shift_mu'][0]
    mu_all = jnp.concatenate([mu[0:1728], jnp.zeros((64,), F32), mu[1728:1856]]).reshape(1, RW_COLS)
    wup = jnp.zeros((384, 2048), F32)
    wup = wup.at[0:64, 0:512].set(p['rwkv_w_up_f'][0])
    wup = wup.at[64:128, 512:1024].set(p['rwkv_w_up_b'][0])
    wup = wup.at[128:192, 1024:1536].set(p['rwkv_a_up'][0])
    wup = wup.at[256:384, 1536:2048].set(p['rwkv_g_up'][0])
    wup = wup.astype(BF16)
    zr = jnp.zeros((512,), F32)
    par = jnp.stack([p['rwkv_w0_f'][0], p['rwkv_w0_b'][0], p['rwkv_a0'][0], p['rwkv_k_k'][0],
                     p['rwkv_k_a'][0], p['rwkv_r_k'][0].reshape(-1), zr, zr]).astype(F32)
    qg = jnp.tile(p['att_q_norm'][0], 8).reshape(1, 512).astype(F32)
    kg = jnp.tile(p['att_k_norm'][0], 2).reshape(1, 128).astype(F32)
    cos_t, sin_t = _rope_tables(T)
    seg = jnp.arange(512) // HEAD
    ones_bd = (seg[:, None] == seg[None, :]).astype(BF16)
    ln = jnp.stack([p['rwkv_lnx_g'][0], p['rwkv_lnx_b'][0]] + [zr] * 6).astype(F32)
    return dict(w_all=w_all, mu_all=mu_all, wup=wup, par=par, qg=qg, kg=kg, cos_t=cos_t, sin_t=sin_t,
                ones_bd=ones_bd, ln=ln, wo=p['hyb_w_out'][0].astype(BF16))


def _row(v):
    return v.reshape(1, -1).astype(F32)


def _layer0(x2d, p, w, ffn_w, n_seq, T):
    r, k, v, a, b, lf, lb, g, bon, q, k_rep, v_t = _pre0(
        x2d, T, _row(p['mix_norm'][0]), w['w_all'], w['mu_all'], w['wup'], w['par'], w['qg'], w['kg'],
        w['cos_t'], w['sin_t'], w['ones_bd'])
    yf, yb = _wkv(r, k, v, a, b, lf, lb, n_seq, T)
    att = _attention(q, k_rep, v_t, n_seq, T)
    return _post0_mlp(x2d, yf, yb, g, bon, att, w['ln'], w['wo'], w['ones_bd'], _row(p['ffn_norm'][0]),
                      _row(p['mix_norm'][1]), ffn_w[0][0], ffn_w[0][1])


def _layer1(x2d, hn, p, s5, ffn_w, n_seq, T):
    g2, wz, w2, lam, pw = s5
    ys = _s5_core(hn, g2, wz, w2, lam, pw, n_seq, T)
    return _post1_mlp(x2d, ys, _row(p['mix_norm'][1]), _row(p['s5_d'][0]), p['s5_glu_w'][0].astype(BF16),
                      _row(p['s5_glu_b'][0]), _row(p['ffn_norm'][1]), ffn_w[1][0], ffn_w[1][1])


def _prepare(p, T):
    ffn_w = [(p['ffn_up'][l].astype(BF16), p['ffn_down'][l].astype(BF16)) for l in range(2)]
    return _layer0_weights(p, T), _s5_tables(p, T), ffn_w


def _trunk(x, p, prep=None):
    n_seq, T, _ = x.shape
    w0, s5, ffn_w = _prepare(p, T) if prep is None else prep
    x2d = x.reshape(n_seq * T, D_MODEL)
    x2d, hn = _layer0(x2d, p, w0, ffn_w, n_seq, T)
    x2d = _layer1(x2d, hn, p, s5, ffn_w, n_seq, T)
    return x2d.reshape(n_seq, T, D_MODEL)


def kernel(x_prompt, x_sample, mix_norm, ffn_norm, ffn_up, ffn_down, hyb_w_in, hyb_shift_mu, rwkv_w0_f, rwkv_w_up_f, rwkv_w0_b, rwkv_w_up_b, rwkv_a0, rwkv_a_up, rwkv_g_up, rwkv_k_k, rwkv_k_a, rwkv_r_k, rwkv_lnx_g, rwkv_lnx_b, att_q_norm, att_k_norm, hyb_w_out, s5_lam_re_f, s5_lam_im_f, s5_log_dt_f, s5_lam_re_b, s5_lam_im_b, s5_log_dt_b, s5_b_re, s5_b_im, s5_c_re_f, s5_c_im_f, s5_c_re_b, s5_c_im_b, s5_d, s5_glu_w, s5_glu_b):
    p = dict(mix_norm=mix_norm, ffn_norm=ffn_norm, ffn_up=ffn_up, ffn_down=ffn_down,
             hyb_w_in=hyb_w_in, hyb_shift_mu=hyb_shift_mu,
             rwkv_w0_f=rwkv_w0_f, rwkv_w_up_f=rwkv_w_up_f, rwkv_w0_b=rwkv_w0_b, rwkv_w_up_b=rwkv_w_up_b,
             rwkv_a0=rwkv_a0, rwkv_a_up=rwkv_a_up, rwkv_g_up=rwkv_g_up,
             rwkv_k_k=rwkv_k_k, rwkv_k_a=rwkv_k_a, rwkv_r_k=rwkv_r_k,
             rwkv_lnx_g=rwkv_lnx_g, rwkv_lnx_b=rwkv_lnx_b,
             att_q_norm=att_q_norm, att_k_norm=att_k_norm, hyb_w_out=hyb_w_out,
             s5_lam_re_f=s5_lam_re_f, s5_lam_im_f=s5_lam_im_f, s5_log_dt_f=s5_log_dt_f,
             s5_lam_re_b=s5_lam_re_b, s5_lam_im_b=s5_lam_im_b, s5_log_dt_b=s5_log_dt_b,
             s5_b_re=s5_b_re, s5_b_im=s5_b_im,
             s5_c_re_f=s5_c_re_f, s5_c_im_f=s5_c_im_f, s5_c_re_b=s5_c_re_b, s5_c_im_b=s5_c_im_b,
             s5_d=s5_d, s5_glu_w=s5_glu_w, s5_glu_b=s5_glu_b)
    assert x_prompt.shape[1] == x_sample.shape[1]
    prep = _prepare(p, x_prompt.shape[1])
    return (_trunk(x_prompt, p, prep), _trunk(x_sample, p, prep))
```
